```python
import jax, jax.numpy as jnp
from jax import lax
import numpy as np

D_MODEL = 2048
BATCH = 4
SEQ = 2048
DEPTH = 2
DEC_BATCH = 128
DEC_SEQ = 8
PAST_LEN = 16384
PAGE_SIZE = 128

N_MIXERS = 2
N_A = (DEPTH + 1) // 2
N_B = DEPTH // 2
HG_HEADS = 16
HG_DK = 128
HG_DV = D_MODEL // HG_HEADS
HG_FDIM = HG_HEADS * HG_DK
HG_CHUNK = 16
POOL_WINDOWS = (2, 4, 8, 16)
POOL_GROUPS = len(POOL_WINDOWS)
POOL_GC = D_MODEL // POOL_GROUPS
POOL_BUF = max(POOL_WINDOWS) - 1
N_MEM = 256
MEM_HEADS = 4
MEM_HD = D_MODEL // MEM_HEADS
D_FF = 4 * D_MODEL
EPS = 1e-6

kernel_name = "hgrn2_pool_hybrid_decode_step"


def rms_norm(x, w):
    xf = x.astype(jnp.float32)
    y = xf * lax.rsqrt(jnp.mean(xf * xf, axis=-1, keepdims=True) + EPS)
    return (y * w.astype(jnp.float32)).astype(x.dtype)


def lower_bounds(lb_logits):
    sm = jax.nn.softmax(lb_logits.astype(jnp.float32), axis=0)
    return jnp.cumsum(sm, axis=0)


def chunk_gla(q, k, v, logf, S0):
    B, L, H, DK = q.shape
    DV = v.shape[-1]
    C = HG_CHUNK if L % HG_CHUNK == 0 else L
    n = L // C

    def blocks(a):
        return a.reshape(B, n, C, H, a.shape[-1]).swapaxes(0, 1)

    causal = jnp.tril(jnp.ones((C, C), dtype=bool))

    def step(S, inp):
        qc, kc, vc, gc = inp
        b = jnp.cumsum(gc, axis=1)
        b_last = b[:, -1]
        q_dec = qc * jnp.exp(b)
        k_inv = kc * jnp.exp(-b)
        A = jnp.einsum('bthk,bshk->bhts', q_dec, k_inv)
        A = jnp.where(causal, A, 0.0)
        o = jnp.einsum('bhts,bshv->bthv', A, vc) + jnp.einsum('bthk,bhkv->bthv', q_dec, S)
        k_end = kc * jnp.exp(b_last[:, None] - b)
        S = S * jnp.exp(b_last)[..., None] + jnp.einsum('bshk,bshv->bhkv', k_end, vc)
        return S, o

    S, o = lax.scan(step, S0, (blocks(q), blocks(k), blocks(v), blocks(logf)))
    return o.swapaxes(0, 1).reshape(B, L, H, DV), S


def hgrn2_mixer(h, S0, w_in, lb, norm_w, w_out):
    B, L, _ = h.shape
    proj = h @ w_in
    q = proj[..., :HG_FDIM]
    fl = proj[..., HG_FDIM:2 * HG_FDIM]
    i = proj[..., 2 * HG_FDIM:2 * HG_FDIM + D_MODEL]
    g = proj[..., 2 * HG_FDIM + D_MODEL:]
    q = jax.nn.silu(q.astype(jnp.float32)).reshape(B, L, HG_HEADS, HG_DK)
    f = lb + (1.0 - lb) * jax.nn.sigmoid(fl.astype(jnp.float32))
    f = f.reshape(B, L, HG_HEADS, HG_DK)
    k = 1.0 - f
    v = i.astype(jnp.float32).reshape(B, L, HG_HEADS, HG_DV)
    o, S = chunk_gla(q, k, v, jnp.log(f), S0.astype(jnp.float32))
    o = o * lax.rsqrt(jnp.mean(o * o, axis=-1, keepdims=True) + EPS) * norm_w.astype(jnp.float32)
    o = o * jax.nn.silu(g.astype(jnp.float32).reshape(B, L, HG_HEADS, HG_DV))
    return o.reshape(B, L, D_MODEL).astype(h.dtype) @ w_out, S


def pool_mixer(h, buf, pos0, w_in, pool_w, pool_scale, w_out):
    B, L, _ = h.shape
    u = h @ w_in
    ext = jnp.concatenate([buf.astype(u.dtype), u], axis=1)
    cs = jnp.cumsum(ext.astype(jnp.float32), axis=1)
    cs = jnp.concatenate([jnp.zeros((B, 1, D_MODEL), jnp.float32), cs], axis=1)
    pos = pos0 + jnp.arange(L)
    end = cs[:, POOL_BUF + 1:POOL_BUF + 1 + L]
    means = []
    for gi, w in enumerate(POOL_WINDOWS):
        sl = slice(gi * POOL_GC, (gi + 1) * POOL_GC)
        start = cs[:, POOL_BUF + 1 - w:POOL_BUF + 1 - w + L, sl]
        cnt = jnp.minimum(w, pos + 1).astype(jnp.float32)
        means.append((end[..., sl] - start) / cnt[None, :, None])
    pooled = jnp.concatenate(means, axis=-1) - u.astype(jnp.float32)
    pooled = pooled.reshape(B, L, POOL_GROUPS, POOL_GC)
    mixed = jnp.einsum('blgc,gcd->blgd', pooled, pool_w.astype(jnp.float32)).reshape(B, L, D_MODEL)
    mixed = (mixed * pool_scale.astype(jnp.float32)).astype(h.dtype)
    return mixed @ w_out, ext[:, -POOL_BUF:]


def mem_kv(mem, norm_mem, w_xkv):
    B = mem.shape[0]
    kv = rms_norm(mem, norm_mem) @ w_xkv
    k = kv[..., :D_MODEL].reshape(B, N_MEM, MEM_HEADS, MEM_HD)
    v = kv[..., D_MODEL:].reshape(B, N_MEM, MEM_HEADS, MEM_HD)
    return k, v


def mem_cross_attn(h, k, v, w_xq, w_xo):
    B, L, _ = h.shape
    q = (h @ w_xq).reshape(B, L, MEM_HEADS, MEM_HD).astype(jnp.float32)
    s = jnp.einsum('blhd,bmhd->bhlm', q, k.astype(jnp.float32)) * (MEM_HD ** -0.5)
    p = jax.nn.softmax(s, axis=-1)
    o = jnp.einsum('bhlm,bmhd->blhd', p, v.astype(jnp.float32))
    return o.reshape(B, L, D_MODEL).astype(h.dtype) @ w_xo


def trunk(x, pos0, S_in, buf_in, mk, mv, p):
    lbs = lower_bounds(p['hg_lb_logits'])
    S_out, buf_out = [], []
    for l in range(DEPTH):
        j = l // N_MIXERS
        hn = rms_norm(x, p['norm_mix_pre'][l])
        if l % N_MIXERS == 0:
            m, S = hgrn2_mixer(hn, S_in[j], p['w_in_a'][j], lbs[l], p['hg_norm'][j], p['w_out_a'][j])
            S_out.append(S)
        else:
            m, b = pool_mixer(hn, buf_in[j], pos0, p['w_in_b'][j], p['pool_w'][j],
                              p['pool_scale'][j], p['w_out_b'][j])
            buf_out.append(b)
        x = x + rms_norm(m, p['norm_mix_post'][l])
        hn = rms_norm(x, p['norm_x_pre'][l])
        a = mem_cross_attn(hn, mk[l], mv[l], p['w_xq'][l], p['w_xo'][l])
        x = x + rms_norm(a, p['norm_x_post'][l])
        hn = rms_norm(x, p['norm_mlp_pre'][l])
        f = jnp.square(jax.nn.relu(hn @ p['w_up'][l])) @ p['w_down'][l]
        x = x + rms_norm(f, p['norm_mlp_post'][l])
    return x, jnp.stack(S_out), jnp.stack(buf_out)


def setup_inputs(seed: int = 0) -> dict:
    key = jax.random.key(seed)
    ks = jax.random.split(key, 32)
    f32 = jnp.float32

    def nrm(k, shape, scale):
        return scale * jax.random.normal(k, shape, f32)

    def gain(k, shape):
        return 1.0 + 0.05 * jax.random.normal(k, shape, f32)

    sD = D_MODEL ** -0.5
    return {
        "x_prompt": nrm(ks[0], (BATCH, SEQ, D_MODEL), 1.0),
        "x_sample": nrm(ks[1], (DEC_BATCH, DEC_SEQ, D_MODEL), 1.0),
        "state_hgrn": nrm(ks[2], (N_A, DEC_BATCH, HG_HEADS, HG_DK, HG_DV), 0.5),
        "state_pool": nrm(ks[3], (N_B, DEC_BATCH, POOL_BUF, D_MODEL), 1.0),
        "cache_mem_k": nrm(ks[4], (DEPTH, DEC_BATCH, N_MEM, MEM_HEADS, MEM_HD), 1.0),
        "cache_mem_v": nrm(ks[5], (DEPTH, DEC_BATCH, N_MEM, MEM_HEADS, MEM_HD), 1.0),
        "mem_prompt": nrm(ks[6], (BATCH, N_MEM, D_MODEL), 1.0),
        "w_in_a": nrm(ks[7], (N_A, D_MODEL, 2 * HG_FDIM + 2 * D_MODEL), sD),
        "hg_lb_logits": nrm(ks[8], (DEPTH + 1, HG_FDIM), 0.1),
        "hg_norm": gain(ks[9], (N_A, HG_DV)),
        "w_out_a": nrm(ks[10], (N_A, D_MODEL, D_MODEL), sD),
        "w_in_b": nrm(ks[11], (N_B, D_MODEL, D_MODEL), sD),
        "pool_w": nrm(ks[12], (N_B, POOL_GROUPS, POOL_GC, POOL_GC), POOL_GC ** -0.5),
        "pool_scale": 1.0 + 0.1 * jax.random.normal(ks[13], (N_B, D_MODEL), f32),
        "w_out_b": nrm(ks[14], (N_B, D_MODEL, D_MODEL), sD),
        "norm_mem": gain(ks[15], (DEPTH, D_MODEL)),
        "w_xq": nrm(ks[16], (DEPTH, D_MODEL, D_MODEL), sD),
        "w_xkv": nrm(ks[17], (DEPTH, D_MODEL, 2 * D_MODEL), sD),
        "w_xo": nrm(ks[18], (DEPTH, D_MODEL, D_MODEL), sD),
        "norm_mix_pre": gain(ks[19], (DEPTH, D_MODEL)),
        "norm_mix_post": gain(ks[20], (DEPTH, D_MODEL)),
        "norm_x_pre": gain(ks[21], (DEPTH, D_MODEL)),
        "norm_x_post": gain(ks[22], (DEPTH, D_MODEL)),
        "norm_mlp_pre": gain(ks[23], (DEPTH, D_MODEL)),
        "norm_mlp_post": gain(ks[24], (DEPTH, D_MODEL)),
        "w_up": nrm(ks[25], (DEPTH, D_MODEL, D_FF), sD),
        "w_down": nrm(ks[26], (DEPTH, D_FF, D_MODEL), D_FF ** -0.5),
    }


def reference(x_prompt, x_sample, state_hgrn, state_pool, cache_mem_k, cache_mem_v, mem_prompt,
              w_in_a, hg_lb_logits, hg_norm, w_out_a, w_in_b, pool_w, pool_scale, w_out_b,
              norm_mem, w_xq, w_xkv, w_xo, norm_mix_pre, norm_mix_post, norm_x_pre, norm_x_post,
              norm_mlp_pre, norm_mlp_post, w_up, w_down):
    p = dict(w_in_a=w_in_a, hg_lb_logits=hg_lb_logits, hg_norm=hg_norm, w_out_a=w_out_a,
             w_in_b=w_in_b, pool_w=pool_w, pool_scale=pool_scale, w_out_b=w_out_b,
             w_xq=w_xq, w_xo=w_xo, norm_mix_pre=norm_mix_pre, norm_mix_post=norm_mix_post,
             norm_x_pre=norm_x_pre, norm_x_post=norm_x_post, norm_mlp_pre=norm_mlp_pre,
             norm_mlp_post=norm_mlp_post, w_up=w_up, w_down=w_down)
    mk_list, mv_list = [], []
    for l in range(DEPTH):
        k, v = mem_kv(mem_prompt, norm_mem[l], w_xkv[l])
        mk_list.append(k)
        mv_list.append(v)
    cache_mem_k_prompt = jnp.stack(mk_list)
    cache_mem_v_prompt = jnp.stack(mv_list)
    S0 = jnp.zeros((N_A, BATCH, HG_HEADS, HG_DK, HG_DV), jnp.float32)
    buf0 = jnp.zeros((N_B, BATCH, POOL_BUF, D_MODEL), x_prompt.dtype)
    y_prompt, state_hgrn_prompt, state_pool_prompt = trunk(
        x_prompt, 0, S0, buf0, cache_mem_k_prompt, cache_mem_v_prompt, p)
    y_sample, state_hgrn_sample, state_pool_sample = trunk(
        x_sample, PAST_LEN, state_hgrn, state_pool, cache_mem_k, cache_mem_v, p)
    return (y_prompt, y_sample, state_hgrn_prompt, state_pool_prompt, cache_mem_k_prompt,
            cache_mem_v_prompt, state_hgrn_sample, state_pool_sample)
```

```python
import functools

import jax
import jax.numpy as jnp
from jax import lax
from jax.experimental import pallas as pl
from jax.experimental.pallas import tpu as pltpu

F32 = jnp.float32
BF16 = jnp.bfloat16
EPS = 1e-6
PAST_LEN = 16384
POOL_WINDOWS = (2, 4, 8, 16)
HEAD_DIM = 128
GLA_BLOCK = 128
GLA_SUB = 16
HALO = 16

V7X_VMEM_BYTES = 64 * 1024 * 1024
VMEM_CAP = V7X_VMEM_BYTES - 6 * 1024 * 1024


def _cparams(sem, vmem_bytes):
    return pltpu.CompilerParams(dimension_semantics=sem,
                                vmem_limit_bytes=int(min(max(vmem_bytes, 16 * 1024 * 1024), VMEM_CAP)))


def _nbytes(shape, dtype):
    n = 1
    for s in shape:
        n *= s
    return n * jnp.dtype(dtype).itemsize


def _rms(v, w):
    ms = jnp.mean(v * v, axis=-1, keepdims=True)
    return v * lax.rsqrt(ms + EPS) * w


def _sigmoid(x):
    return 1.0 / (1.0 + jnp.exp(-x))


def _dot(a, b):
    return jnp.dot(a, b, preferred_element_type=F32)


def _dot_nt(a, b):
    return lax.dot_general(a, b, (((1,), (1,)), ((), ())), preferred_element_type=F32)


def _dot_tn(a, b, precision=None):
    return lax.dot_general(a, b, (((0,), (0,)), ((), ())), preferred_element_type=F32,
                           precision=precision)


def _rmsnorm_kernel(x_ref, w_ref, o_ref):
    o_ref[...] = _rms(x_ref[...], w_ref[...]).astype(o_ref.dtype)


def _rmsnorm(x, w, tm=512):
    M, D = x.shape
    tm = min(tm, M)
    return pl.pallas_call(
        _rmsnorm_kernel,
        grid=(M // tm,),
        in_specs=[pl.BlockSpec((tm, D), lambda i: (i, 0)),
                  pl.BlockSpec((1, D), lambda i: (0, 0))],
        out_specs=pl.BlockSpec((tm, D), lambda i: (i, 0)),
        out_shape=jax.ShapeDtypeStruct((M, D), BF16),
        compiler_params=_cparams(("parallel",), 4 * _nbytes((tm, D), F32)),
        name="rmsnorm",
    )(x, w.reshape(1, D))


def _mm_kernel(a_ref, w_ref, *o_refs):
    acc = _dot(a_ref[...], w_ref[...])
    for o_ref in o_refs:
        o_ref[...] = acc.astype(o_ref.dtype)


def _matmul(a, w, out_dtypes, tm=1024, tn=1024):
    M, K = a.shape
    N = w.shape[1]
    tm, tn = min(tm, M), min(tn, N)
    vm = 2 * (_nbytes((tm, K), BF16) + _nbytes((K, tn), BF16)) + _nbytes((tm, tn), F32)
    vm += sum(2 * _nbytes((tm, tn), d) for d in out_dtypes)
    outs = pl.pallas_call(
        _mm_kernel,
        grid=(M // tm, N // tn),
        in_specs=[pl.BlockSpec((tm, K), lambda i, j: (i, 0)),
                  pl.BlockSpec((K, tn), lambda i, j: (0, j))],
        out_specs=[pl.BlockSpec((tm, tn), lambda i, j: (i, j)) for _ in out_dtypes],
        out_shape=[jax.ShapeDtypeStruct((M, N), d) for d in out_dtypes],
        compiler_params=_cparams(("parallel", "parallel"), vm),
        name="matmul",
    )(a, w)
    return outs


def _proj_a_kernel(h_ref, wq_ref, wf_ref, wi_ref, wg_ref, lbl_ref, q_ref, k_ref, lf_ref, v_ref, g_ref,
                   *, layer):
    h = h_ref[...]
    aq = _dot(h, wq_ref[...])
    q_ref[...] = (aq * _sigmoid(aq)).astype(q_ref.dtype)
    lg = lbl_ref[...]
    e = jnp.exp(lg - jnp.max(lg, axis=0, keepdims=True))
    lb = jnp.sum(e[:layer + 1], axis=0, keepdims=True) / jnp.sum(e, axis=0, keepdims=True)
    f = lb + (1.0 - lb) * _sigmoid(_dot(h, wf_ref[...]))
    k_ref[...] = (1.0 - f).astype(k_ref.dtype)
    lf_ref[...] = jnp.log(f)
    v_ref[...] = _dot(h, wi_ref[...]).astype(v_ref.dtype)
    ag = _dot(h, wg_ref[...])
    g_ref[...] = (ag * _sigmoid(ag)).astype(g_ref.dtype)


def _proj_a(h, w, lb_logits, layer, act_dtype, tm=1024, tn=512):
    M, D = h.shape
    tm = min(tm, M)
    nj = D // tn
    nl = lb_logits.shape[0]
    wspec = lambda s: pl.BlockSpec((D, tn), lambda i, j: (0, j + s * nj))
    ospec = pl.BlockSpec((tm, tn), lambda i, j: (i, j))
    vm = 2 * (_nbytes((tm, D), BF16) + 4 * _nbytes((D, tn), BF16) + 5 * _nbytes((tm, tn), F32))
    vm += 4 * _nbytes((tm, tn), F32)
    return pl.pallas_call(
        functools.partial(_proj_a_kernel, layer=layer),
        grid=(M // tm, nj),
        in_specs=[pl.BlockSpec((tm, D), lambda i, j: (i, 0)), wspec(0), wspec(1), wspec(2), wspec(3),
                  pl.BlockSpec((nl, tn), lambda i, j: (0, j))],
        out_specs=[ospec] * 5,
        out_shape=[jax.ShapeDtypeStruct((M, D), act_dtype), jax.ShapeDtypeStruct((M, D), act_dtype),
                   jax.ShapeDtypeStruct((M, D), F32), jax.ShapeDtypeStruct((M, D), act_dtype),
                   jax.ShapeDtypeStruct((M, D), act_dtype)],
        compiler_params=_cparams(("parallel", "parallel"), vm),
        name="hgrn2_proj",
    )(h, w, w, w, w, lb_logits)


def _mm_res_kernel(a_ref, w_ref, x_ref, pw_ref, nw_ref, ox_ref, oh_ref):
    m = _dot(a_ref[...].astype(BF16), w_ref[...])
    y = x_ref[...] + _rms(m, pw_ref[...])
    ox_ref[...] = y
    oh_ref[...] = _rms(y, nw_ref[...]).astype(oh_ref.dtype)


def _mm_res(a, w, x, post_w, next_w, tm=512):
    M, K = a.shape
    D = w.shape[1]
    tm = min(tm, M)
    row = lambda: pl.BlockSpec((1, D), lambda i: (0, 0))
    vm = 2 * (_nbytes((tm, K), a.dtype) + _nbytes((K, D), BF16) + 2 * _nbytes((tm, D), F32)
              + _nbytes((tm, D), BF16)) + 2 * _nbytes((tm, D), F32)
    return pl.pallas_call(
        _mm_res_kernel,
        grid=(M // tm,),
        in_specs=[pl.BlockSpec((tm, K), lambda i: (i, 0)),
                  pl.BlockSpec((K, D), lambda i: (0, 0)),
                  pl.BlockSpec((tm, D), lambda i: (i, 0)), row(), row()],
        out_specs=[pl.BlockSpec((tm, D), lambda i: (i, 0)), pl.BlockSpec((tm, D), lambda i: (i, 0))],
        out_shape=[jax.ShapeDtypeStruct((M, D), F32), jax.ShapeDtypeStruct((M, D), BF16)],
        compiler_params=_cparams(("parallel",), vm),
        name="matmul_norm_residual",
    )(a, w, x, post_w.reshape(1, D), next_w.reshape(1, D))


def _mlp_kernel(h_ref, wu_ref, wd_ref, x_ref, pw_ref, nw_ref, ox_ref, *rest, emit_next):
    acc_ref = rest[-1]
    f = pl.program_id(1)
    u = jnp.maximum(_dot(h_ref[...], wu_ref[...]), 0.0)
    d = _dot((u * u).astype(BF16), wd_ref[...])

    @pl.when(f == 0)
    def _():
        acc_ref[...] = d

    @pl.when(f > 0)
    def _():
        acc_ref[...] += d

    @pl.when(f == pl.num_programs(1) - 1)
    def _():
        y = x_ref[...] + _rms(acc_ref[...], pw_ref[...])
        ox_ref[...] = y
        if emit_next:
            rest[0][...] = _rms(y, nw_ref[...]).astype(BF16)


def _mlp(h, w_up, w_down, x, post_w, next_w, tm=512, tf=512):
    M, D = h.shape
    FF = w_up.shape[1]
    tm = min(tm, M)
    emit_next = next_w is not None
    nw = next_w if emit_next else post_w
    row = lambda: pl.BlockSpec((1, D), lambda i, f: (0, 0))
    tile = lambda: pl.BlockSpec((tm, D), lambda i, f: (i, 0))
    out_specs = [tile()] + ([tile()] if emit_next else [])
    out_shape = [jax.ShapeDtypeStruct((M, D), F32)] + ([jax.ShapeDtypeStruct((M, D), BF16)] if emit_next else [])
    vm = 2 * (2 * _nbytes((tm, D), BF16) + 2 * _nbytes((D, tf), BF16) + 2 * _nbytes((tm, D), F32))
    vm += 2 * _nbytes((tm, D), F32) + 2 * _nbytes((tm, tf), F32)
    outs = pl.pallas_call(
        functools.partial(_mlp_kernel, emit_next=emit_next),
        grid=(M // tm, FF // tf),
        in_specs=[tile(),
                  pl.BlockSpec((D, tf), lambda i, f: (0, f)),
                  pl.BlockSpec((tf, D), lambda i, f: (f, 0)),
                  tile(), row(), row()],
        out_specs=out_specs,
        out_shape=out_shape,
        scratch_shapes=[pltpu.VMEM((tm, D), F32)],
        compiler_params=_cparams(("parallel", "arbitrary"), vm),
        name="relu2_mlp",
    )(h, w_up, w_down, x, post_w.reshape(1, D), nw.reshape(1, D))
    return (outs[0], outs[1]) if emit_next else (outs[0], None)


def _block_rows(b, c, first_zero):
    T, D = b.shape
    parts = []
    for i in range(T // c):
        if first_zero:
            parts.append(jnp.zeros((c, D), F32) if i == 0 else jnp.broadcast_to(b[i * c - 1:i * c, :], (c, D)))
        else:
            parts.append(jnp.broadcast_to(b[(i + 1) * c - 1:(i + 1) * c, :], (c, D)))
    return parts[0] if len(parts) == 1 else jnp.concatenate(parts, axis=0)


def _gla_prompt_kernel(q_ref, k_ref, lf_ref, v_ref, g_ref, nw_ref, o_ref, sout_ref, s_scr, qs_scr, ks_scr):
    T = q_ref.shape[1]
    D = q_ref.shape[2]
    H = D // HEAD_DIM
    c_idx = pl.program_id(1)

    @pl.when(c_idx == 0)
    def _():
        s_scr[...] = jnp.zeros_like(s_scr)

    lf = lf_ref[0]
    row = lax.broadcasted_iota(jnp.int32, (T, T), 0)
    col = lax.broadcasted_iota(jnp.int32, (T, T), 1)
    tri = (row >= col).astype(F32)
    b = jnp.dot(tri, lf, preferred_element_type=F32, precision=lax.Precision.HIGHEST)

    q = q_ref[0].astype(F32)
    k = k_ref[0].astype(F32)
    levels = []
    c = GLA_SUB
    while c < T:
        levels.append(c)
        c *= 2
    qs_scr[0] = (q * jnp.exp(b)).astype(BF16)
    for n, c in enumerate(levels):
        st = _block_rows(b, c, True)
        qs_scr[n + 1] = (q * jnp.exp(b - st)).astype(BF16)
        if n == 0:
            ks_scr[0] = (k * jnp.exp(st - b)).astype(BF16)
        ks_scr[n + 1] = (k * jnp.exp(_block_rows(b, c, False) - b)).astype(BF16)
    ks_scr[len(levels) + 1] = (k * jnp.exp(b[T - 1:T, :] - b)).astype(BF16)

    masks = [(row // GLA_SUB == col // GLA_SUB) & (col <= row)]
    for c in levels:
        masks.append(((row // c) % 2 == 1) & (col // c == row // c - 1))
    ones = jnp.ones((T, HEAD_DIM), F32)
    nw = nw_ref[...]

    def head(h, carry):
        hs = pl.ds(pl.multiple_of(h * HEAD_DIM, HEAD_DIM), HEAD_DIM)
        a = jnp.zeros((T, T), F32)
        for n in range(len(levels), -1, -1):
            qn = qs_scr[max(n, 1), :, hs]
            a = jnp.where(masks[n], _dot_nt(qn, ks_scr[n, :, hs]), a)
        vh = v_ref[0, :, hs]
        s_old = s_scr[h]
        o = _dot(a.astype(BF16), vh) + _dot(qs_scr[0, :, hs], s_old.astype(BF16))
        dec = _dot_tn(lf_ref[0, :, hs], ones, precision=lax.Precision.HIGHEST)
        s_scr[h] = s_old * jnp.exp(dec) + _dot_tn(ks_scr[len(levels) + 1, :, hs], vh)
        on = _rms(o, nw)
        o_ref[0, :, hs] = (on * g_ref[0, :, hs].astype(F32)).astype(o_ref.dtype)
        return carry

    lax.fori_loop(0, H, head, 0)

    @pl.when(c_idx == pl.num_programs(1) - 1)
    def _():
        sout_ref[0] = s_scr[...]


def _gla_prompt(q, k, lf, v, g, norm_w, B, L):
    D = q.shape[-1]
    H = D // HEAD_DIM
    T = GLA_BLOCK
    nlev = 0
    c = GLA_SUB
    while c < T:
        nlev += 1
        c *= 2
    r3 = lambda a: a.reshape(B, L, D)
    blk = lambda: pl.BlockSpec((1, T, D), lambda b, c: (b, c, 0))
    vm = 2 * (5 * _nbytes((T, D), BF16) + _nbytes((T, D), F32)) + 3 * _nbytes((H, HEAD_DIM, HEAD_DIM), F32)
    vm += (2 * nlev + 3) * _nbytes((T, D), BF16) + 12 * _nbytes((T, D), F32)
    o, s = pl.pallas_call(
        _gla_prompt_kernel,
        grid=(B, L // T),
        in_specs=[blk(), blk(), blk(), blk(), blk(), pl.BlockSpec((1, HEAD_DIM), lambda b, c: (0, 0))],
        out_specs=[blk(), pl.BlockSpec((1, H, HEAD_DIM, HEAD_DIM), lambda b, c: (b, 0, 0, 0))],
        out_shape=[jax.ShapeDtypeStruct((B, L, D), BF16),
                   jax.ShapeDtypeStruct((B, H, HEAD_DIM, HEAD_DIM), F32)],
        scratch_shapes=[pltpu.VMEM((H, HEAD_DIM, HEAD_DIM), F32),
                        pltpu.VMEM((nlev + 1, T, D), BF16),
                        pltpu.VMEM((nlev + 2, T, D), BF16)],
        compiler_params=_cparams(("parallel", "arbitrary"), vm),
        name="hgrn2_recurrence_prompt",
    )(r3(q), r3(k), r3(lf), r3(v), r3(g), norm_w.reshape(1, HEAD_DIM))
    return o.reshape(B * L, D), s


def _gla_step_kernel(q_ref, k_ref, lf_ref, v_ref, g_ref, s0_ref, nw_ref, o_ref, sout_ref, qs_scr, ks_scr, ke_scr):
    bb, T, D = q_ref.shape
    H = D // HEAD_DIM
    lf = lf_ref[...]
    pos = lax.broadcasted_iota(jnp.int32, (bb, T, D), 1)
    b = lf
    sh = 1
    while sh < T:
        b = b + jnp.where(pos >= sh, pltpu.roll(b, sh, 1), 0.0)
        sh *= 2
    b_last = b[:, T - 1:T, :]
    k = k_ref[...]
    qs_scr[...] = q_ref[...] * jnp.exp(b)
    ks_scr[...] = k * jnp.exp(-b)
    ke_scr[...] = k * jnp.exp(b_last - b)
    row = lax.broadcasted_iota(jnp.int32, (T, T), 0)
    col = lax.broadcasted_iota(jnp.int32, (T, T), 1)
    causal = col <= row
    ones = jnp.ones((T, HEAD_DIM), F32)
    nw = nw_ref[...]

    def body(n, carry):
        i = n // H
        h = n % H
        hs = pl.ds(pl.multiple_of(h * HEAD_DIM, HEAD_DIM), HEAD_DIM)
        qd = qs_scr[i, :, hs].astype(BF16)
        a = jnp.where(causal, _dot_nt(qd, ks_scr[i, :, hs].astype(BF16)), 0.0)
        vh = v_ref[i, :, hs].astype(BF16)
        s_old = s0_ref[i, h]
        o = _dot(a.astype(BF16), vh) + _dot(qd, s_old.astype(BF16))
        dec = _dot_tn(lf_ref[i, :, hs], ones, precision=lax.Precision.HIGHEST)
        sout_ref[i, h] = s_old * jnp.exp(dec) + _dot_tn(ke_scr[i, :, hs].astype(BF16), vh)
        on = _rms(o, nw)
        o_ref[i, :, hs] = (on * g_ref[i, :, hs]).astype(o_ref.dtype)
        return carry

    lax.fori_loop(0, bb * H, body, 0)


def _gla_step(q, k, lf, v, g, s0, norm_w, B, L, bb=4):
    D = q.shape[-1]
    H = D // HEAD_DIM
    r3 = lambda a: a.reshape(B, L, D)
    blk = lambda: pl.BlockSpec((bb, L, D), lambda b: (b, 0, 0))
    sblk = lambda: pl.BlockSpec((bb, H, HEAD_DIM, HEAD_DIM), lambda b: (b, 0, 0, 0))
    vm = 2 * (6 * _nbytes((bb, L, D), F32) + 2 * _nbytes((bb, H, HEAD_DIM, HEAD_DIM), F32))
    vm += 10 * _nbytes((bb, L, D), F32)
    o, s = pl.pallas_call(
        _gla_step_kernel,
        grid=(B // bb,),
        in_specs=[blk(), blk(), blk(), blk(), blk(), sblk(), pl.BlockSpec((1, HEAD_DIM), lambda b: (0, 0))],
        out_specs=[blk(), sblk()],
        out_shape=[jax.ShapeDtypeStruct((B, L, D), F32),
                   jax.ShapeDtypeStruct((B, H, HEAD_DIM, HEAD_DIM), F32)],
        scratch_shapes=[pltpu.VMEM((bb, L, D), F32)] * 3,
        compiler_params=_cparams(("parallel",), vm),
        name="hgrn2_recurrence_step",
    )(r3(q), r3(k), r3(lf), r3(v), r3(g), s0, norm_w.reshape(1, HEAD_DIM))
    return o.reshape(B * L, D), s


def _pool_kernel(*refs, pos0, has_halo):
    if has_halo:
        u_ref, halo_ref, buf_ref, pw_ref, ps_ref, o_ref, nb_ref, ext = refs
    else:
        u_ref, buf_ref, pw_ref, ps_ref, o_ref, nb_ref, ext = refs
    bb, tl, D = u_ref.shape
    G = len(POOL_WINDOWS)
    gc = D // G
    l_idx = pl.program_id(1)
    ext[:, HALO:, :] = u_ref[...]
    if has_halo:
        @pl.when(l_idx == 0)
        def _():
            ext[:, :HALO, :] = buf_ref[...]

        @pl.when(l_idx > 0)
        def _():
            ext[:, :HALO, :] = halo_ref[...]
    else:
        ext[:, :HALO, :] = buf_ref[...]

    pos = pos0 + l_idx * tl + lax.broadcasted_iota(jnp.int32, (tl, gc), 0)
    for gi, w in enumerate(POOL_WINDOWS):
        ls = slice(gi * gc, (gi + 1) * gc)
        cnt = jnp.minimum(w, pos + 1).astype(F32)
        rows = []
        for i in range(bb):
            u = ext[i, HALO:HALO + tl, ls]
            s = u
            for d in range(1, w):
                s = s + ext[i, HALO - d:HALO - d + tl, ls]
            rows.append(s / cnt - u)
        pooled = rows[0] if bb == 1 else jnp.concatenate(rows, axis=0)
        mixed = (_dot(pooled.astype(BF16), pw_ref[gi]) * ps_ref[:, ls]).astype(o_ref.dtype)
        for i in range(bb):
            o_ref[i, :, ls] = mixed[i * tl:(i + 1) * tl]

    @pl.when(l_idx == pl.num_programs(1) - 1)
    def _():
        nb_ref[...] = ext[:, tl:tl + HALO, :]


def _pool(u, buf, pos0, pool_w, pool_scale, B, L, bb, tl, out_dtype):
    D = u.shape[-1]
    G = len(POOL_WINDOWS)
    gc = D // G
    tl = min(tl, L)
    u3 = u.reshape(B, L, D)
    has_halo = L > tl
    step = tl // HALO
    in_specs = [pl.BlockSpec((bb, tl, D), lambda b, l: (b, l, 0))]
    args = [u3]
    if has_halo:
        in_specs.append(pl.BlockSpec((bb, HALO, D), lambda b, l: (b, jnp.maximum(l * step - 1, 0), 0)))
        args.append(u3)
    in_specs += [pl.BlockSpec((bb, HALO, D), lambda b, l: (b, 0, 0)),
                 pl.BlockSpec((G, gc, gc), lambda b, l: (0, 0, 0)),
                 pl.BlockSpec((1, D), lambda b, l: (0, 0))]
    args += [buf, pool_w, pool_scale.reshape(1, D)]
    vm = 2 * (_nbytes((bb, tl, D), F32) + 3 * _nbytes((bb, HALO, D), F32) + _nbytes((G, gc, gc), BF16)
              + _nbytes((bb, tl, D), BF16)) + _nbytes((bb, tl + HALO, D), F32) + 4 * _nbytes((bb, tl, D), F32)
    mixed, newbuf = pl.pallas_call(
        functools.partial(_pool_kernel, pos0=pos0, has_halo=has_halo),
        grid=(B // bb, L // tl),
        in_specs=in_specs,
        out_specs=[pl.BlockSpec((bb, tl, D), lambda b, l: (b, l, 0)),
                   pl.BlockSpec((bb, HALO, D), lambda b, l: (b, 0, 0))],
        out_shape=[jax.ShapeDtypeStruct((B, L, D), out_dtype), jax.ShapeDtypeStruct((B, HALO, D), F32)],
        scratch_shapes=[pltpu.VMEM((bb, tl + HALO, D), F32)],
        compiler_params=_cparams(("parallel", "arbitrary"), vm),
        name="causal_pool",
    )(*args)
    return mixed.reshape(B * L, D), newbuf


def _attn_kernel(q_ref, k_ref, v_ref, o_ref, *, heads):
    bb, tq, D = q_ref.shape
    hd = D // heads
    scale = hd ** -0.5
    for i in range(bb):
        for h in range(heads):
            hs = slice(h * hd, (h + 1) * hd)
            s = _dot_nt(q_ref[i, :, hs].astype(BF16), k_ref[i, :, hs].astype(BF16)) * scale
            p = jnp.exp(s - jnp.max(s, axis=-1, keepdims=True))
            den = jnp.sum(p, axis=-1, keepdims=True)
            o = _dot(p.astype(BF16), v_ref[i, :, hs].astype(BF16))
            o_ref[i, :, hs] = (o / den).astype(o_ref.dtype)


def _attn(q, mk, mv, B, L, heads, bb, tq, out_dtype):
    D = q.shape[-1]
    nm = mk.shape[1]
    tq = min(tq, L)
    vm = 2 * (_nbytes((bb, tq, D), q.dtype) + 2 * _nbytes((bb, nm, D), mk.dtype) + _nbytes((bb, tq, D), BF16))
    vm += 2 * _nbytes((bb, nm, D), BF16) + 6 * _nbytes((tq, max(nm, D // heads)), F32)
    o = pl.pallas_call(
        functools.partial(_attn_kernel, heads=heads),
        grid=(B // bb, L // tq),
        in_specs=[pl.BlockSpec((bb, tq, D), lambda b, l: (b, l, 0)),
                  pl.BlockSpec((bb, nm, D), lambda b, l: (b, 0, 0)),
                  pl.BlockSpec((bb, nm, D), lambda b, l: (b, 0, 0))],
        out_specs=pl.BlockSpec((bb, tq, D), lambda b, l: (b, l, 0)),
        out_shape=jax.ShapeDtypeStruct((B, L, D), out_dtype),
        compiler_params=_cparams(("parallel", "parallel"), vm),
        name="memory_cross_attention",
    )(q.reshape(B, L, D), mk, mv)
    return o.reshape(B * L, D)


def _trunk(x, B, L, pos0, S_in, buf_in, mk, mv, p, prompt):
    depth = p["w_xq"].shape[0]
    D = x.shape[-1]
    heads = p["mem_heads"]
    hn = _rmsnorm(x, p["norm_mix_pre"][0])
    S_out, buf_out = [], []
    act = BF16 if prompt else F32
    for l in range(depth):
        j = l // 2
        if l % 2 == 0:
            q, k, lf, v, g = _proj_a(hn, p["w_in_a"][j], p["hg_lb_logits"], l, act)
            if prompt:
                o, S = _gla_prompt(q, k, lf, v, g, p["hg_norm"][j], B, L)
            else:
                o, S = _gla_step(q, k, lf, v, g, S_in[j], p["hg_norm"][j], B, L)
            S_out.append(S)
            x, hn = _mm_res(o, p["w_out_a"][j], x, p["norm_mix_post"][l], p["norm_x_pre"][l])
        else:
            (u,) = _matmul(hn, p["w_in_b"][j], [F32])
            bb, tl = (1, 512) if prompt else (16, L)
            mixed, nb = _pool(u, buf_in[j], pos0, p["pool_w"][j], p["pool_scale"][j], B, L, bb, tl, act)
            buf_out.append(nb[:, 1:, :])
            x, hn = _mm_res(mixed, p["w_out_b"][j], x, p["norm_mix_post"][l], p["norm_x_pre"][l])
        (qx,) = _matmul(hn, p["w_xq"][l], [act])
        bb, tq = (1, 512) if prompt else (2, L)
        a = _attn(qx, mk[l], mv[l], B, L, heads, bb, tq, act)
        x, hn = _mm_res(a, p["w_xo"][l], x, p["norm_x_post"][l], p["norm_mlp_pre"][l])
        next_w = p["norm_mix_pre"][l + 1] if l + 1 < depth else None
        x, hn = _mlp(hn, p["w_up"][l], p["w_down"][l], x, p["norm_mlp_post"][l], next_w)
    return x, jnp.stack(S_out), jnp.stack(buf_out)


def kernel(x_prompt, x_sample, state_hgrn, state_pool, cache_mem_k, cache_mem_v, mem_prompt, w_in_a, hg_lb_logits, hg_norm, w_out_a, w_in_b, pool_w, pool_scale, w_out_b, norm_mem, w_xq, w_xkv, w_xo, norm_mix_pre, norm_mix_post, norm_x_pre, norm_x_post, norm_mlp_pre, norm_mlp_post, w_up, w_down):
    B, L, D = x_prompt.shape
    Bs, Ls, _ = x_sample.shape
    depth = w_xq.shape[0]
    n_mem, heads = cache_mem_k.shape[2], cache_mem_k.shape[3]
    assert L % GLA_BLOCK == 0 and Ls <= GLA_SUB and D % (HEAD_DIM * len(POOL_WINDOWS)) == 0
    assert state_pool.shape[2] == HALO - 1
    bf = lambda a: a.astype(BF16)
    p = dict(w_in_a=bf(w_in_a), hg_lb_logits=hg_lb_logits, hg_norm=hg_norm, w_out_a=bf(w_out_a),
             w_in_b=bf(w_in_b), pool_w=bf(pool_w), pool_scale=pool_scale, w_out_b=bf(w_out_b),
             w_xq=bf(w_xq), w_xo=bf(w_xo), norm_mix_pre=norm_mix_pre, norm_mix_post=norm_mix_post,
             norm_x_pre=norm_x_pre, norm_x_post=norm_x_post, norm_mlp_pre=norm_mlp_pre,
             norm_mlp_post=norm_mlp_post, w_up=bf(w_up), w_down=bf(w_down), mem_heads=heads)
    w_xkv_bf = bf(w_xkv)

    mem2d = mem_prompt.reshape(B * n_mem, D)
    mk_f32, mv_f32, mk_bf, mv_bf = [], [], [], []
    for l in range(depth):
        kv32, kv16 = _matmul(_rmsnorm(mem2d, norm_mem[l]), w_xkv_bf[l], [F32, BF16])
        mk_f32.append(kv32[:, :D].reshape(B, n_mem, heads, D // heads))
        mv_f32.append(kv32[:, D:].reshape(B, n_mem, heads, D // heads))
        mk_bf.append(kv16[:, :D].reshape(B, n_mem, D))
        mv_bf.append(kv16[:, D:].reshape(B, n_mem, D))
    cache_mem_k_prompt = jnp.stack(mk_f32)
    cache_mem_v_prompt = jnp.stack(mv_f32)

    n_b = state_pool.shape[0]
    buf0 = jnp.zeros((n_b, B, HALO, D), F32)
    y_prompt, state_hgrn_prompt, state_pool_prompt = _trunk(
        x_prompt.reshape(B * L, D), B, L, 0, None, buf0, mk_bf, mv_bf, p, True)

    buf_s = jnp.pad(state_pool, ((0, 0), (0, 0), (1, 0), (0, 0)))
    ck = cache_mem_k.reshape(depth, Bs, n_mem, D)
    cv = cache_mem_v.reshape(depth, Bs, n_mem, D)
    y_sample, state_hgrn_sample, state_pool_sample = _trunk(
        x_sample.reshape(Bs * Ls, D), Bs, Ls, PAST_LEN, state_hgrn, buf_s, ck, cv, p, False)

    return (y_prompt.reshape(B, L, D), y_sample.reshape(Bs, Ls, D), state_hgrn_prompt, state_pool_prompt,
            cache_mem_k_prompt, cache_mem_v_prompt, state_hgrn_sample, state_pool_sample)
```

```python
import functools

import jax
import jax.numpy as jnp
from jax import lax
from jax.experimental import pallas as pl
from jax.experimental.pallas import tpu as pltpu

F32 = jnp.float32
BF16 = jnp.bfloat16
EPS = 1e-6
PAST_LEN = 16384
POOL_WINDOWS = (2, 4, 8, 16)
LANES = 128
HEAD_DIM = 128
GLA_BLOCK = 128
GLA_SUB = 16
HALO = 16

V7X_VMEM_BYTES = 64 * 1024 * 1024
VMEM_CAP = V7X_VMEM_BYTES - 6 * 1024 * 1024


def _cparams(sem, vmem_bytes):
    return pltpu.CompilerParams(dimension_semantics=sem,
                                vmem_limit_bytes=int(min(max(vmem_bytes, 16 * 1024 * 1024), VMEM_CAP)))


def _nbytes(shape, dtype):
    n = 1
    for s in shape:
        n *= s
    return n * jnp.dtype(dtype).itemsize


def _rms(v, w):
    ms = jnp.mean(v * v, axis=-1, keepdims=True)
    return v * lax.rsqrt(ms + EPS) * w


def _sigmoid(x):
    return 1.0 / (1.0 + jnp.exp(-x))


def _dot(a, b):
    return jnp.dot(a, b, preferred_element_type=F32)


def _dot_nt(a, b):
    return lax.dot_general(a, b, (((1,), (1,)), ((), ())), preferred_element_type=F32)


def _dot_tn(a, b, precision=None):
    return lax.dot_general(a, b, (((0,), (0,)), ((), ())), preferred_element_type=F32,
                           precision=precision)


def _rmsnorm_kernel(x_ref, w_ref, o_ref):
    o_ref[...] = _rms(x_ref[...], w_ref[...]).astype(o_ref.dtype)


def _rmsnorm(x, w, tm=512):
    M, D = x.shape
    tm = min(tm, M)
    return pl.pallas_call(
        _rmsnorm_kernel,
        grid=(M // tm,),
        in_specs=[pl.BlockSpec((tm, D), lambda i: (i, 0)),
                  pl.BlockSpec((1, D), lambda i: (0, 0))],
        out_specs=pl.BlockSpec((tm, D), lambda i: (i, 0)),
        out_shape=jax.ShapeDtypeStruct((M, D), BF16),
        compiler_params=_cparams(("parallel",), 4 * _nbytes((tm, D), F32)),
        name="rmsnorm",
    )(x, w.reshape(1, D))


def _mm_kernel(a_ref, w_ref, *o_refs):
    acc = _dot(a_ref[...], w_ref[...])
    for o_ref in o_refs:
        o_ref[...] = acc.astype(o_ref.dtype)


def _matmul(a, w, out_dtypes, tm=1024, tn=1024):
    M, K = a.shape
    N = w.shape[1]
    tm, tn = min(tm, M), min(tn, N)
    vm = 2 * (_nbytes((tm, K), BF16) + _nbytes((K, tn), BF16)) + _nbytes((tm, tn), F32)
    vm += sum(2 * _nbytes((tm, tn), d) for d in out_dtypes)
    outs = pl.pallas_call(
        _mm_kernel,
        grid=(M // tm, N // tn),
        in_specs=[pl.BlockSpec((tm, K), lambda i, j: (i, 0)),
                  pl.BlockSpec((K, tn), lambda i, j: (0, j))],
        out_specs=[pl.BlockSpec((tm, tn), lambda i, j: (i, j)) for _ in out_dtypes],
        out_shape=[jax.ShapeDtypeStruct((M, N), d) for d in out_dtypes],
        compiler_params=_cparams(("parallel", "parallel"), vm),
        name="matmul",
    )(a, w)
    return outs


def _proj_a_kernel(h_ref, wq_ref, wf_ref, wi_ref, wg_ref, lbl_ref, q_ref, k_ref, lf_ref, v_ref, g_ref,
                   *, layer):
    h = h_ref[...]
    aq = _dot(h, wq_ref[...])
    q_ref[...] = (aq * _sigmoid(aq)).astype(q_ref.dtype)
    lg = lbl_ref[...]
    e = jnp.exp(lg - jnp.max(lg, axis=0, keepdims=True))
    lb = jnp.sum(e[:layer + 1], axis=0, keepdims=True) / jnp.sum(e, axis=0, keepdims=True)
    f = lb + (1.0 - lb) * _sigmoid(_dot(h, wf_ref[...]))
    k_ref[...] = (1.0 - f).astype(k_ref.dtype)
    lf_ref[...] = jnp.log(f)
    v_ref[...] = _dot(h, wi_ref[...]).astype(v_ref.dtype)
    ag = _dot(h, wg_ref[...])
    g_ref[...] = (ag * _sigmoid(ag)).astype(g_ref.dtype)


def _proj_a(h, w, lb_logits, layer, act_dtype, tm=1024, tn=512):
    M, D = h.shape
    tm = min(tm, M)
    nj = D // tn
    nl = lb_logits.shape[0]
    wspec = lambda s: pl.BlockSpec((D, tn), lambda i, j: (0, j + s * nj))
    ospec = pl.BlockSpec((tm, tn), lambda i, j: (i, j))
    vm = 2 * (_nbytes((tm, D), BF16) + 4 * _nbytes((D, tn), BF16) + 5 * _nbytes((tm, tn), F32))
    vm += 4 * _nbytes((tm, tn), F32)
    return pl.pallas_call(
        functools.partial(_proj_a_kernel, layer=layer),
        grid=(M // tm, nj),
        in_specs=[pl.BlockSpec((tm, D), lambda i, j: (i, 0)), wspec(0), wspec(1), wspec(2), wspec(3),
                  pl.BlockSpec((nl, tn), lambda i, j: (0, j))],
        out_specs=[ospec] * 5,
        out_shape=[jax.ShapeDtypeStruct((M, D), act_dtype), jax.ShapeDtypeStruct((M, D), act_dtype),
                   jax.ShapeDtypeStruct((M, D), F32), jax.ShapeDtypeStruct((M, D), act_dtype),
                   jax.ShapeDtypeStruct((M, D), act_dtype)],
        compiler_params=_cparams(("parallel", "parallel"), vm),
        name="hgrn2_proj",
    )(h, w, w, w, w, lb_logits)


def _mm_res_kernel(a_ref, w_ref, x_ref, pw_ref, nw_ref, ox_ref, oh_ref):
    m = _dot(a_ref[...].astype(BF16), w_ref[...])
    y = x_ref[...] + _rms(m, pw_ref[...])
    ox_ref[...] = y
    oh_ref[...] = _rms(y, nw_ref[...]).astype(oh_ref.dtype)


def _mm_res(a, w, x, post_w, next_w, tm=512):
    M, K = a.shape
    D = w.shape[1]
    tm = min(tm, M)
    row = lambda: pl.BlockSpec((1, D), lambda i: (0, 0))
    vm = 2 * (_nbytes((tm, K), a.dtype) + _nbytes((K, D), BF16) + 2 * _nbytes((tm, D), F32)
              + _nbytes((tm, D), BF16)) + 2 * _nbytes((tm, D), F32)
    return pl.pallas_call(
        _mm_res_kernel,
        grid=(M // tm,),
        in_specs=[pl.BlockSpec((tm, K), lambda i: (i, 0)),
                  pl.BlockSpec((K, D), lambda i: (0, 0)),
                  pl.BlockSpec((tm, D), lambda i: (i, 0)), row(), row()],
        out_specs=[pl.BlockSpec((tm, D), lambda i: (i, 0)), pl.BlockSpec((tm, D), lambda i: (i, 0))],
        out_shape=[jax.ShapeDtypeStruct((M, D), F32), jax.ShapeDtypeStruct((M, D), BF16)],
        compiler_params=_cparams(("parallel",), vm),
        name="matmul_norm_residual",
    )(a, w, x, post_w.reshape(1, D), next_w.reshape(1, D))


def _mlp_kernel(h_ref, wu_ref, wd_ref, x_ref, pw_ref, nw_ref, ox_ref, *rest, emit_next):
    acc_ref = rest[-1]
    f = pl.program_id(1)

    @pl.when(f == 0)
    def _():
        acc_ref[...] = jnp.zeros_like(acc_ref)

    u = jnp.maximum(_dot(h_ref[...], wu_ref[...]), 0.0)
    acc_ref[...] += _dot((u * u).astype(BF16), wd_ref[...])

    @pl.when(f == pl.num_programs(1) - 1)
    def _():
        y = x_ref[...] + _rms(acc_ref[...], pw_ref[...])
        ox_ref[...] = y
        if emit_next:
            rest[0][...] = _rms(y, nw_ref[...]).astype(BF16)


def _mlp(h, w_up, w_down, x, post_w, next_w, tm=512, tf=512):
    M, D = h.shape
    FF = w_up.shape[1]
    tm = min(tm, M)
    emit_next = next_w is not None
    nw = next_w if emit_next else post_w
    row = lambda: pl.BlockSpec((1, D), lambda i, f: (0, 0))
    tile = lambda: pl.BlockSpec((tm, D), lambda i, f: (i, 0))
    out_specs = [tile()] + ([tile()] if emit_next else [])
    out_shape = [jax.ShapeDtypeStruct((M, D), F32)] + ([jax.ShapeDtypeStruct((M, D), BF16)] if emit_next else [])
    vm = 2 * (2 * _nbytes((tm, D), BF16) + 2 * _nbytes((D, tf), BF16) + 2 * _nbytes((tm, D), F32))
    vm += 2 * _nbytes((tm, D), F32) + 2 * _nbytes((tm, tf), F32)
    outs = pl.pallas_call(
        functools.partial(_mlp_kernel, emit_next=emit_next),
        grid=(M // tm, FF // tf),
        in_specs=[tile(),
                  pl.BlockSpec((D, tf), lambda i, f: (0, f)),
                  pl.BlockSpec((tf, D), lambda i, f: (f, 0)),
                  tile(), row(), row()],
        out_specs=out_specs,
        out_shape=out_shape,
        scratch_shapes=[pltpu.VMEM((tm, D), F32)],
        compiler_params=_cparams(("parallel", "arbitrary"), vm),
        name="relu2_mlp",
    )(h, w_up, w_down, x, post_w.reshape(1, D), nw.reshape(1, D))
    return (outs[0], outs[1]) if emit_next else (outs[0], None)


def _block_rows(b, c, first_zero):
    T, D = b.shape
    parts = []
    for i in range(T // c):
        if first_zero:
            parts.append(jnp.zeros((c, D), F32) if i == 0 else jnp.broadcast_to(b[i * c - 1:i * c, :], (c, D)))
        else:
            parts.append(jnp.broadcast_to(b[(i + 1) * c - 1:(i + 1) * c, :], (c, D)))
    return parts[0] if len(parts) == 1 else jnp.concatenate(parts, axis=0)


def _gla_prompt_kernel(q_ref, k_ref, lf_ref, v_ref, g_ref, nw_ref, o_ref, sout_ref, s_scr, qs_scr, ks_scr):
    T = q_ref.shape[1]
    D = q_ref.shape[2]
    H = D // HEAD_DIM
    c_idx = pl.program_id(1)

    @pl.when(c_idx == 0)
    def _():
        s_scr[...] = jnp.zeros_like(s_scr)

    lf = lf_ref[0]
    row = lax.broadcasted_iota(jnp.int32, (T, T), 0)
    col = lax.broadcasted_iota(jnp.int32, (T, T), 1)
    tri = (row >= col).astype(F32)
    b = jnp.dot(tri, lf, preferred_element_type=F32, precision=lax.Precision.HIGHEST)

    q = q_ref[0].astype(F32)
    k = k_ref[0].astype(F32)
    levels = []
    c = GLA_SUB
    while c < T:
        levels.append(c)
        c *= 2
    qs_scr[0] = (q * jnp.exp(b)).astype(BF16)
    for n, c in enumerate(levels):
        st = _block_rows(b, c, True)
        qs_scr[n + 1] = (q * jnp.exp(b - st)).astype(BF16)
        if n == 0:
            ks_scr[0] = (k * jnp.exp(st - b)).astype(BF16)
        ks_scr[n + 1] = (k * jnp.exp(_block_rows(b, c, False) - b)).astype(BF16)
    ks_scr[len(levels) + 1] = (k * jnp.exp(b[T - 1:T, :] - b)).astype(BF16)

    masks = [(row // GLA_SUB == col // GLA_SUB) & (col <= row)]
    for c in levels:
        masks.append(((row // c) % 2 == 1) & (col // c == row // c - 1))
    ones = jnp.ones((T, HEAD_DIM), F32)
    nw = nw_ref[...]

    def head(h, carry):
        hs = pl.ds(pl.multiple_of(h * HEAD_DIM, HEAD_DIM), HEAD_DIM)
        a = jnp.zeros((T, T), F32)
        for n in range(len(levels), -1, -1):
            qn = qs_scr[max(n, 1), :, hs]
            a = jnp.where(masks[n], _dot_nt(qn, ks_scr[n, :, hs]), a)
        vh = v_ref[0, :, hs]
        s_old = s_scr[h]
        o = _dot(a.astype(BF16), vh) + _dot(qs_scr[0, :, hs], s_old.astype(BF16))
        dec = _dot_tn(lf_ref[0, :, hs], ones, precision=lax.Precision.HIGHEST)
        s_scr[h] = s_old * jnp.exp(dec) + _dot_tn(ks_scr[len(levels) + 1, :, hs], vh)
        on = _rms(o, nw)
        o_ref[0, :, hs] = (on * g_ref[0, :, hs].astype(F32)).astype(o_ref.dtype)
        return carry

    lax.fori_loop(0, H, head, 0)

    @pl.when(c_idx == pl.num_programs(1) - 1)
    def _():
        sout_ref[0] = s_scr[...]


def _gla_prompt(q, k, lf, v, g, norm_w, B, L):
    D = q.shape[-1]
    H = D // HEAD_DIM
    T = GLA_BLOCK
    nlev = 0
    c = GLA_SUB
    while c < T:
        nlev += 1
        c *= 2
    r3 = lambda a: a.reshape(B, L, D)
    blk = lambda: pl.BlockSpec((1, T, D), lambda b, c: (b, c, 0))
    vm = 2 * (5 * _nbytes((T, D), BF16) + _nbytes((T, D), F32)) + 3 * _nbytes((H, HEAD_DIM, HEAD_DIM), F32)
    vm += (2 * nlev + 3) * _nbytes((T, D), BF16) + 12 * _nbytes((T, D), F32)
    o, s = pl.pallas_call(
        _gla_prompt_kernel,
        grid=(B, L // T),
        in_specs=[blk(), blk(), blk(), blk(), blk(), pl.BlockSpec((1, HEAD_DIM), lambda b, c: (0, 0))],
        out_specs=[blk(), pl.BlockSpec((1, H, HEAD_DIM, HEAD_DIM), lambda b, c: (b, 0, 0, 0))],
        out_shape=[jax.ShapeDtypeStruct((B, L, D), BF16),
                   jax.ShapeDtypeStruct((B, H, HEAD_DIM, HEAD_DIM), F32)],
        scratch_shapes=[pltpu.VMEM((H, HEAD_DIM, HEAD_DIM), F32),
                        pltpu.VMEM((nlev + 1, T, D), BF16),
                        pltpu.VMEM((nlev + 2, T, D), BF16)],
        compiler_params=_cparams(("parallel", "arbitrary"), vm),
        name="hgrn2_recurrence_prompt",
    )(r3(q), r3(k), r3(lf), r3(v), r3(g), norm_w.reshape(1, HEAD_DIM))
    return o.reshape(B * L, D), s


def _gla_step_kernel(q_ref, k_ref, lf_ref, v_ref, g_ref, s0_ref, nw_ref, o_ref, sout_ref,
                     qs_scr, ks_scr, ke_scr, ds_scr):
    bb, T, D = q_ref.shape
    H = D // HEAD_DIM
    R = bb * T
    pos = lax.broadcasted_iota(jnp.int32, (bb, T, D), 1)
    b = lf_ref[...]
    sh = 1
    while sh < T:
        b = b + jnp.where(pos >= sh, pltpu.roll(b, sh, 1), 0.0)
        sh *= 2
    b_last = b[:, T - 1:T, :]
    k = k_ref[...]
    qs_scr[...] = q_ref[...] * jnp.exp(b)
    ks_scr[...] = k * jnp.exp(-b)
    ke_scr[...] = k * jnp.exp(b_last - b)
    d = jnp.exp(b_last)
    d_hi = d.astype(BF16).astype(F32)
    d_mid = (d - d_hi).astype(BF16).astype(F32)
    d_lo = (d - d_hi - d_mid).astype(BF16).astype(F32)
    ds_scr[...] = jnp.where(pos == 0, d_hi, jnp.where(pos == 1, d_mid, jnp.where(pos == 2, d_lo, 0.0)))

    row = lax.broadcasted_iota(jnp.int32, (R, R), 0)
    col = lax.broadcasted_iota(jnp.int32, (R, R), 1)
    same_seq_causal = (row // T == col // T) & (col <= row)
    rr = lax.broadcasted_iota(jnp.int32, (T, 2 * HEAD_DIM), 0)
    cc = lax.broadcasted_iota(jnp.int32, (T, 2 * HEAD_DIM), 1)
    pick = jnp.where((rr < 3) & (cc >= HEAD_DIM), 1.0, 0.0)
    zeros = jnp.zeros((T, HEAD_DIM), F32)
    nw = nw_ref[...]

    def head(h, carry):
        hs = pl.ds(pl.multiple_of(h * HEAD_DIM, HEAD_DIM), HEAD_DIM)
        qd = qs_scr[:, :, hs].reshape(R, HEAD_DIM)
        kd = ks_scr[:, :, hs].reshape(R, HEAD_DIM)
        vh = v_ref[:, :, hs].reshape(R, HEAD_DIM)
        a = jnp.where(same_seq_causal, _dot_nt(qd.astype(BF16), kd.astype(BF16)), 0.0)
        o_intra = _dot(a.astype(BF16), vh.astype(BF16))
        outs = []
        for i in range(bb):
            s_old = s0_ref[i, h]
            outs.append(o_intra[i * T:(i + 1) * T]
                        + _dot(qd[i * T:(i + 1) * T].astype(BF16), s_old.astype(BF16)))
            lhs = jnp.concatenate([ke_scr[i, :, hs], ds_scr[i, :, hs]], axis=0).astype(BF16)
            rhs = jnp.concatenate([jnp.concatenate([vh[i * T:(i + 1) * T], zeros], axis=1), pick],
                                  axis=0).astype(BF16)
            upd = _dot_tn(lhs, rhs)
            sout_ref[i, h] = s_old * upd[:, HEAD_DIM:] + upd[:, :HEAD_DIM]
        o = jnp.concatenate(outs, axis=0)
        on = _rms(o, nw) * g_ref[:, :, hs].reshape(R, HEAD_DIM)
        o_ref[:, :, hs] = on.reshape(bb, T, HEAD_DIM).astype(o_ref.dtype)
        return carry

    lax.fori_loop(0, H, head, 0)


def _gla_step(q, k, lf, v, g, s0, norm_w, B, L, bb=8):
    D = q.shape[-1]
    H = D // HEAD_DIM
    assert L == 8 and B % bb == 0
    r3 = lambda a: a.reshape(B, L, D)
    blk = lambda: pl.BlockSpec((bb, L, D), lambda b: (b, 0, 0))
    sblk = lambda: pl.BlockSpec((bb, H, HEAD_DIM, HEAD_DIM), lambda b: (b, 0, 0, 0))
    vm = 2 * (6 * _nbytes((bb, L, D), F32) + 2 * _nbytes((bb, H, HEAD_DIM, HEAD_DIM), F32))
    vm += 12 * _nbytes((bb, L, D), F32)
    o, s = pl.pallas_call(
        _gla_step_kernel,
        grid=(B // bb,),
        in_specs=[blk(), blk(), blk(), blk(), blk(), sblk(), pl.BlockSpec((1, HEAD_DIM), lambda b: (0, 0))],
        out_specs=[blk(), sblk()],
        out_shape=[jax.ShapeDtypeStruct((B, L, D), F32),
                   jax.ShapeDtypeStruct((B, H, HEAD_DIM, HEAD_DIM), F32)],
        scratch_shapes=[pltpu.VMEM((bb, L, D), F32)] * 4,
        compiler_params=_cparams(("parallel",), vm),
        name="hgrn2_recurrence_step",
    )(r3(q), r3(k), r3(lf), r3(v), r3(g), s0, norm_w.reshape(1, HEAD_DIM))
    return o.reshape(B * L, D), s


def _pool_kernel(*refs, pos0, has_halo):
    if has_halo:
        u_ref, halo_ref, buf_ref, pw_ref, ps_ref, o_ref, nb_ref, ext = refs
    else:
        u_ref, buf_ref, pw_ref, ps_ref, o_ref, nb_ref, ext = refs
    bb, tl, D = u_ref.shape
    G = len(POOL_WINDOWS)
    gc = D // G
    l_idx = pl.program_id(1)
    ext[:, HALO:, :] = u_ref[...]
    if has_halo:
        @pl.when(l_idx == 0)
        def _():
            ext[:, :HALO, :] = buf_ref[...]

        @pl.when(l_idx > 0)
        def _():
            ext[:, :HALO, :] = halo_ref[...]
    else:
        ext[:, :HALO, :] = buf_ref[...]

    pos = pos0 + l_idx * tl + lax.broadcasted_iota(jnp.int32, (tl, gc), 0)
    for gi, w in enumerate(POOL_WINDOWS):
        ls = slice(gi * gc, (gi + 1) * gc)
        cnt = jnp.minimum(w, pos + 1).astype(F32)
        rows = []
        for i in range(bb):
            u = ext[i, HALO:HALO + tl, ls]
            s = u
            for d in range(1, w):
                s = s + ext[i, HALO - d:HALO - d + tl, ls]
            rows.append(s / cnt - u)
        pooled = rows[0] if bb == 1 else jnp.concatenate(rows, axis=0)
        mixed = (_dot(pooled.astype(BF16), pw_ref[gi]) * ps_ref[:, ls]).astype(o_ref.dtype)
        for i in range(bb):
            o_ref[i, :, ls] = mixed[i * tl:(i + 1) * tl]

    @pl.when(l_idx == pl.num_programs(1) - 1)
    def _():
        nb_ref[...] = ext[:, tl:tl + HALO, :]


def _pool(u, buf, pos0, pool_w, pool_scale, B, L, bb, tl, out_dtype):
    D = u.shape[-1]
    G = len(POOL_WINDOWS)
    gc = D // G
    tl = min(tl, L)
    u3 = u.reshape(B, L, D)
    has_halo = L > tl
    step = tl // HALO
    in_specs = [pl.BlockSpec((bb, tl, D), lambda b, l: (b, l, 0))]
    args = [u3]
    if has_halo:
        in_specs.append(pl.BlockSpec((bb, HALO, D), lambda b, l: (b, jnp.maximum(l * step - 1, 0), 0)))
        args.append(u3)
    in_specs += [pl.BlockSpec((bb, HALO, D), lambda b, l: (b, 0, 0)),
                 pl.BlockSpec((G, gc, gc), lambda b, l: (0, 0, 0)),
                 pl.BlockSpec((1, D), lambda b, l: (0, 0))]
    args += [buf, pool_w, pool_scale.reshape(1, D)]
    vm = 2 * (_nbytes((bb, tl, D), F32) + 3 * _nbytes((bb, HALO, D), F32) + _nbytes((G, gc, gc), BF16)
              + _nbytes((bb, tl, D), BF16)) + _nbytes((bb, tl + HALO, D), F32) + 4 * _nbytes((bb, tl, D), F32)
    mixed, newbuf = pl.pallas_call(
        functools.partial(_pool_kernel, pos0=pos0, has_halo=has_halo),
        grid=(B // bb, L // tl),
        in_specs=in_specs,
        out_specs=[pl.BlockSpec((bb, tl, D), lambda b, l: (b, l, 0)),
                   pl.BlockSpec((bb, HALO, D), lambda b, l: (b, 0, 0))],
        out_shape=[jax.ShapeDtypeStruct((B, L, D), out_dtype), jax.ShapeDtypeStruct((B, HALO, D), F32)],
        scratch_shapes=[pltpu.VMEM((bb, tl + HALO, D), F32)],
        compiler_params=_cparams(("parallel", "arbitrary"), vm),
        name="causal_pool",
    )(*args)
    return mixed.reshape(B * L, D), newbuf


def _chunk_major(w, heads, axis):
    shp = w.shape
    n = shp[axis] // (heads * LANES)
    w = w.reshape(shp[:axis] + (heads, n, LANES) + shp[axis + 1:])
    return jnp.swapaxes(w, axis, axis + 1).reshape(shp)


def _attn_kernel(q_ref, k_ref, v_ref, o_ref, *, heads, flat):
    bb, tq, D = q_ref.shape
    R = D // LANES
    nchunk = R // heads
    n_mem = k_ref.shape[2] // R if flat else k_ref.shape[2]
    scale = (D // heads) ** -0.5
    for i in range(bb):
        for h in range(heads):
            js = [c * heads + h for c in range(nchunk)]

            def head_of(ref):
                if flat:
                    parts = [ref[0, i, pl.ds(j, n_mem, stride=R), :] for j in js]
                else:
                    parts = [ref[0, i, :, j * LANES:(j + 1) * LANES] for j in js]
                return jnp.concatenate(parts, axis=-1).astype(BF16)

            qh = jnp.concatenate([q_ref[i, :, j * LANES:(j + 1) * LANES] for j in js], axis=-1).astype(BF16)
            s = _dot_nt(qh, head_of(k_ref)) * scale
            p = jnp.exp(s - jnp.max(s, axis=-1, keepdims=True))
            den = jnp.sum(p, axis=-1, keepdims=True)
            o = _dot(p.astype(BF16), head_of(v_ref)) / den
            for c, j in enumerate(js):
                o_ref[i, :, j * LANES:(j + 1) * LANES] = o[:, c * LANES:(c + 1) * LANES].astype(o_ref.dtype)


def _attn(q, mk, mv, layer, B, L, heads, bb, tq, out_dtype, flat):
    D = q.shape[-1]
    tq = min(tq, L)
    kv_blk = (1, bb) + mk.shape[2:]
    vm = 2 * (_nbytes((bb, tq, D), q.dtype) + 2 * _nbytes(kv_blk, mk.dtype) + _nbytes((bb, tq, D), out_dtype))
    vm += 2 * _nbytes(kv_blk, BF16) + 8 * _nbytes((tq, D // heads), F32)
    o = pl.pallas_call(
        functools.partial(_attn_kernel, heads=heads, flat=flat),
        grid=(B // bb, L // tq),
        in_specs=[pl.BlockSpec((bb, tq, D), lambda b, l: (b, l, 0)),
                  pl.BlockSpec(kv_blk, lambda b, l: (layer, b, 0, 0)),
                  pl.BlockSpec(kv_blk, lambda b, l: (layer, b, 0, 0))],
        out_specs=pl.BlockSpec((bb, tq, D), lambda b, l: (b, l, 0)),
        out_shape=jax.ShapeDtypeStruct((B, L, D), out_dtype),
        compiler_params=_cparams(("parallel", "parallel"), vm),
        name="memory_cross_attention",
    )(q.reshape(B, L, D), mk, mv)
    return o.reshape(B * L, D)


def _mem_proj_kernel(x_ref, nw_ref, w_ref, flat_ref, bf_ref):
    tm, D = x_ref.shape
    R = D // LANES
    acc = _dot(_rms(x_ref[...], nw_ref[0]).astype(BF16), w_ref[0])
    bf_ref[0] = acc.astype(bf_ref.dtype)
    for j in range(R):
        flat_ref[pl.ds(j, tm, stride=R), :] = acc[:, j * LANES:(j + 1) * LANES]


def _mem_proj(mem, norm_w, w, tm=256):
    M, D = mem.shape
    depth = w.shape[0]
    R = D // LANES
    tm = min(tm, M)
    nt = M // tm
    vm = 2 * (_nbytes((tm, D), F32) + _nbytes((D, D), BF16) + _nbytes((tm, D), F32) + _nbytes((tm, D), BF16))
    vm += 2 * _nbytes((tm, D), F32)
    return pl.pallas_call(
        _mem_proj_kernel,
        grid=(depth, nt),
        in_specs=[pl.BlockSpec((tm, D), lambda l, i: (i, 0)),
                  pl.BlockSpec((1, 1, D), lambda l, i: (l, 0, 0)),
                  pl.BlockSpec((1, D, D), lambda l, i: (l, 0, 0))],
        out_specs=[pl.BlockSpec((tm * R, LANES), lambda l, i: (l * nt + i, 0)),
                   pl.BlockSpec((1, tm, D), lambda l, i: (l, i, 0))],
        out_shape=[jax.ShapeDtypeStruct((depth * M * R, LANES), F32),
                   jax.ShapeDtypeStruct((depth, M, D), BF16)],
        compiler_params=_cparams(("parallel", "parallel"), vm),
        name="memory_kv_projection",
    )(mem, norm_w.reshape(depth, 1, D), w)


def _trunk(x, B, L, pos0, S_in, buf_in, mk, mv, p, prompt):
    depth = p["w_xq"].shape[0]
    D = x.shape[-1]
    heads = p["mem_heads"]
    hn = _rmsnorm(x, p["norm_mix_pre"][0])
    S_out, buf_out = [], []
    act = BF16 if prompt else F32
    for l in range(depth):
        j = l // 2
        if l % 2 == 0:
            q, k, lf, v, g = _proj_a(hn, p["w_in_a"][j], p["hg_lb_logits"], l, act)
            if prompt:
                o, S = _gla_prompt(q, k, lf, v, g, p["hg_norm"][j], B, L)
            else:
                o, S = _gla_step(q, k, lf, v, g, S_in[j], p["hg_norm"][j], B, L)
            S_out.append(S)
            x, hn = _mm_res(o, p["w_out_a"][j], x, p["norm_mix_post"][l], p["norm_x_pre"][l])
        else:
            (u,) = _matmul(hn, p["w_in_b"][j], [F32])
            bb, tl = (1, 512) if prompt else (16, L)
            mixed, nb = _pool(u, buf_in[j], pos0, p["pool_w"][j], p["pool_scale"][j], B, L, bb, tl, act)
            buf_out.append(nb[:, 1:, :])
            x, hn = _mm_res(mixed, p["w_out_b"][j], x, p["norm_mix_post"][l], p["norm_x_pre"][l])
        (qx,) = _matmul(hn, p["w_xq"][l], [act])
        bb, tq = (1, 512) if prompt else (2, L)
        a = _attn(qx, mk, mv, l, B, L, heads, bb, tq, act, flat=not prompt)
        x, hn = _mm_res(a, p["w_xo"][l], x, p["norm_x_post"][l], p["norm_mlp_pre"][l])
        next_w = p["norm_mix_pre"][l + 1] if l + 1 < depth else None
        x, hn = _mlp(hn, p["w_up"][l], p["w_down"][l], x, p["norm_mlp_post"][l], next_w)
    return x, jnp.stack(S_out), jnp.stack(buf_out)


def kernel(x_prompt, x_sample, state_hgrn, state_pool, cache_mem_k, cache_mem_v, mem_prompt, w_in_a, hg_lb_logits, hg_norm, w_out_a, w_in_b, pool_w, pool_scale, w_out_b, norm_mem, w_xq, w_xkv, w_xo, norm_mix_pre, norm_mix_post, norm_x_pre, norm_x_post, norm_mlp_pre, norm_mlp_post, w_up, w_down):
    B, L, D = x_prompt.shape
    Bs, Ls, _ = x_sample.shape
    depth = w_xq.shape[0]
    n_mem, heads = cache_mem_k.shape[2], cache_mem_k.shape[3]
    assert L % GLA_BLOCK == 0 and Ls <= GLA_SUB and D % (HEAD_DIM * len(POOL_WINDOWS)) == 0
    assert state_pool.shape[2] == HALO - 1
    bf = lambda a: a.astype(BF16)
    p = dict(w_in_a=bf(w_in_a), hg_lb_logits=hg_lb_logits, hg_norm=hg_norm, w_out_a=bf(w_out_a),
             w_in_b=bf(w_in_b), pool_w=bf(pool_w), pool_scale=pool_scale, w_out_b=bf(w_out_b),
             w_xq=bf(_chunk_major(w_xq, heads, 2)), w_xo=bf(_chunk_major(w_xo, heads, 1)),
             norm_mix_pre=norm_mix_pre, norm_mix_post=norm_mix_post,
             norm_x_pre=norm_x_pre, norm_x_post=norm_x_post, norm_mlp_pre=norm_mlp_pre,
             norm_mlp_post=norm_mlp_post, w_up=bf(w_up), w_down=bf(w_down), mem_heads=heads)
    R = D // LANES
    nchunk = R // heads

    def cache_view(flat, nb):
        c = flat.reshape(depth, nb, n_mem, nchunk, heads, LANES)
        return jnp.swapaxes(c, 3, 4).reshape(depth, nb, n_mem, heads, D // heads)

    def flat_view(cache):
        nb = cache.shape[1]
        c = cache.reshape(depth, nb, n_mem, heads, nchunk, LANES)
        return jnp.swapaxes(c, 3, 4).reshape(depth, nb, n_mem * R, LANES)

    mem2d = mem_prompt.reshape(B * n_mem, D)
    k_flat, k_bf = _mem_proj(mem2d, norm_mem, bf(_chunk_major(w_xkv[:, :, :D], heads, 2)))
    v_flat, v_bf = _mem_proj(mem2d, norm_mem, bf(_chunk_major(w_xkv[:, :, D:], heads, 2)))
    cache_mem_k_prompt = cache_view(k_flat, B)
    cache_mem_v_prompt = cache_view(v_flat, B)
    mk_bf = k_bf.reshape(depth, B, n_mem, D)
    mv_bf = v_bf.reshape(depth, B, n_mem, D)

    n_b = state_pool.shape[0]
    buf0 = jnp.zeros((n_b, B, HALO, D), F32)
    y_prompt, state_hgrn_prompt, state_pool_prompt = _trunk(
        x_prompt.reshape(B * L, D), B, L, 0, None, buf0, mk_bf, mv_bf, p, True)

    buf_s = jnp.pad(state_pool, ((0, 0), (0, 0), (1, 0), (0, 0)))
    ck = flat_view(cache_mem_k)
    cv = flat_view(cache_mem_v)
    y_sample, state_hgrn_sample, state_pool_sample = _trunk(
        x_sample.reshape(Bs * Ls, D), Bs, Ls, PAST_LEN, state_hgrn, buf_s, ck, cv, p, False)

    return (y_prompt.reshape(B, L, D), y_sample.reshape(Bs, Ls, D), state_hgrn_prompt, state_pool_prompt,
            cache_mem_k_prompt, cache_mem_v_prompt, state_hgrn_sample, state_pool_sample)
```

```python
import functools

import numpy as np

import jax
import jax.numpy as jnp
from jax import lax
from jax.experimental import pallas as pl
from jax.experimental.pallas import tpu as pltpu

F32 = jnp.float32
BF16 = jnp.bfloat16
EPS = 1e-6
PAST_LEN = 16384
POOL_WINDOWS = (2, 4, 8, 16)
LANES = 128
HEAD_DIM = 128
GLA_BLOCK = 128
GLA_SUB = 16
GLA_HEAD_UNROLL = 4
HALO = 16

V7X_VMEM_BYTES = 64 * 1024 * 1024
VMEM_CAP = V7X_VMEM_BYTES - 6 * 1024 * 1024


def _cparams(sem, vmem_bytes):
    return pltpu.CompilerParams(dimension_semantics=sem,
                                vmem_limit_bytes=int(min(max(vmem_bytes, 16 * 1024 * 1024), VMEM_CAP)))


def _nbytes(shape, dtype):
    n = 1
    for s in shape:
        n *= s
    return n * jnp.dtype(dtype).itemsize


def _rms(v, w):
    ms = jnp.mean(v * v, axis=-1, keepdims=True)
    return v * lax.rsqrt(ms + EPS) * w


def _sigmoid(x):
    return 1.0 / (1.0 + jnp.exp(-x))


def _dot(a, b):
    return jnp.dot(a, b, preferred_element_type=F32)


def _dot_nt(a, b):
    return lax.dot_general(a, b, (((1,), (1,)), ((), ())), preferred_element_type=F32)


def _dot_tn(a, b, precision=None):
    return lax.dot_general(a, b, (((0,), (0,)), ((), ())), preferred_element_type=F32,
                           precision=precision)


def _rmsnorm_kernel(x_ref, w_ref, o_ref):
    o_ref[...] = _rms(x_ref[...], w_ref[...]).astype(o_ref.dtype)


def _row(layer):
    return lambda width: pl.BlockSpec((None, 1, width), lambda *_: (layer, 0, 0))


def _rmsnorm(x, w, layer, tm=512):
    M, D = x.shape
    tm = min(tm, M)
    return pl.pallas_call(
        _rmsnorm_kernel,
        grid=(M // tm,),
        in_specs=[pl.BlockSpec((tm, D), lambda i: (i, 0)), _row(layer)(D)],
        out_specs=pl.BlockSpec((tm, D), lambda i: (i, 0)),
        out_shape=jax.ShapeDtypeStruct((M, D), BF16),
        compiler_params=_cparams(("parallel",), 4 * _nbytes((tm, D), F32)),
        name="rmsnorm",
    )(x, w)


def _mm_kernel(a_ref, w_ref, *o_refs):
    acc = _dot(a_ref[...], w_ref[...])
    for o_ref in o_refs:
        o_ref[...] = acc.astype(o_ref.dtype)


def _matmul(a, w, layer, out_dtypes, tm=1024, tn=1024):
    M, K = a.shape
    N = w.shape[2]
    tm, tn = min(tm, M), min(tn, N)
    vm = 2 * (_nbytes((tm, K), BF16) + _nbytes((K, tn), BF16)) + _nbytes((tm, tn), F32)
    vm += sum(2 * _nbytes((tm, tn), d) for d in out_dtypes)
    outs = pl.pallas_call(
        _mm_kernel,
        grid=(M // tm, N // tn),
        in_specs=[pl.BlockSpec((tm, K), lambda i, j: (i, 0)),
                  pl.BlockSpec((None, K, tn), lambda i, j: (layer, 0, j))],
        out_specs=[pl.BlockSpec((tm, tn), lambda i, j: (i, j)) for _ in out_dtypes],
        out_shape=[jax.ShapeDtypeStruct((M, N), d) for d in out_dtypes],
        compiler_params=_cparams(("parallel", "parallel"), vm),
        name="matmul",
    )(a, w)
    return outs


def _proj_a_kernel(h_ref, wq_ref, wf_ref, wi_ref, wg_ref, lbl_ref, q_ref, k_ref, lf_ref, v_ref, g_ref,
                   *, layer):
    h = h_ref[...]
    aq = _dot(h, wq_ref[...])
    q_ref[...] = (aq * _sigmoid(aq)).astype(q_ref.dtype)
    lg = lbl_ref[...]
    e = jnp.exp(lg - jnp.max(lg, axis=0, keepdims=True))
    lb = jnp.sum(e[:layer + 1], axis=0, keepdims=True) / jnp.sum(e, axis=0, keepdims=True)
    f = lb + (1.0 - lb) * _sigmoid(_dot(h, wf_ref[...]))
    k_ref[...] = (1.0 - f).astype(k_ref.dtype)
    lf_ref[...] = jnp.log(f)
    v_ref[...] = _dot(h, wi_ref[...]).astype(v_ref.dtype)
    ag = _dot(h, wg_ref[...])
    g_ref[...] = (ag * _sigmoid(ag)).astype(g_ref.dtype)


def _proj_a(h, w, mixer, lb_logits, layer, act_dtype, tm=1024, tn=512):
    M, D = h.shape
    tm = min(tm, M)
    nj = D // tn
    nl = lb_logits.shape[0]
    wspec = lambda s: pl.BlockSpec((None, D, tn), lambda i, j: (mixer, 0, j + s * nj))
    ospec = pl.BlockSpec((tm, tn), lambda i, j: (i, j))
    vm = 2 * (_nbytes((tm, D), BF16) + 4 * _nbytes((D, tn), BF16) + 5 * _nbytes((tm, tn), F32))
    vm += 4 * _nbytes((tm, tn), F32)
    return pl.pallas_call(
        functools.partial(_proj_a_kernel, layer=layer),
        grid=(M // tm, nj),
        in_specs=[pl.BlockSpec((tm, D), lambda i, j: (i, 0)), wspec(0), wspec(1), wspec(2), wspec(3),
                  pl.BlockSpec((nl, tn), lambda i, j: (0, j))],
        out_specs=[ospec] * 5,
        out_shape=[jax.ShapeDtypeStruct((M, D), act_dtype), jax.ShapeDtypeStruct((M, D), act_dtype),
                   jax.ShapeDtypeStruct((M, D), F32), jax.ShapeDtypeStruct((M, D), act_dtype),
                   jax.ShapeDtypeStruct((M, D), act_dtype)],
        compiler_params=_cparams(("parallel", "parallel"), vm),
        name="hgrn2_proj",
    )(h, w, w, w, w, lb_logits)


def _mm_res_kernel(a_ref, w_ref, x_ref, pw_ref, nw_ref, ox_ref, oh_ref):
    m = _dot(a_ref[...].astype(BF16), w_ref[...])
    y = x_ref[...] + _rms(m, pw_ref[...])
    ox_ref[...] = y
    oh_ref[...] = _rms(y, nw_ref[...]).astype(oh_ref.dtype)


def _mm_res(a, w, wl, x, post_w, next_w, layer, tm=512):
    M, K = a.shape
    D = w.shape[2]
    tm = min(tm, M)
    vm = 2 * (_nbytes((tm, K), a.dtype) + _nbytes((K, D), BF16) + 2 * _nbytes((tm, D), F32)
              + _nbytes((tm, D), BF16)) + 2 * _nbytes((tm, D), F32)
    return pl.pallas_call(
        _mm_res_kernel,
        grid=(M // tm,),
        in_specs=[pl.BlockSpec((tm, K), lambda i: (i, 0)),
                  pl.BlockSpec((None, K, D), lambda i: (wl, 0, 0)),
                  pl.BlockSpec((tm, D), lambda i: (i, 0)), _row(layer)(D), _row(layer)(D)],
        out_specs=[pl.BlockSpec((tm, D), lambda i: (i, 0)), pl.BlockSpec((tm, D), lambda i: (i, 0))],
        out_shape=[jax.ShapeDtypeStruct((M, D), F32), jax.ShapeDtypeStruct((M, D), BF16)],
        compiler_params=_cparams(("parallel",), vm),
        name="matmul_norm_residual",
    )(a, w, x, post_w, next_w)


def _mlp_kernel(h_ref, wu_ref, wd_ref, x_ref, pw_ref, nw_ref, ox_ref, *rest, emit_next):
    acc_ref = rest[-1]
    f = pl.program_id(1)

    @pl.when(f == 0)
    def _():
        acc_ref[...] = jnp.zeros_like(acc_ref)

    u = jnp.maximum(_dot(h_ref[...], wu_ref[...]), 0.0)
    acc_ref[...] += _dot((u * u).astype(BF16), wd_ref[...])

    @pl.when(f == pl.num_programs(1) - 1)
    def _():
        y = x_ref[...] + _rms(acc_ref[...], pw_ref[...])
        ox_ref[...] = y
        if emit_next:
            rest[0][...] = _rms(y, nw_ref[...]).astype(BF16)


def _mlp(h, w_up, w_down, x, post_w, next_w, layer, tm=512, tf=1024):
    M, D = h.shape
    FF = w_up.shape[2]
    tm = min(tm, M)
    emit_next = layer + 1 < next_w.shape[0]
    tile = lambda: pl.BlockSpec((tm, D), lambda i, f: (i, 0))
    out_specs = [tile()] + ([tile()] if emit_next else [])
    out_shape = [jax.ShapeDtypeStruct((M, D), F32)] + ([jax.ShapeDtypeStruct((M, D), BF16)] if emit_next else [])
    vm = 2 * (2 * _nbytes((tm, D), BF16) + 2 * _nbytes((D, tf), BF16) + 2 * _nbytes((tm, D), F32))
    vm += 2 * _nbytes((tm, D), F32) + 2 * _nbytes((tm, tf), F32)
    outs = pl.pallas_call(
        functools.partial(_mlp_kernel, emit_next=emit_next),
        grid=(M // tm, FF // tf),
        in_specs=[tile(),
                  pl.BlockSpec((None, D, tf), lambda i, f: (layer, 0, f)),
                  pl.BlockSpec((None, tf, D), lambda i, f: (layer, f, 0)),
                  tile(), _row(layer)(D), _row(layer + 1 if emit_next else layer)(D)],
        out_specs=out_specs,
        out_shape=out_shape,
        scratch_shapes=[pltpu.VMEM((tm, D), F32)],
        compiler_params=_cparams(("parallel", "arbitrary"), vm),
        name="relu2_mlp",
    )(h, w_up, w_down, x, post_w, next_w)
    return (outs[0], outs[1]) if emit_next else (outs[0], None)


def _split3(x):
    hi = x.astype(BF16)
    r1 = x - hi.astype(F32)
    mid = r1.astype(BF16)
    lo = (r1 - mid.astype(F32)).astype(BF16)
    return hi, mid, lo


def _decay_rows(d, rows):
    hi, mid, lo = _split3(d)
    r = lax.broadcasted_iota(jnp.int32, (rows, d.shape[-1]), 0)
    return jnp.where(r == 0, hi.astype(F32), jnp.where(r == 1, mid.astype(F32),
                                                       jnp.where(r == 2, lo.astype(F32), 0.0)))


def _gla_prompt_kernel(q_ref, k_ref, lf_ref, v_ref, g_ref, nw_ref, sel_ref, lvl_ref, o_ref, sout_ref,
                       s_scr, qs_scr, ks_scr, ke_scr, fac_scr):
    T = q_ref.shape[1]
    D = q_ref.shape[2]
    H = D // HEAD_DIM
    SUB = GLA_SUB
    nb = T // SUB
    c_idx = pl.program_id(1)

    @pl.when(c_idx == 0)
    def _():
        s_scr[...] = jnp.zeros_like(s_scr)

    chunks = []
    c = 2 * SUB
    while c < T:
        chunks.append(c)
        c *= 2
    nf = 2 * len(chunks) + 3
    sums = _dot(sel_ref[...], jnp.concatenate(_split3(lf_ref[0]), axis=0))
    fac_scr[...] = jnp.exp(sums[T:])
    nq = len(chunks) + 1
    for i in range(nb):
        rows = slice(i * SUB, (i + 1) * SUB)
        r = sums[rows]
        q16 = q_ref[0, rows, :].astype(F32) * jnp.exp(r)
        kd = k_ref[0, rows, :].astype(F32) * jnp.exp(-r)
        f_row = lambda n: fac_scr[n * nb + i:n * nb + i + 1, :]
        qs_scr[0, rows, :] = (q16 * f_row(0)).astype(BF16)
        qs_scr[1, rows, :] = q16.astype(BF16)
        ks_scr[0, rows, :] = kd.astype(BF16)
        ks_scr[1, rows, :] = (kd * f_row(nq)).astype(BF16)
        for n in range(len(chunks)):
            qs_scr[n + 2, rows, :] = (q16 * f_row(n + 1)).astype(BF16)
            ks_scr[n + 2, rows, :] = (kd * f_row(nq + n + 1)).astype(BF16)
        ke_scr[rows, :] = (kd * f_row(nf - 1)).astype(BF16)
    ke_scr[T:, :] = _decay_rows(fac_scr[(nf - 1) * nb:(nf - 1) * nb + 1, :], SUB).astype(BF16)

    rr = lax.broadcasted_iota(jnp.int32, (SUB, 2 * HEAD_DIM), 0)
    cc = lax.broadcasted_iota(jnp.int32, (SUB, 2 * HEAD_DIM), 1)
    pick = jnp.where((rr < 3) & (cc >= HEAD_DIM), 1.0, 0.0).astype(BF16)
    zeros = jnp.zeros((T, HEAD_DIM), BF16)
    nw = nw_ref[...]

    def head(h, s_old):
        hs = pl.ds(pl.multiple_of(h * HEAD_DIM, HEAD_DIM), HEAD_DIM)
        lvl = lvl_ref[...]
        r = _dot_nt(qs_scr[1, :, hs], ks_scr[0:2, :, hs].reshape(2 * T, HEAD_DIM))
        a = jnp.where(lvl == 1, r[:, :T], jnp.where(lvl == 2, r[:, T:], 0.0))
        for n in range(len(chunks)):
            a = jnp.where(lvl == n + 3, _dot_nt(qs_scr[n + 2, :, hs], ks_scr[n + 2, :, hs]), a)
        vh = v_ref[0, :, hs]
        o = _dot(jnp.concatenate([a.astype(BF16), qs_scr[0, :, hs]], axis=1),
                 jnp.concatenate([vh, s_old.astype(BF16)], axis=0))
        rhs = jnp.concatenate([jnp.concatenate([vh, zeros], axis=1), pick], axis=0)
        upd = _dot_tn(ke_scr[:, hs], rhs)
        s_new = s_old * upd[:, HEAD_DIM:] + upd[:, :HEAD_DIM]
        return s_new, (_rms(o, nw) * g_ref[0, :, hs].astype(F32)).astype(o_ref.dtype)

    def heads(n, carry):
        hh = [n * GLA_HEAD_UNROLL + u for u in range(GLA_HEAD_UNROLL)]
        res = [head(h, s_old) for h, s_old in zip(hh, [s_scr[h] for h in hh])]
        for h, (s_new, o) in zip(hh, res):
            s_scr[h] = s_new
            o_ref[0, :, pl.ds(pl.multiple_of(h * HEAD_DIM, HEAD_DIM), HEAD_DIM)] = o
        return carry

    lax.fori_loop(0, H // GLA_HEAD_UNROLL, heads, 0)

    @pl.when(c_idx == pl.num_programs(1) - 1)
    def _():
        sout_ref[0] = s_scr[...]


def _gla_range_matrix(T, SUB):
    chunks = []
    c = 2 * SUB
    while c < T:
        chunks.append(c)
        c *= 2
    nb = T // SUB
    t = np.arange(T)
    rows = [((t[None, :] <= t[:, None]) & (t[None, :] // SUB == t[:, None] // SUB))]
    start = np.arange(nb)[:, None] * SUB
    for C in [T] + chunks:
        rows.append((t[None, :] >= (start // C) * C) & (t[None, :] < start))
    for C in [SUB] + chunks + [T]:
        rows.append((t[None, :] >= start) & (t[None, :] < (start // C + 1) * C))
    sel = np.concatenate(rows, axis=0).astype(np.float32)
    row, col = t[:, None], t[None, :]
    lvl = np.where((row // SUB == col // SUB) & (col <= row), 1, 0)
    for j, C in enumerate([SUB] + chunks):
        lvl = np.where(((row // C) % 2 == 1) & (col // C == row // C - 1), 2 + j, lvl)
    return sel, lvl.astype(np.int32), len(chunks)


def _gla_prompt(q, k, lf, v, g, norm_w, mixer, B, L):
    D = q.shape[-1]
    H = D // HEAD_DIM
    T = GLA_BLOCK
    assert T == HEAD_DIM and H % GLA_HEAD_UNROLL == 0
    sel, lvl, nchunks = _gla_range_matrix(T, GLA_SUB)
    sel3 = jnp.asarray(np.concatenate([sel] * 3, axis=1), BF16)
    nfac = sel.shape[0] - T
    r3 = lambda a: a.reshape(B, L, D)
    blk = lambda: pl.BlockSpec((1, T, D), lambda b, c: (b, c, 0))
    const = lambda a: pl.BlockSpec(a.shape, lambda b, c: (0, 0))
    vm = 2 * (5 * _nbytes((T, D), BF16) + _nbytes((T, D), F32)) + 3 * _nbytes((H, HEAD_DIM, HEAD_DIM), F32)
    vm += (2 * nchunks + 6) * _nbytes((T, D), BF16) + 8 * _nbytes((T, D), F32)
    o, s = pl.pallas_call(
        _gla_prompt_kernel,
        grid=(B, L // T),
        in_specs=[blk(), blk(), blk(), blk(), blk(), _row(mixer)(HEAD_DIM), const(sel3), const(lvl)],
        out_specs=[blk(), pl.BlockSpec((1, H, HEAD_DIM, HEAD_DIM), lambda b, c: (b, 0, 0, 0))],
        out_shape=[jax.ShapeDtypeStruct((B, L, D), BF16),
                   jax.ShapeDtypeStruct((B, H, HEAD_DIM, HEAD_DIM), F32)],
        scratch_shapes=[pltpu.VMEM((H, HEAD_DIM, HEAD_DIM), F32),
                        pltpu.VMEM((nchunks + 2, T, D), BF16),
                        pltpu.VMEM((nchunks + 2, T, D), BF16),
                        pltpu.VMEM((T + GLA_SUB, D), BF16),
                        pltpu.VMEM((nfac, D), F32)],
        compiler_params=_cparams(("parallel", "arbitrary"), vm),
        name="hgrn2_recurrence_prompt",
    )(r3(q), r3(k), r3(lf), r3(v), r3(g), norm_w, sel3, jnp.asarray(lvl))
    return o.reshape(B * L, D), s


def _gla_step_kernel(q_ref, k_ref, lf_ref, v_ref, g_ref, s0_ref, nw_ref, o_ref, sout_ref,
                     qs_scr, ks_scr, ke_scr, ds_scr):
    bb, T, D = q_ref.shape
    H = D // HEAD_DIM
    R = bb * T
    pos = lax.broadcasted_iota(jnp.int32, (bb, T, D), 1)
    b = lf_ref[...]
    sh = 1
    while sh < T:
        b = b + jnp.where(pos >= sh, pltpu.roll(b, sh, 1), 0.0)
        sh *= 2
    b_last = b[:, T - 1:T, :]
    k = k_ref[...]
    qs_scr[...] = q_ref[...] * jnp.exp(b)
    ks_scr[...] = k * jnp.exp(-b)
    ke_scr[...] = k * jnp.exp(b_last - b)
    d_hi, d_mid, d_lo = (t.astype(F32) for t in _split3(jnp.exp(b_last)))
    ds_scr[...] = jnp.where(pos == 0, d_hi, jnp.where(pos == 1, d_mid, jnp.where(pos == 2, d_lo, 0.0)))

    row = lax.broadcasted_iota(jnp.int32, (R, R), 0)
    col = lax.broadcasted_iota(jnp.int32, (R, R), 1)
    same_seq_causal = (row // T == col // T) & (col <= row)
    rr = lax.broadcasted_iota(jnp.int32, (T, 2 * HEAD_DIM), 0)
    cc = lax.broadcasted_iota(jnp.int32, (T, 2 * HEAD_DIM), 1)
    pick = jnp.where((rr < 3) & (cc >= HEAD_DIM), 1.0, 0.0)
    zeros = jnp.zeros((T, HEAD_DIM), F32)
    nw = nw_ref[...]

    def head(h, carry):
        hs = pl.ds(pl.multiple_of(h * HEAD_DIM, HEAD_DIM), HEAD_DIM)
        qd = qs_scr[:, :, hs].reshape(R, HEAD_DIM)
        kd = ks_scr[:, :, hs].reshape(R, HEAD_DIM)
        vh = v_ref[:, :, hs].reshape(R, HEAD_DIM)
        a = jnp.where(same_seq_causal, _dot_nt(qd.astype(BF16), kd.astype(BF16)), 0.0)
        o_intra = _dot(a.astype(BF16), vh.astype(BF16))
        outs = []
        for i in range(bb):
            s_old = s0_ref[i, h]
            outs.append(o_intra[i * T:(i + 1) * T]
                        + _dot(qd[i * T:(i + 1) * T].astype(BF16), s_old.astype(BF16)))
            lhs = jnp.concatenate([ke_scr[i, :, hs], ds_scr[i, :, hs]], axis=0).astype(BF16)
            rhs = jnp.concatenate([jnp.concatenate([vh[i * T:(i + 1) * T], zeros], axis=1), pick],
                                  axis=0).astype(BF16)
            upd = _dot_tn(lhs, rhs)
            sout_ref[i, h] = s_old * upd[:, HEAD_DIM:] + upd[:, :HEAD_DIM]
        o = jnp.concatenate(outs, axis=0)
        on = _rms(o, nw) * g_ref[:, :, hs].reshape(R, HEAD_DIM)
        o_ref[:, :, hs] = on.reshape(bb, T, HEAD_DIM).astype(o_ref.dtype)
        return carry

    lax.fori_loop(0, H, head, 0)


def _gla_step(q, k, lf, v, g, s0, norm_w, mixer, B, L, bb=8):
    D = q.shape[-1]
    H = D // HEAD_DIM
    assert L == 8 and B % bb == 0
    r3 = lambda a: a.reshape(B, L, D)
    blk = lambda: pl.BlockSpec((bb, L, D), lambda b: (b, 0, 0))
    sblk = lambda: pl.BlockSpec((bb, H, HEAD_DIM, HEAD_DIM), lambda b: (b, 0, 0, 0))
    vm = 2 * (6 * _nbytes((bb, L, D), F32) + 2 * _nbytes((bb, H, HEAD_DIM, HEAD_DIM), F32))
    vm += 12 * _nbytes((bb, L, D), F32)
    o, s = pl.pallas_call(
        _gla_step_kernel,
        grid=(B // bb,),
        in_specs=[blk(), blk(), blk(), blk(), blk(),
                  pl.BlockSpec((None, bb, H, HEAD_DIM, HEAD_DIM), lambda b: (mixer, b, 0, 0, 0)),
                  _row(mixer)(HEAD_DIM)],
        out_specs=[blk(), sblk()],
        out_shape=[jax.ShapeDtypeStruct((B, L, D), F32),
                   jax.ShapeDtypeStruct((B, H, HEAD_DIM, HEAD_DIM), F32)],
        scratch_shapes=[pltpu.VMEM((bb, L, D), F32)] * 4,
        compiler_params=_cparams(("parallel",), vm),
        name="hgrn2_recurrence_step",
    )(r3(q), r3(k), r3(lf), r3(v), r3(g), s0, norm_w)
    return o.reshape(B * L, D), s


def _pool_kernel(*refs, pos0, has_halo):
    if has_halo:
        u_ref, halo_ref, buf_ref, pw_ref, ps_ref, o_ref, nb_ref, ext = refs
    else:
        u_ref, buf_ref, pw_ref, ps_ref, o_ref, nb_ref, ext = refs
    bb, tl, D = u_ref.shape
    G = len(POOL_WINDOWS)
    gc = D // G
    l_idx = pl.program_id(1)
    ext[:, HALO:, :] = u_ref[...]
    if has_halo:
        @pl.when(l_idx == 0)
        def _():
            ext[:, :HALO, :] = buf_ref[...]

        @pl.when(l_idx > 0)
        def _():
            ext[:, :HALO, :] = halo_ref[...]
    else:
        ext[:, :HALO, :] = buf_ref[...]

    pos = pos0 + l_idx * tl + lax.broadcasted_iota(jnp.int32, (tl, gc), 0)
    for gi, w in enumerate(POOL_WINDOWS):
        ls = slice(gi * gc, (gi + 1) * gc)
        cnt = jnp.minimum(w, pos + 1).astype(F32)
        rows = []
        for i in range(bb):
            u = ext[i, HALO:HALO + tl, ls]
            s = u
            for d in range(1, w):
                s = s + ext[i, HALO - d:HALO - d + tl, ls]
            rows.append(s / cnt - u)
        pooled = rows[0] if bb == 1 else jnp.concatenate(rows, axis=0)
        mixed = (_dot(pooled.astype(BF16), pw_ref[gi]) * ps_ref[:, ls]).astype(o_ref.dtype)
        for i in range(bb):
            o_ref[i, :, ls] = mixed[i * tl:(i + 1) * tl]

    @pl.when(l_idx == pl.num_programs(1) - 1)
    def _():
        nb_ref[...] = ext[:, tl:tl + HALO, :]


def _pool(u, buf, pos0, pool_w, pool_scale, mixer, B, L, bb, tl, out_dtype):
    D = u.shape[-1]
    G = len(POOL_WINDOWS)
    gc = D // G
    tl = min(tl, L)
    u3 = u.reshape(B, L, D)
    has_halo = L > tl
    step = tl // HALO
    in_specs = [pl.BlockSpec((bb, tl, D), lambda b, l: (b, l, 0))]
    args = [u3]
    if has_halo:
        in_specs.append(pl.BlockSpec((bb, HALO, D), lambda b, l: (b, jnp.maximum(l * step - 1, 0), 0)))
        args.append(u3)
    in_specs += [pl.BlockSpec((None, bb, HALO, D), lambda b, l: (mixer, b, 0, 0)),
                 pl.BlockSpec((None, G, gc, gc), lambda b, l: (mixer, 0, 0, 0)),
                 _row(mixer)(D)]
    args += [buf, pool_w, pool_scale]
    vm = 2 * (_nbytes((bb, tl, D), F32) + 3 * _nbytes((bb, HALO, D), F32) + _nbytes((G, gc, gc), BF16)
              + _nbytes((bb, tl, D), BF16)) + _nbytes((bb, tl + HALO, D), F32) + 4 * _nbytes((bb, tl, D), F32)
    mixed, newbuf = pl.pallas_call(
        functools.partial(_pool_kernel, pos0=pos0, has_halo=has_halo),
        grid=(B // bb, L // tl),
        in_specs=in_specs,
        out_specs=[pl.BlockSpec((bb, tl, D), lambda b, l: (b, l, 0)),
                   pl.BlockSpec((bb, HALO, D), lambda b, l: (b, 0, 0))],
        out_shape=[jax.ShapeDtypeStruct((B, L, D), out_dtype), jax.ShapeDtypeStruct((B, HALO, D), F32)],
        scratch_shapes=[pltpu.VMEM((bb, tl + HALO, D), F32)],
        compiler_params=_cparams(("parallel", "arbitrary"), vm),
        name="causal_pool",
    )(*args)
    return mixed.reshape(B * L, D), newbuf


def _attn_kernel(q_ref, k_ref, v_ref, o_ref, *, heads, flat):
    bb, tq, D = q_ref.shape
    R = D // LANES
    hd = D // heads
    nchunk = hd // LANES
    n_mem = k_ref.shape[1] // R if flat else k_ref.shape[1]
    scale = hd ** -0.5
    for i in range(bb):
        for h in range(heads):
            hs = slice(h * hd, (h + 1) * hd)

            def head_of(ref):
                if not flat:
                    return ref[i, :, hs].astype(BF16)
                parts = [ref[i, pl.ds(c * heads + h, n_mem, stride=R), :] for c in range(nchunk)]
                return jnp.concatenate(parts, axis=-1).astype(BF16)

            s = _dot_nt(q_ref[i, :, hs].astype(BF16), head_of(k_ref)) * scale
            p = jnp.exp(s - jnp.max(s, axis=-1, keepdims=True))
            den = jnp.sum(p, axis=-1, keepdims=True)
            o_ref[i, :, hs] = (_dot(p.astype(BF16), head_of(v_ref)) / den).astype(o_ref.dtype)


def _attn(q, mk, mv, layer, B, L, heads, bb, tq, out_dtype, flat):
    D = q.shape[-1]
    tq = min(tq, L)
    kv_shape = (bb,) + mk.shape[2:]
    kv_blk = (None,) + kv_shape
    vm = 2 * (_nbytes((bb, tq, D), q.dtype) + 2 * _nbytes(kv_shape, mk.dtype) + _nbytes((bb, tq, D), out_dtype))
    vm += 2 * _nbytes(kv_shape, BF16) + 8 * _nbytes((tq, D // heads), F32)
    o = pl.pallas_call(
        functools.partial(_attn_kernel, heads=heads, flat=flat),
        grid=(B // bb, L // tq),
        in_specs=[pl.BlockSpec((bb, tq, D), lambda b, l: (b, l, 0)),
                  pl.BlockSpec(kv_blk, lambda b, l: (layer, b, 0, 0)),
                  pl.BlockSpec(kv_blk, lambda b, l: (layer, b, 0, 0))],
        out_specs=pl.BlockSpec((bb, tq, D), lambda b, l: (b, l, 0)),
        out_shape=jax.ShapeDtypeStruct((B, L, D), out_dtype),
        compiler_params=_cparams(("parallel", "parallel"), vm),
        name="memory_cross_attention",
    )(q.reshape(B, L, D), mk, mv)
    return o.reshape(B * L, D)


def _mem_proj_kernel(x_ref, nw_ref, w_ref, flat_ref, bf_ref, *, heads):
    tm, D = x_ref.shape
    R = D // LANES
    nchunk = R // heads
    acc = _dot(_rms(x_ref[...], nw_ref[...]).astype(BF16), w_ref[...])
    bf_ref[...] = acc.astype(bf_ref.dtype)
    for h in range(heads):
        for c in range(nchunk):
            j = h * nchunk + c
            flat_ref[pl.ds(c * heads + h, tm, stride=R), :] = acc[:, j * LANES:(j + 1) * LANES]


def _mem_proj(mem, norm_w, w, half, heads, tm=256):
    M, D = mem.shape
    depth = w.shape[0]
    R = D // LANES
    tm = min(tm, M)
    nt = M // tm
    vm = 2 * (_nbytes((tm, D), F32) + _nbytes((D, D), BF16) + _nbytes((tm, D), F32) + _nbytes((tm, D), BF16))
    vm += 2 * _nbytes((tm, D), F32)
    return pl.pallas_call(
        functools.partial(_mem_proj_kernel, heads=heads),
        grid=(depth, nt),
        in_specs=[pl.BlockSpec((tm, D), lambda l, i: (i, 0)),
                  pl.BlockSpec((None, 1, D), lambda l, i: (l, 0, 0)),
                  pl.BlockSpec((None, D, D), lambda l, i: (l, 0, half))],
        out_specs=[pl.BlockSpec((tm * R, LANES), lambda l, i: (l * nt + i, 0)),
                   pl.BlockSpec((None, tm, D), lambda l, i: (l, i, 0))],
        out_shape=[jax.ShapeDtypeStruct((depth * M * R, LANES), F32),
                   jax.ShapeDtypeStruct((depth, M, D), BF16)],
        compiler_params=_cparams(("parallel", "parallel"), vm),
        name="memory_kv_projection",
    )(mem, norm_w, w)


def _trunk(x, B, L, pos0, S_in, buf_in, mk, mv, p, prompt):
    depth = p["w_xq"].shape[0]
    heads = p["mem_heads"]
    hn = _rmsnorm(x, p["norm_mix_pre"], 0)
    S_out, buf_out = [], []
    act = BF16 if prompt else F32
    for l in range(depth):
        j = l // 2
        if l % 2 == 0:
            q, k, lf, v, g = _proj_a(hn, p["w_in_a"], j, p["hg_lb_logits"], l, act)
            if prompt:
                o, S = _gla_prompt(q, k, lf, v, g, p["hg_norm"], j, B, L)
            else:
                o, S = _gla_step(q, k, lf, v, g, S_in, p["hg_norm"], j, B, L)
            S_out.append(S)
            x, hn = _mm_res(o, p["w_out_a"], j, x, p["norm_mix_post"], p["norm_x_pre"], l)
        else:
            (u,) = _matmul(hn, p["w_in_b"], j, [F32])
            bb, tl = (1, 512) if prompt else (16, L)
            mixed, nb = _pool(u, buf_in, pos0, p["pool_w"], p["pool_scale"], j, B, L, bb, tl, act)
            buf_out.append(nb[:, 1:, :])
            x, hn = _mm_res(mixed, p["w_out_b"], j, x, p["norm_mix_post"], p["norm_x_pre"], l)
        (qx,) = _matmul(hn, p["w_xq"], l, [act])
        bb, tq = (1, 512) if prompt else (2, L)
        a = _attn(qx, mk, mv, l, B, L, heads, bb, tq, act, flat=not prompt)
        x, hn = _mm_res(a, p["w_xo"], l, x, p["norm_x_post"], p["norm_mlp_pre"], l)
        x, hn = _mlp(hn, p["w_up"], p["w_down"], x, p["norm_mlp_post"], p["norm_mix_pre"], l)
    return x, jnp.stack(S_out), jnp.stack(buf_out)


def kernel(x_prompt, x_sample, state_hgrn, state_pool, cache_mem_k, cache_mem_v, mem_prompt, w_in_a, hg_lb_logits, hg_norm, w_out_a, w_in_b, pool_w, pool_scale, w_out_b, norm_mem, w_xq, w_xkv, w_xo, norm_mix_pre, norm_mix_post, norm_x_pre, norm_x_post, norm_mlp_pre, norm_mlp_post, w_up, w_down):
    B, L, D = x_prompt.shape
    Bs, Ls, _ = x_sample.shape
    depth = w_xq.shape[0]
    n_mem, heads = cache_mem_k.shape[2], cache_mem_k.shape[3]
    assert L % GLA_BLOCK == 0 and Ls <= GLA_SUB and D % (HEAD_DIM * len(POOL_WINDOWS)) == 0
    assert state_pool.shape[2] == HALO - 1
    bf = lambda a: a.astype(BF16)
    rows = lambda a: a.reshape(a.shape[0], 1, a.shape[1])
    p = dict(w_in_a=bf(w_in_a), hg_lb_logits=hg_lb_logits, hg_norm=rows(hg_norm), w_out_a=bf(w_out_a),
             w_in_b=bf(w_in_b), pool_w=bf(pool_w), pool_scale=rows(pool_scale), w_out_b=bf(w_out_b),
             w_xq=bf(w_xq), w_xo=bf(w_xo), norm_mix_pre=rows(norm_mix_pre), norm_mix_post=rows(norm_mix_post),
             norm_x_pre=rows(norm_x_pre), norm_x_post=rows(norm_x_post), norm_mlp_pre=rows(norm_mlp_pre),
             norm_mlp_post=rows(norm_mlp_post), w_up=bf(w_up), w_down=bf(w_down), mem_heads=heads)
    R = D // LANES
    nchunk = R // heads

    def cache_view(flat, nb):
        c = flat.reshape(depth, nb, n_mem, nchunk, heads, LANES)
        return jnp.swapaxes(c, 3, 4).reshape(depth, nb, n_mem, heads, D // heads)

    def flat_view(cache):
        nb = cache.shape[1]
        c = cache.reshape(depth, nb, n_mem, heads, nchunk, LANES)
        return jnp.swapaxes(c, 3, 4).reshape(depth, nb, n_mem * R, LANES)

    mem2d = mem_prompt.reshape(B * n_mem, D)
    w_xkv_bf = bf(w_xkv)
    k_flat, k_bf = _mem_proj(mem2d, rows(norm_mem), w_xkv_bf, 0, heads)
    v_flat, v_bf = _mem_proj(mem2d, rows(norm_mem), w_xkv_bf, 1, heads)
    cache_mem_k_prompt = cache_view(k_flat, B)
    cache_mem_v_prompt = cache_view(v_flat, B)
    mk_bf = k_bf.reshape(depth, B, n_mem, D)
    mv_bf = v_bf.reshape(depth, B, n_mem, D)

    n_b = state_pool.shape[0]
    buf0 = jnp.zeros((n_b, B, HALO, D), F32)
    y_prompt, state_hgrn_prompt, state_pool_prompt = _trunk(
        x_prompt.reshape(B * L, D), B, L, 0, None, buf0, mk_bf, mv_bf, p, True)

    buf_s = jnp.pad(state_pool, ((0, 0), (0, 0), (1, 0), (0, 0)))
    ck = flat_view(cache_mem_k)
    cv = flat_view(cache_mem_v)
    y_sample, state_hgrn_sample, state_pool_sample = _trunk(
        x_sample.reshape(Bs * Ls, D), Bs, Ls, PAST_LEN, state_hgrn, buf_s, ck, cv, p, False)

    return (y_prompt.reshape(B, L, D), y_sample.reshape(Bs, Ls, D), state_hgrn_prompt, state_pool_prompt,
            cache_mem_k_prompt, cache_mem_v_prompt, state_hgrn_sample, state_pool_sample)
```

```python
import functools

import numpy as np

import jax
import jax.numpy as jnp
from jax import lax
from jax.experimental import pallas as pl
from jax.experimental.pallas import tpu as pltpu

F32 = jnp.float32
BF16 = jnp.bfloat16
EPS = 1e-6
PAST_LEN = 16384
POOL_WINDOWS = (2, 4, 8, 16)
LANES = 128
SUBL = 8
HEAD_DIM = 128
GLA_BLOCK = 128
GLA_SUB = 16
HALO = 16
MM_RES_SPLIT = 4

V7X_VMEM_BYTES = 64 * 1024 * 1024
VMEM_CAP = V7X_VMEM_BYTES - 6 * 1024 * 1024


def _cparams(sem, vmem_bytes):
    return pltpu.CompilerParams(dimension_semantics=sem,
                                vmem_limit_bytes=int(min(max(vmem_bytes, 16 * 1024 * 1024), VMEM_CAP)))


def _nbytes(shape, dtype):
    n = 1
    for s in shape:
        n *= s
    return n * jnp.dtype(dtype).itemsize


def _rms(v, w):
    ms = jnp.mean(v * v, axis=-1, keepdims=True)
    return v * lax.rsqrt(ms + EPS) * w


def _sigmoid(x):
    return 1.0 / (1.0 + jnp.exp(-x))


def _dot(a, b):
    return jnp.dot(a, b, preferred_element_type=F32)


def _dot_nt(a, b):
    return lax.dot_general(a, b, (((1,), (1,)), ((), ())), preferred_element_type=F32)


def _dot_tn(a, b, precision=None):
    return lax.dot_general(a, b, (((0,), (0,)), ((), ())), preferred_element_type=F32,
                           precision=precision)


def _rmsnorm_kernel(x_ref, w_ref, o_ref):
    o_ref[...] = _rms(x_ref[...], w_ref[...]).astype(o_ref.dtype)


def _row(layer):
    return lambda width: pl.BlockSpec((None, 1, width), lambda *_: (layer, 0, 0))


def _rmsnorm(x, w, layer, tm=512):
    M, D = x.shape
    tm = min(tm, M)
    return pl.pallas_call(
        _rmsnorm_kernel,
        grid=(M // tm,),
        in_specs=[pl.BlockSpec((tm, D), lambda i: (i, 0)), _row(layer)(D)],
        out_specs=pl.BlockSpec((tm, D), lambda i: (i, 0)),
        out_shape=jax.ShapeDtypeStruct((M, D), BF16),
        compiler_params=_cparams(("parallel",), 4 * _nbytes((tm, D), F32)),
        name="rmsnorm",
    )(x, w)


def _mm_kernel(a_ref, w_ref, *o_refs):
    acc = _dot(a_ref[...], w_ref[...])
    for o_ref in o_refs:
        o_ref[...] = acc.astype(o_ref.dtype)


def _matmul(a, w, layer, out_dtypes, tm=1024, tn=1024):
    M, K = a.shape
    N = w.shape[2]
    tm, tn = min(tm, M), min(tn, N)
    vm = 2 * (_nbytes((tm, K), BF16) + _nbytes((K, tn), BF16)) + _nbytes((tm, tn), F32)
    vm += sum(2 * _nbytes((tm, tn), d) for d in out_dtypes)
    outs = pl.pallas_call(
        _mm_kernel,
        grid=(M // tm, N // tn),
        in_specs=[pl.BlockSpec((tm, K), lambda i, j: (i, 0)),
                  pl.BlockSpec((None, K, tn), lambda i, j: (layer, 0, j))],
        out_specs=[pl.BlockSpec((tm, tn), lambda i, j: (i, j)) for _ in out_dtypes],
        out_shape=[jax.ShapeDtypeStruct((M, N), d) for d in out_dtypes],
        compiler_params=_cparams(("parallel", "parallel"), vm),
        name="matmul",
    )(a, w)
    return outs


def _proj_a_kernel(h_ref, wq_ref, wf_ref, wi_ref, wg_ref, lbl_ref, q_ref, k_ref, lf_ref, v_ref, g_ref,
                   *, layer):
    h = h_ref[...]
    aq = _dot(h, wq_ref[...])
    q_ref[...] = (aq * _sigmoid(aq)).astype(q_ref.dtype)
    lg = lbl_ref[...]
    e = jnp.exp(lg - jnp.max(lg, axis=0, keepdims=True))
    lb = jnp.sum(e[:layer + 1], axis=0, keepdims=True) / jnp.sum(e, axis=0, keepdims=True)
    f = lb + (1.0 - lb) * _sigmoid(_dot(h, wf_ref[...]))
    k_ref[...] = (1.0 - f).astype(k_ref.dtype)
    lf_ref[...] = jnp.log(f)
    v_ref[...] = _dot(h, wi_ref[...]).astype(v_ref.dtype)
    ag = _dot(h, wg_ref[...])
    g_ref[...] = (ag * _sigmoid(ag)).astype(g_ref.dtype)


def _proj_a(h, w, mixer, lb_logits, layer, act_dtype, tm=1024, tn=512):
    M, D = h.shape
    tm = min(tm, M)
    nj = D // tn
    nl = lb_logits.shape[0]
    wspec = lambda s: pl.BlockSpec((None, D, tn), lambda i, j: (mixer, 0, j + s * nj))
    ospec = pl.BlockSpec((tm, tn), lambda i, j: (i, j))
    vm = 2 * (_nbytes((tm, D), BF16) + 4 * _nbytes((D, tn), BF16) + 5 * _nbytes((tm, tn), F32))
    vm += 4 * _nbytes((tm, tn), F32)
    return pl.pallas_call(
        functools.partial(_proj_a_kernel, layer=layer),
        grid=(M // tm, nj),
        in_specs=[pl.BlockSpec((tm, D), lambda i, j: (i, 0)), wspec(0), wspec(1), wspec(2), wspec(3),
                  pl.BlockSpec((nl, tn), lambda i, j: (0, j))],
        out_specs=[ospec] * 5,
        out_shape=[jax.ShapeDtypeStruct((M, D), act_dtype), jax.ShapeDtypeStruct((M, D), act_dtype),
                   jax.ShapeDtypeStruct((M, D), F32), jax.ShapeDtypeStruct((M, D), act_dtype),
                   jax.ShapeDtypeStruct((M, D), act_dtype)],
        compiler_params=_cparams(("parallel", "parallel"), vm),
        name="hgrn2_proj",
    )(h, w, w, w, w, lb_logits)


def _mm_res_kernel(a_ref, w_ref, x_ref, pw_ref, nw_ref, ox_ref, oh_ref):
    tm = a_ref.shape[0]
    nsplit = MM_RES_SPLIT if tm % (8 * MM_RES_SPLIT) == 0 and tm // MM_RES_SPLIT >= 128 else 1
    for p in range(nsplit):
        rows = slice(p * tm // nsplit, (p + 1) * tm // nsplit)
        m = _dot(a_ref[rows, :].astype(BF16), w_ref[...])
        y = x_ref[rows, :] + _rms(m, pw_ref[...])
        ox_ref[rows, :] = y
        oh_ref[rows, :] = _rms(y, nw_ref[...]).astype(oh_ref.dtype)


def _mm_res(a, w, wl, x, post_w, next_w, layer, tm=512):
    M, K = a.shape
    D = w.shape[2]
    tm = min(tm, M)
    vm = 2 * (_nbytes((tm, K), a.dtype) + _nbytes((K, D), BF16) + 2 * _nbytes((tm, D), F32)
              + _nbytes((tm, D), BF16)) + 2 * _nbytes((tm, D), F32)
    return pl.pallas_call(
        _mm_res_kernel,
        grid=(M // tm,),
        in_specs=[pl.BlockSpec((tm, K), lambda i: (i, 0)),
                  pl.BlockSpec((None, K, D), lambda i: (wl, 0, 0)),
                  pl.BlockSpec((tm, D), lambda i: (i, 0)), _row(layer)(D), _row(layer)(D)],
        out_specs=[pl.BlockSpec((tm, D), lambda i: (i, 0)), pl.BlockSpec((tm, D), lambda i: (i, 0))],
        out_shape=[jax.ShapeDtypeStruct((M, D), F32), jax.ShapeDtypeStruct((M, D), BF16)],
        compiler_params=_cparams(("parallel",), vm),
        name="matmul_norm_residual",
    )(a, w, x, post_w, next_w)


def _mlp_kernel(h_ref, wu_ref, wd_ref, x_ref, pw_ref, nw_ref, ox_ref, *rest, emit_next):
    acc_ref = rest[-1]
    f = pl.program_id(1)

    @pl.when(f == 0)
    def _():
        acc_ref[...] = jnp.zeros_like(acc_ref)

    u = jnp.maximum(_dot(h_ref[...], wu_ref[...]), 0.0)
    acc_ref[...] += _dot((u * u).astype(BF16), wd_ref[...])

    @pl.when(f == pl.num_programs(1) - 1)
    def _():
        y = x_ref[...] + _rms(acc_ref[...], pw_ref[...])
        ox_ref[...] = y
        if emit_next:
            rest[0][...] = _rms(y, nw_ref[...]).astype(BF16)


def _mlp(h, w_up, w_down, x, post_w, next_w, layer, tm=512, tf=1024):
    M, D = h.shape
    FF = w_up.shape[2]
    tm = min(tm, M)
    emit_next = layer + 1 < next_w.shape[0]
    tile = lambda: pl.BlockSpec((tm, D), lambda i, f: (i, 0))
    out_specs = [tile()] + ([tile()] if emit_next else [])
    out_shape = [jax.ShapeDtypeStruct((M, D), F32)] + ([jax.ShapeDtypeStruct((M, D), BF16)] if emit_next else [])
    vm = 2 * (2 * _nbytes((tm, D), BF16) + 2 * _nbytes((D, tf), BF16) + 2 * _nbytes((tm, D), F32))
    vm += 2 * _nbytes((tm, D), F32) + 2 * _nbytes((tm, tf), F32)
    outs = pl.pallas_call(
        functools.partial(_mlp_kernel, emit_next=emit_next),
        grid=(M // tm, FF // tf),
        in_specs=[tile(),
                  pl.BlockSpec((None, D, tf), lambda i, f: (layer, 0, f)),
                  pl.BlockSpec((None, tf, D), lambda i, f: (layer, f, 0)),
                  tile(), _row(layer)(D), _row(layer + 1 if emit_next else layer)(D)],
        out_specs=out_specs,
        out_shape=out_shape,
        scratch_shapes=[pltpu.VMEM((tm, D), F32)],
        compiler_params=_cparams(("parallel", "arbitrary"), vm),
        name="relu2_mlp",
    )(h, w_up, w_down, x, post_w, next_w)
    return (outs[0], outs[1]) if emit_next else (outs[0], None)


def _split3(x):
    hi = x.astype(BF16)
    r1 = x - hi.astype(F32)
    mid = r1.astype(BF16)
    lo = (r1 - mid.astype(F32)).astype(BF16)
    return hi, mid, lo


def _decay_rows(d, rows):
    hi, mid, lo = _split3(d)
    r = lax.broadcasted_iota(jnp.int32, (rows, d.shape[-1]), 0)
    return jnp.where(r == 0, hi.astype(F32), jnp.where(r == 1, mid.astype(F32),
                                                       jnp.where(r == 2, lo.astype(F32), 0.0)))


def _gla_prompt_kernel(q_ref, k_ref, lf_ref, v_ref, g_ref, nw_ref, sel_ref, lvl_ref, o_ref, sout_ref,
                       s_scr, qs_scr, ks_scr, ke_scr, fac_scr, a_scr):
    T = q_ref.shape[1]
    D = q_ref.shape[2]
    H = D // HEAD_DIM
    SUB = GLA_SUB
    nb = T // SUB
    c_idx = pl.program_id(1)

    @pl.when(c_idx == 0)
    def _():
        s_scr[...] = jnp.zeros_like(s_scr)

    chunks = []
    c = 2 * SUB
    while c < T:
        chunks.append(c)
        c *= 2
    nf = 2 * len(chunks) + 3
    sums = _dot(sel_ref[...], jnp.concatenate(_split3(lf_ref[0]), axis=0))
    fac_scr[...] = jnp.exp(sums[T:])
    nq = len(chunks) + 1
    for i in range(nb):
        rows = slice(i * SUB, (i + 1) * SUB)
        r = sums[rows]
        q16 = q_ref[0, rows, :].astype(F32) * jnp.exp(r)
        kd = k_ref[0, rows, :].astype(F32) * jnp.exp(-r)
        f_row = lambda n: fac_scr[n * nb + i:n * nb + i + 1, :]
        qs_scr[0, rows, :] = (q16 * f_row(0)).astype(BF16)
        qs_scr[1, rows, :] = q16.astype(BF16)
        ks_scr[0, rows, :] = kd.astype(BF16)
        ks_scr[1, rows, :] = (kd * f_row(nq)).astype(BF16)
        for n in range(len(chunks)):
            qs_scr[n + 2, rows, :] = (q16 * f_row(n + 1)).astype(BF16)
            ks_scr[n + 2, rows, :] = (kd * f_row(nq + n + 1)).astype(BF16)
        ke_scr[rows, :] = (kd * f_row(nf - 1)).astype(BF16)
    ke_scr[T:, :] = _decay_rows(fac_scr[(nf - 1) * nb:(nf - 1) * nb + 1, :], SUB).astype(BF16)

    rr = lax.broadcasted_iota(jnp.int32, (SUB, 2 * HEAD_DIM), 0)
    cc = lax.broadcasted_iota(jnp.int32, (SUB, 2 * HEAD_DIM), 1)
    pick = jnp.where((rr < 3) & (cc >= HEAD_DIM), 1.0, 0.0).astype(BF16)
    zeros = jnp.zeros((T, HEAD_DIM), BF16)
    nw = nw_ref[...]

    lvl = lvl_ref[...]
    for h in range(H):
        hs = slice(h * HEAD_DIM, (h + 1) * HEAD_DIM)
        r = _dot_nt(qs_scr[1, :, hs], ks_scr[0:2, :, hs].reshape(2 * T, HEAD_DIM))
        a = jnp.where(lvl == 1, r[:, :T], jnp.where(lvl == 2, r[:, T:], 0.0))
        for n in range(len(chunks)):
            a = jnp.where(lvl == n + 3, _dot_nt(qs_scr[n + 2, :, hs], ks_scr[n + 2, :, hs]), a)
        a_scr[h] = a.astype(BF16)
    for h in range(H):
        hs = slice(h * HEAD_DIM, (h + 1) * HEAD_DIM)
        vh = v_ref[0, :, hs]
        s_old = s_scr[h]
        o = _dot(jnp.concatenate([a_scr[h], qs_scr[0, :, hs]], axis=1),
                 jnp.concatenate([vh, s_old.astype(BF16)], axis=0))
        rhs = jnp.concatenate([jnp.concatenate([vh, zeros], axis=1), pick], axis=0)
        upd = _dot_tn(ke_scr[:, hs], rhs)
        s_scr[h] = s_old * upd[:, HEAD_DIM:] + upd[:, :HEAD_DIM]
        o_ref[0, :, hs] = (_rms(o, nw) * g_ref[0, :, hs].astype(F32)).astype(o_ref.dtype)

    @pl.when(c_idx == pl.num_programs(1) - 1)
    def _():
        sout_ref[0] = s_scr[...]


def _gla_range_matrix(T, SUB):
    chunks = []
    c = 2 * SUB
    while c < T:
        chunks.append(c)
        c *= 2
    nb = T // SUB
    t = np.arange(T)
    rows = [((t[None, :] <= t[:, None]) & (t[None, :] // SUB == t[:, None] // SUB))]
    start = np.arange(nb)[:, None] * SUB
    for C in [T] + chunks:
        rows.append((t[None, :] >= (start // C) * C) & (t[None, :] < start))
    for C in [SUB] + chunks + [T]:
        rows.append((t[None, :] >= start) & (t[None, :] < (start // C + 1) * C))
    sel = np.concatenate(rows, axis=0).astype(np.float32)
    row, col = t[:, None], t[None, :]
    lvl = np.where((row // SUB == col // SUB) & (col <= row), 1, 0)
    for j, C in enumerate([SUB] + chunks):
        lvl = np.where(((row // C) % 2 == 1) & (col // C == row // C - 1), 2 + j, lvl)
    return sel, lvl.astype(np.int32), len(chunks)


def _gla_prompt(q, k, lf, v, g, norm_w, mixer, B, L):
    D = q.shape[-1]
    H = D // HEAD_DIM
    T = GLA_BLOCK
    assert T == HEAD_DIM
    sel, lvl, nchunks = _gla_range_matrix(T, GLA_SUB)
    sel3 = jnp.asarray(np.concatenate([sel] * 3, axis=1), BF16)
    nfac = sel.shape[0] - T
    r3 = lambda a: a.reshape(B, L, D)
    blk = lambda: pl.BlockSpec((1, T, D), lambda b, c: (b, c, 0))
    const = lambda a: pl.BlockSpec(a.shape, lambda b, c: (0, 0))
    vm = 2 * (5 * _nbytes((T, D), BF16) + _nbytes((T, D), F32)) + 3 * _nbytes((H, HEAD_DIM, HEAD_DIM), F32)
    vm += (2 * nchunks + 6) * _nbytes((T, D), BF16) + 8 * _nbytes((T, D), F32)
    o, s = pl.pallas_call(
        _gla_prompt_kernel,
        grid=(B, L // T),
        in_specs=[blk(), blk(), blk(), blk(), blk(), _row(mixer)(HEAD_DIM), const(sel3), const(lvl)],
        out_specs=[blk(), pl.BlockSpec((1, H, HEAD_DIM, HEAD_DIM), lambda b, c: (b, 0, 0, 0))],
        out_shape=[jax.ShapeDtypeStruct((B, L, D), BF16),
                   jax.ShapeDtypeStruct((B, H, HEAD_DIM, HEAD_DIM), F32)],
        scratch_shapes=[pltpu.VMEM((H, HEAD_DIM, HEAD_DIM), F32),
                        pltpu.VMEM((nchunks + 2, T, D), BF16),
                        pltpu.VMEM((nchunks + 2, T, D), BF16),
                        pltpu.VMEM((T + GLA_SUB, D), BF16),
                        pltpu.VMEM((nfac, D), F32),
                        pltpu.VMEM((H, T, T), BF16)],
        compiler_params=_cparams(("parallel", "arbitrary"), vm),
        name="hgrn2_recurrence_prompt",
    )(r3(q), r3(k), r3(lf), r3(v), r3(g), norm_w, sel3, jnp.asarray(lvl))
    return o.reshape(B * L, D), s


def _gla_step_kernel(q_ref, k_ref, lf_ref, v_ref, g_ref, s0_ref, nw_ref, o_ref, sout_ref,
                     qs_scr, ks_scr, ke_scr, ds_scr):
    bb, T, D = q_ref.shape
    H = D // HEAD_DIM
    R = bb * T
    pos = lax.broadcasted_iota(jnp.int32, (bb, T, D), 1)
    b = lf_ref[...]
    sh = 1
    while sh < T:
        b = b + jnp.where(pos >= sh, pltpu.roll(b, sh, 1), 0.0)
        sh *= 2
    b_last = b[:, T - 1:T, :]
    k = k_ref[...]
    qs_scr[...] = q_ref[...] * jnp.exp(b)
    ks_scr[...] = k * jnp.exp(-b)
    ke_scr[...] = k * jnp.exp(b_last - b)
    d_hi, d_mid, d_lo = (t.astype(F32) for t in _split3(jnp.exp(b_last)))
    ds_scr[...] = jnp.where(pos == 0, d_hi, jnp.where(pos == 1, d_mid, jnp.where(pos == 2, d_lo, 0.0)))

    row = lax.broadcasted_iota(jnp.int32, (R, R), 0)
    col = lax.broadcasted_iota(jnp.int32, (R, R), 1)
    same_seq_causal = (row // T == col // T) & (col <= row)
    rr = lax.broadcasted_iota(jnp.int32, (T, 2 * HEAD_DIM), 0)
    cc = lax.broadcasted_iota(jnp.int32, (T, 2 * HEAD_DIM), 1)
    pick = jnp.where((rr < 3) & (cc >= HEAD_DIM), 1.0, 0.0)
    zeros = jnp.zeros((T, HEAD_DIM), F32)
    nw = nw_ref[...]

    for h in range(H):
        hs = slice(h * HEAD_DIM, (h + 1) * HEAD_DIM)
        qd = qs_scr[:, :, hs].reshape(R, HEAD_DIM)
        kd = ks_scr[:, :, hs].reshape(R, HEAD_DIM)
        vh = v_ref[:, :, hs].reshape(R, HEAD_DIM)
        a = jnp.where(same_seq_causal, _dot_nt(qd.astype(BF16), kd.astype(BF16)), 0.0)
        o_intra = _dot(a.astype(BF16), vh.astype(BF16))
        outs = []
        for i in range(bb):
            s_old = s0_ref[i, h]
            outs.append(o_intra[i * T:(i + 1) * T]
                        + _dot(qd[i * T:(i + 1) * T].astype(BF16), s_old.astype(BF16)))
            lhs = jnp.concatenate([ke_scr[i, :, hs], ds_scr[i, :, hs]], axis=0).astype(BF16)
            rhs = jnp.concatenate([jnp.concatenate([vh[i * T:(i + 1) * T], zeros], axis=1), pick],
                                  axis=0).astype(BF16)
            upd = _dot_tn(lhs, rhs)
            sout_ref[i, h] = s_old * upd[:, HEAD_DIM:] + upd[:, :HEAD_DIM]
        o = jnp.concatenate(outs, axis=0)
        on = _rms(o, nw) * g_ref[:, :, hs].reshape(R, HEAD_DIM)
        o_ref[:, :, hs] = on.reshape(bb, T, HEAD_DIM).astype(o_ref.dtype)


def _gla_step(q, k, lf, v, g, s0, norm_w, mixer, B, L, bb=8):
    D = q.shape[-1]
    H = D // HEAD_DIM
    assert L == 8 and B % bb == 0
    r3 = lambda a: a.reshape(B, L, D)
    blk = lambda: pl.BlockSpec((bb, L, D), lambda b: (b, 0, 0))
    sblk = lambda: pl.BlockSpec((bb, H, HEAD_DIM, HEAD_DIM), lambda b: (b, 0, 0, 0))
    vm = 2 * (6 * _nbytes((bb, L, D), F32) + 2 * _nbytes((bb, H, HEAD_DIM, HEAD_DIM), F32))
    vm += 12 * _nbytes((bb, L, D), F32)
    o, s = pl.pallas_call(
        _gla_step_kernel,
        grid=(B // bb,),
        in_specs=[blk(), blk(), blk(), blk(), blk(),
                  pl.BlockSpec((None, bb, H, HEAD_DIM, HEAD_DIM), lambda b: (mixer, b, 0, 0, 0)),
                  _row(mixer)(HEAD_DIM)],
        out_specs=[blk(), sblk()],
        out_shape=[jax.ShapeDtypeStruct((B, L, D), F32),
                   jax.ShapeDtypeStruct((B, H, HEAD_DIM, HEAD_DIM), F32)],
        scratch_shapes=[pltpu.VMEM((bb, L, D), F32)] * 4,
        compiler_params=_cparams(("parallel",), vm),
        name="hgrn2_recurrence_step",
    )(r3(q), r3(k), r3(lf), r3(v), r3(g), s0, norm_w)
    return o.reshape(B * L, D), s


def _pool_kernel(*refs, pos0, has_halo):
    if has_halo:
        u_ref, halo_ref, buf_ref, pw_ref, ps_ref, o_ref, nb_ref, ext = refs
    else:
        u_ref, buf_ref, pw_ref, ps_ref, o_ref, nb_ref, ext = refs
    bb, tl, D = u_ref.shape
    G = len(POOL_WINDOWS)
    gc = D // G
    l_idx = pl.program_id(1)
    ext[:, HALO:, :] = u_ref[...]
    if has_halo:
        @pl.when(l_idx == 0)
        def _():
            ext[:, :HALO, :] = buf_ref[...]

        @pl.when(l_idx > 0)
        def _():
            ext[:, :HALO, :] = halo_ref[...]
    else:
        ext[:, :HALO, :] = buf_ref[...]

    pos = pos0 + l_idx * tl + lax.broadcasted_iota(jnp.int32, (tl, gc), 0)
    for gi, w in enumerate(POOL_WINDOWS):
        ls = slice(gi * gc, (gi + 1) * gc)
        cnt = jnp.minimum(w, pos + 1).astype(F32)
        rows = []
        for i in range(bb):
            u = ext[i, HALO:HALO + tl, ls]
            s = u
            for d in range(1, w):
                s = s + ext[i, HALO - d:HALO - d + tl, ls]
            rows.append(s / cnt - u)
        pooled = rows[0] if bb == 1 else jnp.concatenate(rows, axis=0)
        mixed = (_dot(pooled.astype(BF16), pw_ref[gi]) * ps_ref[:, ls]).astype(o_ref.dtype)
        for i in range(bb):
            o_ref[i, :, ls] = mixed[i * tl:(i + 1) * tl]

    @pl.when(l_idx == pl.num_programs(1) - 1)
    def _():
        nb_ref[...] = ext[:, tl:tl + HALO, :]


def _pool(u, buf, pos0, pool_w, pool_scale, mixer, B, L, bb, tl, out_dtype):
    D = u.shape[-1]
    G = len(POOL_WINDOWS)
    gc = D // G
    tl = min(tl, L)
    u3 = u.reshape(B, L, D)
    has_halo = L > tl
    step = tl // HALO
    in_specs = [pl.BlockSpec((bb, tl, D), lambda b, l: (b, l, 0))]
    args = [u3]
    if has_halo:
        in_specs.append(pl.BlockSpec((bb, HALO, D), lambda b, l: (b, jnp.maximum(l * step - 1, 0), 0)))
        args.append(u3)
    in_specs += [pl.BlockSpec((None, bb, HALO, D), lambda b, l: (mixer, b, 0, 0)),
                 pl.BlockSpec((None, G, gc, gc), lambda b, l: (mixer, 0, 0, 0)),
                 _row(mixer)(D)]
    args += [buf, pool_w, pool_scale]
    vm = 2 * (_nbytes((bb, tl, D), F32) + 3 * _nbytes((bb, HALO, D), F32) + _nbytes((G, gc, gc), BF16)
              + _nbytes((bb, tl, D), BF16)) + _nbytes((bb, tl + HALO, D), F32) + 4 * _nbytes((bb, tl, D), F32)
    mixed, newbuf = pl.pallas_call(
        functools.partial(_pool_kernel, pos0=pos0, has_halo=has_halo),
        grid=(B // bb, L // tl),
        in_specs=in_specs,
        out_specs=[pl.BlockSpec((bb, tl, D), lambda b, l: (b, l, 0)),
                   pl.BlockSpec((bb, HALO, D), lambda b, l: (b, 0, 0))],
        out_shape=[jax.ShapeDtypeStruct((B, L, D), out_dtype), jax.ShapeDtypeStruct((B, HALO, D), F32)],
        scratch_shapes=[pltpu.VMEM((bb, tl + HALO, D), F32)],
        compiler_params=_cparams(("parallel", "arbitrary"), vm),
        name="causal_pool",
    )(*args)
    return mixed.reshape(B * L, D), newbuf


def _attn_kernel(q_ref, k_ref, v_ref, o_ref, *, heads):
    bb, tq, D = q_ref.shape
    hd = D // heads
    scale = hd ** -0.5
    for i in range(bb):
        for h in range(heads):
            hs = slice(h * hd, (h + 1) * hd)
            s = _dot_nt(q_ref[i, :, hs].astype(BF16), k_ref[i, :, hs].astype(BF16)) * scale
            p = jnp.exp(s - jnp.max(s, axis=-1, keepdims=True))
            den = jnp.sum(p, axis=-1, keepdims=True)
            o_ref[i, :, hs] = (_dot(p.astype(BF16), v_ref[i, :, hs].astype(BF16)) / den).astype(o_ref.dtype)


def _attn(q, mk, mv, layer, B, L, heads, bb, tq, out_dtype):
    D = q.shape[-1]
    tq = min(tq, L)
    kv_shape = (bb,) + mk.shape[2:]
    kv_blk = (None,) + kv_shape
    vm = 2 * (_nbytes((bb, tq, D), q.dtype) + 2 * _nbytes(kv_shape, mk.dtype) + _nbytes((bb, tq, D), out_dtype))
    vm += 2 * _nbytes(kv_shape, BF16) + 8 * _nbytes((tq, D // heads), F32)
    o = pl.pallas_call(
        functools.partial(_attn_kernel, heads=heads),
        grid=(B // bb, L // tq),
        in_specs=[pl.BlockSpec((bb, tq, D), lambda b, l: (b, l, 0)),
                  pl.BlockSpec(kv_blk, lambda b, l: (layer, b, 0, 0)),
                  pl.BlockSpec(kv_blk, lambda b, l: (layer, b, 0, 0))],
        out_specs=pl.BlockSpec((bb, tq, D), lambda b, l: (b, l, 0)),
        out_shape=jax.ShapeDtypeStruct((B, L, D), out_dtype),
        compiler_params=_cparams(("parallel", "parallel"), vm),
        name="memory_cross_attention",
    )(q.reshape(B, L, D), mk, mv)
    return o.reshape(B * L, D)


def _attn_decode_kernel(q_ref, k_ref, v_ref, o_ref, *, heads):
    bb, tq, D = q_ref.shape
    n_mem, nu = k_ref.shape[1], k_ref.shape[2]
    hd = D // heads
    per = SUBL // heads
    W = n_mem * SUBL
    scale = hd ** -0.5
    res = lax.broadcasted_iota(jnp.int32, (tq, W), 1) % SUBL
    tile = lambda r: jnp.concatenate([r] * (W // LANES), axis=1)

    def feat(j):
        c, h = divmod(j, heads)
        return slice(h * hd + c * LANES, h * hd + (c + 1) * LANES)

    def same_row_reduce(x, op):
        r = x[:, :LANES]
        for j in range(1, W // LANES):
            r = op(r, x[:, j * LANES:(j + 1) * LANES])
        sh = SUBL
        while sh < LANES:
            r = op(r, pltpu.roll(r, sh, 1))
            sh *= 2
        return r

    for i in range(bb):
        w = None
        for u in range(nu):
            slab = k_ref[i, :, u].reshape(W, LANES).astype(BF16)
            qm = jnp.concatenate([q_ref[i, :, feat(u * SUBL + s)] for s in range(SUBL)], axis=0)
            p = _dot_nt(qm.astype(BF16), slab)
            z = p[:tq]
            for s in range(1, SUBL):
                z = jnp.where(res == s, p[s * tq:(s + 1) * tq], z)
            w = z if w is None else w + z
        sc = w
        for c in range(1, per):
            sc = sc + pltpu.roll(w, W - c * heads, 1)
        sc = sc * scale
        e = jnp.exp(sc - tile(same_row_reduce(sc, jnp.maximum)))
        pn = e / tile(same_row_reduce(e, jnp.add))
        blocks = []
        for s in range(SUBL):
            c_lo = s // heads
            src = pn if c_lo == 0 else pltpu.roll(pn, c_lo * heads, 1)
            blocks.append(jnp.where(res == s, src, 0.0))
        pm = jnp.concatenate(blocks, axis=0).astype(BF16)
        for u in range(nu):
            o = _dot(pm, v_ref[i, :, u].reshape(W, LANES).astype(BF16))
            for s in range(SUBL):
                o_ref[i, :, feat(u * SUBL + s)] = o[s * tq:(s + 1) * tq].astype(o_ref.dtype)


def _attn_decode(q, ck, cv, layer, B, L, heads, bb, out_dtype):
    D = q.shape[-1]
    assert L == SUBL and SUBL % heads == 0 and D % (SUBL * LANES) == 0
    kv_shape = (bb,) + ck.shape[2:]
    vm = 2 * (2 * _nbytes((bb, L, D), F32) + 2 * _nbytes(kv_shape, ck.dtype))
    vm += 4 * _nbytes(kv_shape[1:], BF16) + 6 * _nbytes((SUBL * L, ck.shape[2] * SUBL), F32)
    o = pl.pallas_call(
        functools.partial(_attn_decode_kernel, heads=heads),
        grid=(B // bb,),
        in_specs=[pl.BlockSpec((bb, L, D), lambda b: (b, 0, 0)),
                  pl.BlockSpec((None,) + kv_shape, lambda b: (layer, b, 0, 0, 0, 0)),
                  pl.BlockSpec((None,) + kv_shape, lambda b: (layer, b, 0, 0, 0, 0))],
        out_specs=pl.BlockSpec((bb, L, D), lambda b: (b, 0, 0)),
        out_shape=jax.ShapeDtypeStruct((B, L, D), out_dtype),
        compiler_params=_cparams(("parallel",), vm),
        name="memory_cross_attention_decode",
    )(q.reshape(B, L, D), ck, cv)
    return o.reshape(B * L, D)


def _mem_proj_kernel(x_ref, nw_ref, w_ref, flat_ref, bf_ref, *, heads):
    tm, D = x_ref.shape
    R = D // LANES
    nchunk = R // heads
    acc = _dot(_rms(x_ref[...], nw_ref[...]).astype(BF16), w_ref[...])
    bf_ref[...] = acc.astype(bf_ref.dtype)
    for h in range(heads):
        for c in range(nchunk):
            j = h * nchunk + c
            flat_ref[pl.ds(c * heads + h, tm, stride=R), :] = acc[:, j * LANES:(j + 1) * LANES]


def _mem_proj(mem, norm_w, w, half, heads, tm=256):
    M, D = mem.shape
    depth = w.shape[0]
    R = D // LANES
    tm = min(tm, M)
    nt = M // tm
    vm = 2 * (_nbytes((tm, D), F32) + _nbytes((D, D), BF16) + _nbytes((tm, D), F32) + _nbytes((tm, D), BF16))
    vm += 2 * _nbytes((tm, D), F32)
    return pl.pallas_call(
        functools.partial(_mem_proj_kernel, heads=heads),
        grid=(depth, nt),
        in_specs=[pl.BlockSpec((tm, D), lambda l, i: (i, 0)),
                  pl.BlockSpec((None, 1, D), lambda l, i: (l, 0, 0)),
                  pl.BlockSpec((None, D, D), lambda l, i: (l, 0, half))],
        out_specs=[pl.BlockSpec((tm * R, LANES), lambda l, i: (l * nt + i, 0)),
                   pl.BlockSpec((None, tm, D), lambda l, i: (l, i, 0))],
        out_shape=[jax.ShapeDtypeStruct((depth * M * R, LANES), F32),
                   jax.ShapeDtypeStruct((depth, M, D), BF16)],
        compiler_params=_cparams(("parallel", "parallel"), vm),
        name="memory_kv_projection",
    )(mem, norm_w, w)


def _trunk(x, B, L, pos0, S_in, buf_in, mk, mv, p, prompt):
    depth = p["w_xq"].shape[0]
    heads = p["mem_heads"]
    hn = _rmsnorm(x, p["norm_mix_pre"], 0)
    S_out, buf_out = [], []
    act = BF16 if prompt else F32
    for l in range(depth):
        j = l // 2
        if l % 2 == 0:
            q, k, lf, v, g = _proj_a(hn, p["w_in_a"], j, p["hg_lb_logits"], l, act)
            if prompt:
                o, S = _gla_prompt(q, k, lf, v, g, p["hg_norm"], j, B, L)
            else:
                o, S = _gla_step(q, k, lf, v, g, S_in, p["hg_norm"], j, B, L)
            S_out.append(S)
            x, hn = _mm_res(o, p["w_out_a"], j, x, p["norm_mix_post"], p["norm_x_pre"], l)
        else:
            (u,) = _matmul(hn, p["w_in_b"], j, [F32])
            bb, tl = (1, 512) if prompt else (16, L)
            mixed, nb = _pool(u, buf_in, pos0, p["pool_w"], p["pool_scale"], j, B, L, bb, tl, act)
            buf_out.append(nb[:, 1:, :])
            x, hn = _mm_res(mixed, p["w_out_b"], j, x, p["norm_mix_post"], p["norm_x_pre"], l)
        (qx,) = _matmul(hn, p["w_xq"], l, [act])
        if prompt:
            a = _attn(qx, mk, mv, l, B, L, heads, 1, 512, act)
        else:
            a = _attn_decode(qx, mk, mv, l, B, L, heads, 4, act)
        x, hn = _mm_res(a, p["w_xo"], l, x, p["norm_x_post"], p["norm_mlp_pre"], l)
        x, hn = _mlp(hn, p["w_up"], p["w_down"], x, p["norm_mlp_post"], p["norm_mix_pre"], l)
    return x, jnp.stack(S_out), jnp.stack(buf_out)


def kernel(x_prompt, x_sample, state_hgrn, state_pool, cache_mem_k, cache_mem_v, mem_prompt, w_in_a, hg_lb_logits, hg_norm, w_out_a, w_in_b, pool_w, pool_scale, w_out_b, norm_mem, w_xq, w_xkv, w_xo, norm_mix_pre, norm_mix_post, norm_x_pre, norm_x_post, norm_mlp_pre, norm_mlp_post, w_up, w_down):
    B, L, D = x_prompt.shape
    Bs, Ls, _ = x_sample.shape
    depth = w_xq.shape[0]
    n_mem, heads = cache_mem_k.shape[2], cache_mem_k.shape[3]
    assert L % GLA_BLOCK == 0 and Ls <= GLA_SUB and D % (HEAD_DIM * len(POOL_WINDOWS)) == 0
    assert state_pool.shape[2] == HALO - 1
    bf = lambda a: a.astype(BF16)
    rows = lambda a: a.reshape(a.shape[0], 1, a.shape[1])
    p = dict(w_in_a=bf(w_in_a), hg_lb_logits=hg_lb_logits, hg_norm=rows(hg_norm), w_out_a=bf(w_out_a),
             w_in_b=bf(w_in_b), pool_w=bf(pool_w), pool_scale=rows(pool_scale), w_out_b=bf(w_out_b),
             w_xq=bf(w_xq), w_xo=bf(w_xo), norm_mix_pre=rows(norm_mix_pre), norm_mix_post=rows(norm_mix_post),
             norm_x_pre=rows(norm_x_pre), norm_x_post=rows(norm_x_post), norm_mlp_pre=rows(norm_mlp_pre),
             norm_mlp_post=rows(norm_mlp_post), w_up=bf(w_up), w_down=bf(w_down), mem_heads=heads)
    R = D // LANES
    nchunk = R // heads

    def cache_view(flat, nb):
        c = flat.reshape(depth, nb, n_mem, nchunk, heads, LANES)
        return jnp.swapaxes(c, 3, 4).reshape(depth, nb, n_mem, heads, D // heads)

    def flat_view(cache):
        nb = cache.shape[1]
        c = cache.reshape(depth, nb, n_mem, heads, nchunk, LANES)
        return jnp.swapaxes(c, 3, 4).reshape(depth, nb, n_mem, R // SUBL, SUBL, LANES)

    mem2d = mem_prompt.reshape(B * n_mem, D)
    w_xkv_bf = bf(w_xkv)
    k_flat, k_bf = _mem_proj(mem2d, rows(norm_mem), w_xkv_bf, 0, heads)
    v_flat, v_bf = _mem_proj(mem2d, rows(norm_mem), w_xkv_bf, 1, heads)
    cache_mem_k_prompt = cache_view(k_flat, B)
    cache_mem_v_prompt = cache_view(v_flat, B)
    mk_bf = k_bf.reshape(depth, B, n_mem, D)
    mv_bf = v_bf.reshape(depth, B, n_mem, D)

    n_b = state_pool.shape[0]
    buf0 = jnp.zeros((n_b, B, HALO, D), F32)
    y_prompt, state_hgrn_prompt, state_pool_prompt = _trunk(
        x_prompt.reshape(B * L, D), B, L, 0, None, buf0, mk_bf, mv_bf, p, True)

    buf_s = jnp.pad(state_pool, ((0, 0), (0, 0), (1, 0), (0, 0)))
    ck = flat_view(cache_mem_k)
    cv = flat_view(cache_mem_v)
    y_sample, state_hgrn_sample, state_pool_sample = _trunk(
        x_sample.reshape(Bs * Ls, D), Bs, Ls, PAST_LEN, state_hgrn, buf_s, ck, cv, p, False)

    return (y_prompt.reshape(B, L, D), y_sample.reshape(Bs, Ls, D), state_hgrn_prompt, state_pool_prompt,
            cache_mem_k_prompt, cache_mem_v_prompt, state_hgrn_sample, state_pool_sample)
```

```python
import functools

import numpy as np

import jax
import jax.numpy as jnp
from jax import lax
from jax.experimental import pallas as pl
from jax.experimental.pallas import tpu as pltpu

F32 = jnp.float32
BF16 = jnp.bfloat16
EPS = 1e-6
PAST_LEN = 16384
POOL_WINDOWS = (2, 4, 8, 16)
LANES = 128
SUBL = 8
HEAD_DIM = 128
GLA_BLOCK = 128
GLA_SUB = 16
HALO = 16
MM_RES_SPLIT = 4
SIDE_EVERY = 4

V7X_VMEM_BYTES = 64 * 1024 * 1024
VMEM_CAP = V7X_VMEM_BYTES - 6 * 1024 * 1024


def _cparams(sem, vmem_bytes):
    return pltpu.CompilerParams(dimension_semantics=sem,
                                vmem_limit_bytes=int(min(max(vmem_bytes, 16 * 1024 * 1024), VMEM_CAP)))


def _nbytes(shape, dtype):
    n = 1
    for s in shape:
        n *= s
    return n * jnp.dtype(dtype).itemsize


def _rms(v, w):
    ms = jnp.mean(v * v, axis=-1, keepdims=True)
    return v * lax.rsqrt(ms + EPS) * w


def _sigmoid(x):
    return 1.0 / (1.0 + jnp.exp(-x))


def _dot(a, b):
    return jnp.dot(a, b, preferred_element_type=F32)


def _dot_nt(a, b):
    return lax.dot_general(a, b, (((1,), (1,)), ((), ())), preferred_element_type=F32)


def _dot_tn(a, b, precision=None):
    return lax.dot_general(a, b, (((0,), (0,)), ((), ())), preferred_element_type=F32,
                           precision=precision)


def _rmsnorm_kernel(x_ref, w_ref, o_ref):
    o_ref[...] = _rms(x_ref[...], w_ref[...]).astype(o_ref.dtype)


def _row(layer):
    return lambda width: pl.BlockSpec((None, 1, width), lambda *_: (layer, 0, 0))


def _rmsnorm(x, w, layer, tm=512):
    M, D = x.shape
    tm = min(tm, M)
    return pl.pallas_call(
        _rmsnorm_kernel,
        grid=(M // tm,),
        in_specs=[pl.BlockSpec((tm, D), lambda i: (i, 0)), _row(layer)(D)],
        out_specs=pl.BlockSpec((tm, D), lambda i: (i, 0)),
        out_shape=jax.ShapeDtypeStruct((M, D), BF16),
        compiler_params=_cparams(("parallel",), 4 * _nbytes((tm, D), F32)),
        name="rmsnorm",
    )(x, w)


def _mm_kernel(a_ref, w_ref, *o_refs):
    acc = _dot(a_ref[...], w_ref[...])
    for o_ref in o_refs:
        o_ref[...] = acc.astype(o_ref.dtype)


def _matmul(a, w, layer, out_dtypes, tm=1024, tn=1024):
    M, K = a.shape
    N = w.shape[2]
    tm, tn = min(tm, M), min(tn, N)
    vm = 2 * (_nbytes((tm, K), BF16) + _nbytes((K, tn), BF16)) + _nbytes((tm, tn), F32)
    vm += sum(2 * _nbytes((tm, tn), d) for d in out_dtypes)
    outs = pl.pallas_call(
        _mm_kernel,
        grid=(M // tm, N // tn),
        in_specs=[pl.BlockSpec((tm, K), lambda i, j: (i, 0)),
                  pl.BlockSpec((None, K, tn), lambda i, j: (layer, 0, j))],
        out_specs=[pl.BlockSpec((tm, tn), lambda i, j: (i, j)) for _ in out_dtypes],
        out_shape=[jax.ShapeDtypeStruct((M, N), d) for d in out_dtypes],
        compiler_params=_cparams(("parallel", "parallel"), vm),
        name="matmul",
    )(a, w)
    return outs


def _proj_a_kernel(h_ref, wq_ref, wf_ref, wi_ref, wg_ref, lbl_ref, q_ref, k_ref, lf_ref, v_ref, g_ref,
                   *, layer):
    h = h_ref[...]
    aq = _dot(h, wq_ref[...])
    q_ref[...] = (aq * _sigmoid(aq)).astype(q_ref.dtype)
    lg = lbl_ref[...]
    e = jnp.exp(lg - jnp.max(lg, axis=0, keepdims=True))
    lb = jnp.sum(e[:layer + 1], axis=0, keepdims=True) / jnp.sum(e, axis=0, keepdims=True)
    f = lb + (1.0 - lb) * _sigmoid(_dot(h, wf_ref[...]))
    k_ref[...] = (1.0 - f).astype(k_ref.dtype)
    lf_ref[...] = jnp.log(f)
    v_ref[...] = _dot(h, wi_ref[...]).astype(v_ref.dtype)
    ag = _dot(h, wg_ref[...])
    g_ref[...] = (ag * _sigmoid(ag)).astype(g_ref.dtype)


def _proj_a(h, w, mixer, lb_logits, layer, act_dtype, tm=1024, tn=512):
    M, D = h.shape
    tm = min(tm, M)
    nj = D // tn
    nl = lb_logits.shape[0]
    wspec = lambda s: pl.BlockSpec((None, D, tn), lambda i, j: (mixer, 0, j + s * nj))
    ospec = pl.BlockSpec((tm, tn), lambda i, j: (i, j))
    vm = 2 * (_nbytes((tm, D), BF16) + 4 * _nbytes((D, tn), BF16) + 5 * _nbytes((tm, tn), F32))
    vm += 4 * _nbytes((tm, tn), F32)
    return pl.pallas_call(
        functools.partial(_proj_a_kernel, layer=layer),
        grid=(M // tm, nj),
        in_specs=[pl.BlockSpec((tm, D), lambda i, j: (i, 0)), wspec(0), wspec(1), wspec(2), wspec(3),
                  pl.BlockSpec((nl, tn), lambda i, j: (0, j))],
        out_specs=[ospec] * 5,
        out_shape=[jax.ShapeDtypeStruct((M, D), act_dtype), jax.ShapeDtypeStruct((M, D), act_dtype),
                   jax.ShapeDtypeStruct((M, D), F32), jax.ShapeDtypeStruct((M, D), act_dtype),
                   jax.ShapeDtypeStruct((M, D), act_dtype)],
        compiler_params=_cparams(("parallel", "parallel"), vm),
        name="hgrn2_proj",
    )(h, w, w, w, w, lb_logits)


def _mm_res_kernel(a_ref, w_ref, x_ref, pw_ref, nw_ref, ox_ref, oh_ref):
    tm = a_ref.shape[0]
    nsplit = MM_RES_SPLIT if tm % (8 * MM_RES_SPLIT) == 0 and tm // MM_RES_SPLIT >= 128 else 1
    for p in range(nsplit):
        rows = slice(p * tm // nsplit, (p + 1) * tm // nsplit)
        m = _dot(a_ref[rows, :].astype(BF16), w_ref[...])
        y = x_ref[rows, :] + _rms(m, pw_ref[...])
        ox_ref[rows, :] = y
        oh_ref[rows, :] = _rms(y, nw_ref[...]).astype(oh_ref.dtype)


def _mm_res(a, w, wl, x, post_w, next_w, layer, tm=512):
    M, K = a.shape
    D = w.shape[2]
    tm = min(tm, M)
    vm = 2 * (_nbytes((tm, K), a.dtype) + _nbytes((K, D), BF16) + 2 * _nbytes((tm, D), F32)
              + _nbytes((tm, D), BF16)) + 2 * _nbytes((tm, D), F32)
    return pl.pallas_call(
        _mm_res_kernel,
        grid=(M // tm,),
        in_specs=[pl.BlockSpec((tm, K), lambda i: (i, 0)),
                  pl.BlockSpec((None, K, D), lambda i: (wl, 0, 0)),
                  pl.BlockSpec((tm, D), lambda i: (i, 0)), _row(layer)(D), _row(layer)(D)],
        out_specs=[pl.BlockSpec((tm, D), lambda i: (i, 0)), pl.BlockSpec((tm, D), lambda i: (i, 0))],
        out_shape=[jax.ShapeDtypeStruct((M, D), F32), jax.ShapeDtypeStruct((M, D), BF16)],
        compiler_params=_cparams(("parallel",), vm),
        name="matmul_norm_residual",
    )(a, w, x, post_w, next_w)


def _mlp_kernel(h_ref, wu_ref, wd_ref, x_ref, pw_ref, nw_ref, ox_ref, *rest, emit_next):
    acc_ref = rest[-1]
    f = pl.program_id(1)

    @pl.when(f == 0)
    def _():
        acc_ref[...] = jnp.zeros_like(acc_ref)

    u = jnp.maximum(_dot(h_ref[...], wu_ref[...]), 0.0)
    acc_ref[...] += _dot((u * u).astype(BF16), wd_ref[...])

    @pl.when(f == pl.num_programs(1) - 1)
    def _():
        y = x_ref[...] + _rms(acc_ref[...], pw_ref[...])
        ox_ref[...] = y
        if emit_next:
            rest[0][...] = _rms(y, nw_ref[...]).astype(BF16)


def _mlp(h, w_up, w_down, x, post_w, next_w, layer, tm=512, tf=1024):
    M, D = h.shape
    FF = w_up.shape[2]
    tm = min(tm, M)
    emit_next = layer + 1 < next_w.shape[0]
    tile = lambda: pl.BlockSpec((tm, D), lambda i, f: (i, 0))
    out_specs = [tile()] + ([tile()] if emit_next else [])
    out_shape = [jax.ShapeDtypeStruct((M, D), F32)] + ([jax.ShapeDtypeStruct((M, D), BF16)] if emit_next else [])
    vm = 2 * (2 * _nbytes((tm, D), BF16) + 2 * _nbytes((D, tf), BF16) + 2 * _nbytes((tm, D), F32))
    vm += 2 * _nbytes((tm, D), F32) + 2 * _nbytes((tm, tf), F32)
    outs = pl.pallas_call(
        functools.partial(_mlp_kernel, emit_next=emit_next),
        grid=(M // tm, FF // tf),
        in_specs=[tile(),
                  pl.BlockSpec((None, D, tf), lambda i, f: (layer, 0, f)),
                  pl.BlockSpec((None, tf, D), lambda i, f: (layer, f, 0)),
                  tile(), _row(layer)(D), _row(layer + 1 if emit_next else layer)(D)],
        out_specs=out_specs,
        out_shape=out_shape,
        scratch_shapes=[pltpu.VMEM((tm, D), F32)],
        compiler_params=_cparams(("parallel", "arbitrary"), vm),
        name="relu2_mlp",
    )(h, w_up, w_down, x, post_w, next_w)
    return (outs[0], outs[1]) if emit_next else (outs[0], None)


def _split3(x):
    hi = x.astype(BF16)
    r1 = x - hi.astype(F32)
    mid = r1.astype(BF16)
    lo = (r1 - mid.astype(F32)).astype(BF16)
    return hi, mid, lo


def _decay_rows(d, rows):
    hi, mid, lo = _split3(d)
    r = lax.broadcasted_iota(jnp.int32, (rows, d.shape[-1]), 0)
    return jnp.where(r == 0, hi.astype(F32), jnp.where(r == 1, mid.astype(F32),
                                                       jnp.where(r == 2, lo.astype(F32), 0.0)))


def _gla_prompt_kernel(*refs, decode_heads=None):
    if decode_heads is None:
        (q_ref, k_ref, lf_ref, v_ref, g_ref, nw_ref, sel_ref, lvl_ref, o_ref, sout_ref,
         s_scr, qs_scr, ks_scr, ke_scr, fac_scr, a_scr) = refs
    else:
        (q_ref, k_ref, lf_ref, v_ref, g_ref, nw_ref, sel_ref, lvl_ref, dq_ref, dk_ref, dv_ref,
         o_ref, sout_ref, do_ref, s_scr, qs_scr, ks_scr, ke_scr, fac_scr, a_scr) = refs
    T = q_ref.shape[1]
    D = q_ref.shape[2]
    H = D // HEAD_DIM
    SUB = GLA_SUB
    nb = T // SUB
    c_idx = pl.program_id(1)

    @pl.when(c_idx == 0)
    def _():
        s_scr[...] = jnp.zeros_like(s_scr)

    side = iter(()) if decode_heads is None else _attn_decode_pieces(dq_ref, dk_ref, dv_ref, do_ref,
                                                                     heads=decode_heads)
    stage = [0]

    def side_step():
        stage[0] += 1
        if stage[0] % SIDE_EVERY == 0:
            next(side, None)

    chunks = []
    c = 2 * SUB
    while c < T:
        chunks.append(c)
        c *= 2
    nf = 2 * len(chunks) + 3
    sums = _dot(sel_ref[...], jnp.concatenate(_split3(lf_ref[0]), axis=0))
    fac_scr[...] = jnp.exp(sums[T:])
    nq = len(chunks) + 1
    for i in range(nb):
        rows = slice(i * SUB, (i + 1) * SUB)
        r = sums[rows]
        q16 = q_ref[0, rows, :].astype(F32) * jnp.exp(r)
        kd = k_ref[0, rows, :].astype(F32) * jnp.exp(-r)
        f_row = lambda n: fac_scr[n * nb + i:n * nb + i + 1, :]
        qs_scr[0, rows, :] = (q16 * f_row(0)).astype(BF16)
        qs_scr[1, rows, :] = q16.astype(BF16)
        ks_scr[0, rows, :] = kd.astype(BF16)
        ks_scr[1, rows, :] = (kd * f_row(nq)).astype(BF16)
        for n in range(len(chunks)):
            qs_scr[n + 2, rows, :] = (q16 * f_row(n + 1)).astype(BF16)
            ks_scr[n + 2, rows, :] = (kd * f_row(nq + n + 1)).astype(BF16)
        ke_scr[rows, :] = (kd * f_row(nf - 1)).astype(BF16)
        side_step()
    ke_scr[T:, :] = _decay_rows(fac_scr[(nf - 1) * nb:(nf - 1) * nb + 1, :], SUB).astype(BF16)

    rr = lax.broadcasted_iota(jnp.int32, (SUB, 2 * HEAD_DIM), 0)
    cc = lax.broadcasted_iota(jnp.int32, (SUB, 2 * HEAD_DIM), 1)
    pick = jnp.where((rr < 3) & (cc >= HEAD_DIM), 1.0, 0.0).astype(BF16)
    zeros = jnp.zeros((T, HEAD_DIM), BF16)
    nw = nw_ref[...]

    lvl = lvl_ref[...]
    for h in range(H):
        hs = slice(h * HEAD_DIM, (h + 1) * HEAD_DIM)
        r = _dot_nt(qs_scr[1, :, hs], ks_scr[0:2, :, hs].reshape(2 * T, HEAD_DIM))
        a = jnp.where(lvl == 1, r[:, :T], jnp.where(lvl == 2, r[:, T:], 0.0))
        for n in range(len(chunks)):
            a = jnp.where(lvl == n + 3, _dot_nt(qs_scr[n + 2, :, hs], ks_scr[n + 2, :, hs]), a)
        a_scr[h] = a.astype(BF16)
        side_step()
    for h in range(H):
        hs = slice(h * HEAD_DIM, (h + 1) * HEAD_DIM)
        vh = v_ref[0, :, hs]
        s_old = s_scr[h]
        o = _dot(jnp.concatenate([a_scr[h], qs_scr[0, :, hs]], axis=1),
                 jnp.concatenate([vh, s_old.astype(BF16)], axis=0))
        rhs = jnp.concatenate([jnp.concatenate([vh, zeros], axis=1), pick], axis=0)
        upd = _dot_tn(ke_scr[:, hs], rhs)
        s_scr[h] = s_old * upd[:, HEAD_DIM:] + upd[:, :HEAD_DIM]
        o_ref[0, :, hs] = (_rms(o, nw) * g_ref[0, :, hs].astype(F32)).astype(o_ref.dtype)
        side_step()
    for _ in side:
        pass

    @pl.when(c_idx == pl.num_programs(1) - 1)
    def _():
        sout_ref[0] = s_scr[...]


def _gla_range_matrix(T, SUB):
    chunks = []
    c = 2 * SUB
    while c < T:
        chunks.append(c)
        c *= 2
    nb = T // SUB
    t = np.arange(T)
    rows = [((t[None, :] <= t[:, None]) & (t[None, :] // SUB == t[:, None] // SUB))]
    start = np.arange(nb)[:, None] * SUB
    for C in [T] + chunks:
        rows.append((t[None, :] >= (start // C) * C) & (t[None, :] < start))
    for C in [SUB] + chunks + [T]:
        rows.append((t[None, :] >= start) & (t[None, :] < (start // C + 1) * C))
    sel = np.concatenate(rows, axis=0).astype(np.float32)
    row, col = t[:, None], t[None, :]
    lvl = np.where((row // SUB == col // SUB) & (col <= row), 1, 0)
    for j, C in enumerate([SUB] + chunks):
        lvl = np.where(((row // C) % 2 == 1) & (col // C == row // C - 1), 2 + j, lvl)
    return sel, lvl.astype(np.int32), len(chunks)


def _gla_prompt(q, k, lf, v, g, norm_w, mixer, B, L, decode=None):
    D = q.shape[-1]
    H = D // HEAD_DIM
    T = GLA_BLOCK
    assert T == HEAD_DIM
    nc = L // T
    sel, lvl, nchunks = _gla_range_matrix(T, GLA_SUB)
    sel3 = jnp.asarray(np.concatenate([sel] * 3, axis=1), BF16)
    nfac = sel.shape[0] - T
    r3 = lambda a: a.reshape(B, L, D)
    blk = lambda: pl.BlockSpec((1, T, D), lambda b, c: (b, c, 0))
    const = lambda a: pl.BlockSpec(a.shape, lambda b, c: (0, 0))
    vm = 2 * (5 * _nbytes((T, D), BF16) + _nbytes((T, D), F32)) + 3 * _nbytes((H, HEAD_DIM, HEAD_DIM), F32)
    vm += (2 * nchunks + 6) * _nbytes((T, D), BF16) + 8 * _nbytes((T, D), F32)
    in_specs = [blk(), blk(), blk(), blk(), blk(), _row(mixer)(HEAD_DIM), const(sel3), const(lvl)]
    args = [r3(q), r3(k), r3(lf), r3(v), r3(g), norm_w, sel3, jnp.asarray(lvl)]
    out_specs = [blk(), pl.BlockSpec((1, H, HEAD_DIM, HEAD_DIM), lambda b, c: (b, 0, 0, 0))]
    out_shape = [jax.ShapeDtypeStruct((B, L, D), BF16), jax.ShapeDtypeStruct((B, H, HEAD_DIM, HEAD_DIM), F32)]
    heads = None
    if decode is not None:
        dq, ck, cv, layer, Bd, Ld, heads, d_dtype = decode
        assert Ld == SUBL and SUBL % heads == 0 and Bd % (B * nc) == 0
        bb = Bd // (B * nc)
        kv_shape = (bb,) + ck.shape[2:]
        kv_spec = lambda: pl.BlockSpec((None,) + kv_shape, lambda b, c: (layer, b * nc + c, 0, 0, 0, 0))
        dblk = lambda: pl.BlockSpec((bb, Ld, D), lambda b, c: (b * nc + c, 0, 0))
        in_specs += [dblk(), kv_spec(), kv_spec()]
        args += [dq.reshape(Bd, Ld, D), ck, cv]
        out_specs.append(dblk())
        out_shape.append(jax.ShapeDtypeStruct((Bd, Ld, D), d_dtype))
        vm += 4 * _nbytes((bb, Ld, D), F32) + 4 * _nbytes(kv_shape, ck.dtype)
        vm += 4 * _nbytes(kv_shape[1:], BF16) + 6 * _nbytes((SUBL * Ld, ck.shape[2] * SUBL), F32)
    outs = pl.pallas_call(
        functools.partial(_gla_prompt_kernel, decode_heads=heads),
        grid=(B, nc),
        in_specs=in_specs,
        out_specs=out_specs,
        out_shape=out_shape,
        scratch_shapes=[pltpu.VMEM((H, HEAD_DIM, HEAD_DIM), F32),
                        pltpu.VMEM((nchunks + 2, T, D), BF16),
                        pltpu.VMEM((nchunks + 2, T, D), BF16),
                        pltpu.VMEM((T + GLA_SUB, D), BF16),
                        pltpu.VMEM((nfac, D), F32),
                        pltpu.VMEM((H, T, T), BF16)],
        compiler_params=_cparams(("parallel", "arbitrary"), vm),
        name="hgrn2_recurrence_prompt",
    )(*args)
    a_dec = outs[2].reshape(Bd * Ld, D) if decode is not None else None
    return outs[0].reshape(B * L, D), outs[1], a_dec


def _gla_step_kernel(q_ref, k_ref, lf_ref, v_ref, g_ref, s0_ref, nw_ref, o_ref, sout_ref,
                     qs_scr, ks_scr, ke_scr, ds_scr):
    bb, T, D = q_ref.shape
    H = D // HEAD_DIM
    R = bb * T
    pos = lax.broadcasted_iota(jnp.int32, (bb, T, D), 1)
    b = lf_ref[...]
    sh = 1
    while sh < T:
        b = b + jnp.where(pos >= sh, pltpu.roll(b, sh, 1), 0.0)
        sh *= 2
    b_last = b[:, T - 1:T, :]
    k = k_ref[...]
    qs_scr[...] = q_ref[...] * jnp.exp(b)
    ks_scr[...] = k * jnp.exp(-b)
    ke_scr[...] = k * jnp.exp(b_last - b)
    d_hi, d_mid, d_lo = (t.astype(F32) for t in _split3(jnp.exp(b_last)))
    ds_scr[...] = jnp.where(pos == 0, d_hi, jnp.where(pos == 1, d_mid, jnp.where(pos == 2, d_lo, 0.0)))

    row = lax.broadcasted_iota(jnp.int32, (R, R), 0)
    col = lax.broadcasted_iota(jnp.int32, (R, R), 1)
    same_seq_causal = (row // T == col // T) & (col <= row)
    rr = lax.broadcasted_iota(jnp.int32, (T, 2 * HEAD_DIM), 0)
    cc = lax.broadcasted_iota(jnp.int32, (T, 2 * HEAD_DIM), 1)
    pick = jnp.where((rr < 3) & (cc >= HEAD_DIM), 1.0, 0.0)
    zeros = jnp.zeros((T, HEAD_DIM), F32)
    nw = nw_ref[...]

    for h in range(H):
        hs = slice(h * HEAD_DIM, (h + 1) * HEAD_DIM)
        qd = qs_scr[:, :, hs].reshape(R, HEAD_DIM)
        kd = ks_scr[:, :, hs].reshape(R, HEAD_DIM)
        vh = v_ref[:, :, hs].reshape(R, HEAD_DIM)
        a = jnp.where(same_seq_causal, _dot_nt(qd.astype(BF16), kd.astype(BF16)), 0.0)
        o_intra = _dot(a.astype(BF16), vh.astype(BF16))
        outs = []
        for i in range(bb):
            s_old = s0_ref[i, h]
            outs.append(o_intra[i * T:(i + 1) * T]
                        + _dot(qd[i * T:(i + 1) * T].astype(BF16), s_old.astype(BF16)))
            lhs = jnp.concatenate([ke_scr[i, :, hs], ds_scr[i, :, hs]], axis=0).astype(BF16)
            rhs = jnp.concatenate([jnp.concatenate([vh[i * T:(i + 1) * T], zeros], axis=1), pick],
                                  axis=0).astype(BF16)
            upd = _dot_tn(lhs, rhs)
            sout_ref[i, h] = s_old * upd[:, HEAD_DIM:] + upd[:, :HEAD_DIM]
        o = jnp.concatenate(outs, axis=0)
        on = _rms(o, nw) * g_ref[:, :, hs].reshape(R, HEAD_DIM)
        o_ref[:, :, hs] = on.reshape(bb, T, HEAD_DIM).astype(o_ref.dtype)


def _gla_step(q, k, lf, v, g, s0, norm_w, mixer, B, L, bb=8):
    D = q.shape[-1]
    H = D // HEAD_DIM
    assert L == 8 and B % bb == 0
    r3 = lambda a: a.reshape(B, L, D)
    blk = lambda: pl.BlockSpec((bb, L, D), lambda b: (b, 0, 0))
    sblk = lambda: pl.BlockSpec((bb, H, HEAD_DIM, HEAD_DIM), lambda b: (b, 0, 0, 0))
    vm = 2 * (6 * _nbytes((bb, L, D), F32) + 2 * _nbytes((bb, H, HEAD_DIM, HEAD_DIM), F32))
    vm += 12 * _nbytes((bb, L, D), F32)
    o, s = pl.pallas_call(
        _gla_step_kernel,
        grid=(B // bb,),
        in_specs=[blk(), blk(), blk(), blk(), blk(),
                  pl.BlockSpec((None, bb, H, HEAD_DIM, HEAD_DIM), lambda b: (mixer, b, 0, 0, 0)),
                  _row(mixer)(HEAD_DIM)],
        out_specs=[blk(), sblk()],
        out_shape=[jax.ShapeDtypeStruct((B, L, D), F32),
                   jax.ShapeDtypeStruct((B, H, HEAD_DIM, HEAD_DIM), F32)],
        scratch_shapes=[pltpu.VMEM((bb, L, D), F32)] * 4,
        compiler_params=_cparams(("parallel",), vm),
        name="hgrn2_recurrence_step",
    )(r3(q), r3(k), r3(lf), r3(v), r3(g), s0, norm_w)
    return o.reshape(B * L, D), s


def _pool_kernel(*refs, pos0, has_halo):
    if has_halo:
        u_ref, halo_ref, buf_ref, pw_ref, ps_ref, o_ref, nb_ref, ext = refs
    else:
        u_ref, buf_ref, pw_ref, ps_ref, o_ref, nb_ref, ext = refs
    bb, tl, D = u_ref.shape
    G = len(POOL_WINDOWS)
    gc = D // G
    l_idx = pl.program_id(1)
    ext[:, HALO:, :] = u_ref[...]
    if has_halo:
        @pl.when(l_idx == 0)
        def _():
            ext[:, :HALO, :] = buf_ref[...]

        @pl.when(l_idx > 0)
        def _():
            ext[:, :HALO, :] = halo_ref[...]
    else:
        ext[:, :HALO, :] = buf_ref[...]

    pos = pos0 + l_idx * tl + lax.broadcasted_iota(jnp.int32, (tl, gc), 0)
    for gi, w in enumerate(POOL_WINDOWS):
        ls = slice(gi * gc, (gi + 1) * gc)
        cnt = jnp.minimum(w, pos + 1).astype(F32)
        rows = []
        for i in range(bb):
            u = ext[i, HALO:HALO + tl, ls]
            s = u
            for d in range(1, w):
                s = s + ext[i, HALO - d:HALO - d + tl, ls]
            rows.append(s / cnt - u)
        pooled = rows[0] if bb == 1 else jnp.concatenate(rows, axis=0)
        mixed = (_dot(pooled.astype(BF16), pw_ref[gi]) * ps_ref[:, ls]).astype(o_ref.dtype)
        for i in range(bb):
            o_ref[i, :, ls] = mixed[i * tl:(i + 1) * tl]

    @pl.when(l_idx == pl.num_programs(1) - 1)
    def _():
        nb_ref[...] = ext[:, tl:tl + HALO, :]


def _pool(u, buf, pos0, pool_w, pool_scale, mixer, B, L, bb, tl, out_dtype):
    D = u.shape[-1]
    G = len(POOL_WINDOWS)
    gc = D // G
    tl = min(tl, L)
    u3 = u.reshape(B, L, D)
    has_halo = L > tl
    step = tl // HALO
    in_specs = [pl.BlockSpec((bb, tl, D), lambda b, l: (b, l, 0))]
    args = [u3]
    if has_halo:
        in_specs.append(pl.BlockSpec((bb, HALO, D), lambda b, l: (b, jnp.maximum(l * step - 1, 0), 0)))
        args.append(u3)
    in_specs += [pl.BlockSpec((None, bb, HALO, D), lambda b, l: (mixer, b, 0, 0)),
                 pl.BlockSpec((None, G, gc, gc), lambda b, l: (mixer, 0, 0, 0)),
                 _row(mixer)(D)]
    args += [buf, pool_w, pool_scale]
    vm = 2 * (_nbytes((bb, tl, D), F32) + 3 * _nbytes((bb, HALO, D), F32) + _nbytes((G, gc, gc), BF16)
              + _nbytes((bb, tl, D), BF16)) + _nbytes((bb, tl + HALO, D), F32) + 4 * _nbytes((bb, tl, D), F32)
    mixed, newbuf = pl.pallas_call(
        functools.partial(_pool_kernel, pos0=pos0, has_halo=has_halo),
        grid=(B // bb, L // tl),
        in_specs=in_specs,
        out_specs=[pl.BlockSpec((bb, tl, D), lambda b, l: (b, l, 0)),
                   pl.BlockSpec((bb, HALO, D), lambda b, l: (b, 0, 0))],
        out_shape=[jax.ShapeDtypeStruct((B, L, D), out_dtype), jax.ShapeDtypeStruct((B, HALO, D), F32)],
        scratch_shapes=[pltpu.VMEM((bb, tl + HALO, D), F32)],
        compiler_params=_cparams(("parallel", "arbitrary"), vm),
        name="causal_pool",
    )(*args)
    return mixed.reshape(B * L, D), newbuf


def _attn_kernel(q_ref, k_ref, v_ref, o_ref, *, heads):
    bb, tq, D = q_ref.shape
    hd = D // heads
    scale = hd ** -0.5
    for i in range(bb):
        for h in range(heads):
            hs = slice(h * hd, (h + 1) * hd)
            s = _dot_nt(q_ref[i, :, hs].astype(BF16), k_ref[i, :, hs].astype(BF16)) * scale
            p = jnp.exp(s - jnp.max(s, axis=-1, keepdims=True))
            den = jnp.sum(p, axis=-1, keepdims=True)
            o_ref[i, :, hs] = (_dot(p.astype(BF16), v_ref[i, :, hs].astype(BF16)) / den).astype(o_ref.dtype)


def _attn(q, mk, mv, layer, B, L, heads, bb, tq, out_dtype):
    D = q.shape[-1]
    tq = min(tq, L)
    kv_shape = (bb,) + mk.shape[2:]
    kv_blk = (None,) + kv_shape
    vm = 2 * (_nbytes((bb, tq, D), q.dtype) + 2 * _nbytes(kv_shape, mk.dtype) + _nbytes((bb, tq, D), out_dtype))
    vm += 2 * _nbytes(kv_shape, BF16) + 8 * _nbytes((tq, D // heads), F32)
    o = pl.pallas_call(
        functools.partial(_attn_kernel, heads=heads),
        grid=(B // bb, L // tq),
        in_specs=[pl.BlockSpec((bb, tq, D), lambda b, l: (b, l, 0)),
                  pl.BlockSpec(kv_blk, lambda b, l: (layer, b, 0, 0)),
                  pl.BlockSpec(kv_blk, lambda b, l: (layer, b, 0, 0))],
        out_specs=pl.BlockSpec((bb, tq, D), lambda b, l: (b, l, 0)),
        out_shape=jax.ShapeDtypeStruct((B, L, D), out_dtype),
        compiler_params=_cparams(("parallel", "parallel"), vm),
        name="memory_cross_attention",
    )(q.reshape(B, L, D), mk, mv)
    return o.reshape(B * L, D)


def _attn_decode_pieces(q_ref, k_ref, v_ref, o_ref, *, heads):
    bb, tq, D = q_ref.shape
    n_mem, nu = k_ref.shape[1], k_ref.shape[2]
    hd = D // heads
    per = SUBL // heads
    W = n_mem * SUBL
    scale = hd ** -0.5
    res = lax.broadcasted_iota(jnp.int32, (tq, W), 1) % SUBL
    tile = lambda r: jnp.concatenate([r] * (W // LANES), axis=1)

    def feat(j):
        c, h = divmod(j, heads)
        return slice(h * hd + c * LANES, h * hd + (c + 1) * LANES)

    def same_row_reduce(x, op):
        r = x[:, :LANES]
        for j in range(1, W // LANES):
            r = op(r, x[:, j * LANES:(j + 1) * LANES])
        sh = SUBL
        while sh < LANES:
            r = op(r, pltpu.roll(r, sh, 1))
            sh *= 2
        return r

    for i in range(bb):
        w = None
        for u in range(nu):
            slab = k_ref[i, :, u].reshape(W, LANES).astype(BF16)
            qm = jnp.concatenate([q_ref[i, :, feat(u * SUBL + s)] for s in range(SUBL)], axis=0)
            p = _dot_nt(qm.astype(BF16), slab)
            z = p[:tq]
            for s in range(1, SUBL):
                z = jnp.where(res == s, p[s * tq:(s + 1) * tq], z)
            w = z if w is None else w + z
            yield
        sc = w
        for c in range(1, per):
            sc = sc + pltpu.roll(w, W - c * heads, 1)
        sc = sc * scale
        e = jnp.exp(sc - tile(same_row_reduce(sc, jnp.maximum)))
        pn = e / tile(same_row_reduce(e, jnp.add))
        blocks = []
        for s in range(SUBL):
            c_lo = s // heads
            src = pn if c_lo == 0 else pltpu.roll(pn, c_lo * heads, 1)
            blocks.append(jnp.where(res == s, src, 0.0))
        pm = jnp.concatenate(blocks, axis=0).astype(BF16)
        yield
        for u in range(nu):
            o = _dot(pm, v_ref[i, :, u].reshape(W, LANES).astype(BF16))
            for s in range(SUBL):
                o_ref[i, :, feat(u * SUBL + s)] = o[s * tq:(s + 1) * tq].astype(o_ref.dtype)
            yield


def _attn_decode_kernel(q_ref, k_ref, v_ref, o_ref, *, heads):
    for _ in _attn_decode_pieces(q_ref, k_ref, v_ref, o_ref, heads=heads):
        pass


def _attn_decode(q, ck, cv, layer, B, L, heads, bb, out_dtype):
    D = q.shape[-1]
    assert L == SUBL and SUBL % heads == 0 and D % (SUBL * LANES) == 0
    kv_shape = (bb,) + ck.shape[2:]
    vm = 2 * (2 * _nbytes((bb, L, D), F32) + 2 * _nbytes(kv_shape, ck.dtype))
    vm += 4 * _nbytes(kv_shape[1:], BF16) + 6 * _nbytes((SUBL * L, ck.shape[2] * SUBL), F32)
    o = pl.pallas_call(
        functools.partial(_attn_decode_kernel, heads=heads),
        grid=(B // bb,),
        in_specs=[pl.BlockSpec((bb, L, D), lambda b: (b, 0, 0)),
                  pl.BlockSpec((None,) + kv_shape, lambda b: (layer, b, 0, 0, 0, 0)),
                  pl.BlockSpec((None,) + kv_shape, lambda b: (layer, b, 0, 0, 0, 0))],
        out_specs=pl.BlockSpec((bb, L, D), lambda b: (b, 0, 0)),
        out_shape=jax.ShapeDtypeStruct((B, L, D), out_dtype),
        compiler_params=_cparams(("parallel",), vm),
        name="memory_cross_attention_decode",
    )(q.reshape(B, L, D), ck, cv)
    return o.reshape(B * L, D)


def _mem_proj_kernel(x_ref, nw_ref, w_ref, flat_ref, bf_ref, w_scr, *, heads):
    tm, D = x_ref.shape
    R = D // LANES
    nchunk = R // heads

    @pl.when(pl.program_id(1) == 0)
    def _():
        w_scr[...] = w_ref[...].astype(BF16)

    acc = _dot(_rms(x_ref[...], nw_ref[...]).astype(BF16), w_scr[...])
    bf_ref[...] = acc.astype(bf_ref.dtype)
    for h in range(heads):
        for c in range(nchunk):
            j = h * nchunk + c
            flat_ref[pl.ds(c * heads + h, tm, stride=R), :] = acc[:, j * LANES:(j + 1) * LANES]


def _mem_proj(mem, norm_w, w, half, heads, tm=256):
    M, D = mem.shape
    depth = w.shape[0]
    R = D // LANES
    tm = min(tm, M)
    nt = M // tm
    vm = 2 * (_nbytes((tm, D), F32) + _nbytes((D, D), F32) + _nbytes((tm, D), F32) + _nbytes((tm, D), BF16))
    vm += _nbytes((D, D), BF16) + 2 * _nbytes((tm, D), F32)
    return pl.pallas_call(
        functools.partial(_mem_proj_kernel, heads=heads),
        grid=(depth, nt),
        in_specs=[pl.BlockSpec((tm, D), lambda l, i: (i, 0)),
                  pl.BlockSpec((None, 1, D), lambda l, i: (l, 0, 0)),
                  pl.BlockSpec((None, D, D), lambda l, i: (l, 0, half))],
        out_specs=[pl.BlockSpec((tm * R, LANES), lambda l, i: (l * nt + i, 0)),
                   pl.BlockSpec((None, tm, D), lambda l, i: (l, i, 0))],
        out_shape=[jax.ShapeDtypeStruct((depth * M * R, LANES), F32),
                   jax.ShapeDtypeStruct((depth, M, D), BF16)],
        scratch_shapes=[pltpu.VMEM((D, D), BF16)],
        compiler_params=_cparams(("parallel", "arbitrary"), vm),
        name="memory_kv_projection",
    )(mem, norm_w, w)


class _Group:
    def __init__(self, x, B, L, pos0, S_in, buf_in, mk, mv, prompt, p):
        self.B, self.L, self.pos0, self.S_in, self.buf_in, self.mk, self.mv = B, L, pos0, S_in, buf_in, mk, mv
        self.prompt = prompt
        self.act = BF16 if prompt else F32
        self.x = x
        self.hn = _rmsnorm(x, p["norm_mix_pre"], 0)
        self.S_out, self.buf_out = [], []


def _mixer(g, l, p, decode=None):
    j = l // 2
    a_dec = None
    if l % 2 == 0:
        q, k, lf, v, gate = _proj_a(g.hn, p["w_in_a"], j, p["hg_lb_logits"], l, g.act)
        if g.prompt:
            o, S, a_dec = _gla_prompt(q, k, lf, v, gate, p["hg_norm"], j, g.B, g.L, decode)
        else:
            o, S = _gla_step(q, k, lf, v, gate, g.S_in, p["hg_norm"], j, g.B, g.L)
        g.S_out.append(S)
        g.x, g.hn = _mm_res(o, p["w_out_a"], j, g.x, p["norm_mix_post"], p["norm_x_pre"], l)
    else:
        (u,) = _matmul(g.hn, p["w_in_b"], j, [F32])
        bb, tl = (1, 512) if g.prompt else (16, g.L)
        mixed, nb = _pool(u, g.buf_in, g.pos0, p["pool_w"], p["pool_scale"], j, g.B, g.L, bb, tl, g.act)
        g.buf_out.append(nb[:, 1:, :])
        g.x, g.hn = _mm_res(mixed, p["w_out_b"], j, g.x, p["norm_mix_post"], p["norm_x_pre"], l)
    return a_dec


def _after_attention(g, l, a, p):
    g.x, g.hn = _mm_res(a, p["w_xo"], l, g.x, p["norm_x_post"], p["norm_mlp_pre"], l)
    g.x, g.hn = _mlp(g.hn, p["w_up"], p["w_down"], g.x, p["norm_mlp_post"], p["norm_mix_pre"], l)


def _trunks(gp, gs, p):
    depth = p["w_xq"].shape[0]
    heads = p["mem_heads"]
    for l in range(depth):
        _mixer(gs, l, p)
        (qs,) = _matmul(gs.hn, p["w_xq"], l, [gs.act])
        ride = l % 2 == 0
        a_s = _mixer(gp, l, p, (qs, gs.mk, gs.mv, l, gs.B, gs.L, heads, gs.act) if ride else None)
        if not ride:
            a_s = _attn_decode(qs, gs.mk, gs.mv, l, gs.B, gs.L, heads, 4, gs.act)
        _after_attention(gs, l, a_s, p)
        (qp,) = _matmul(gp.hn, p["w_xq"], l, [gp.act])
        _after_attention(gp, l, _attn(qp, gp.mk, gp.mv, l, gp.B, gp.L, heads, 1, 512, gp.act), p)


def kernel(x_prompt, x_sample, state_hgrn, state_pool, cache_mem_k, cache_mem_v, mem_prompt, w_in_a, hg_lb_logits, hg_norm, w_out_a, w_in_b, pool_w, pool_scale, w_out_b, norm_mem, w_xq, w_xkv, w_xo, norm_mix_pre, norm_mix_post, norm_x_pre, norm_x_post, norm_mlp_pre, norm_mlp_post, w_up, w_down):
    B, L, D = x_prompt.shape
    Bs, Ls, _ = x_sample.shape
    depth = w_xq.shape[0]
    n_mem, heads = cache_mem_k.shape[2], cache_mem_k.shape[3]
    assert L % GLA_BLOCK == 0 and Ls <= GLA_SUB and D % (HEAD_DIM * len(POOL_WINDOWS)) == 0
    assert state_pool.shape[2] == HALO - 1
    bf = lambda a: a.astype(BF16)
    rows = lambda a: a.reshape(a.shape[0], 1, a.shape[1])
    p = dict(w_in_a=bf(w_in_a), hg_lb_logits=hg_lb_logits, hg_norm=rows(hg_norm), w_out_a=bf(w_out_a),
             w_in_b=bf(w_in_b), pool_w=bf(pool_w), pool_scale=rows(pool_scale), w_out_b=bf(w_out_b),
             w_xq=bf(w_xq), w_xo=bf(w_xo), norm_mix_pre=rows(norm_mix_pre), norm_mix_post=rows(norm_mix_post),
             norm_x_pre=rows(norm_x_pre), norm_x_post=rows(norm_x_post), norm_mlp_pre=rows(norm_mlp_pre),
             norm_mlp_post=rows(norm_mlp_post), w_up=bf(w_up), w_down=bf(w_down), mem_heads=heads)
    R = D // LANES
    nchunk = R // heads

    def cache_view(flat, nb):
        c = flat.reshape(depth, nb, n_mem, nchunk, heads, LANES)
        return jnp.swapaxes(c, 3, 4).reshape(depth, nb, n_mem, heads, D // heads)

    def flat_view(cache):
        nb = cache.shape[1]
        c = cache.reshape(depth, nb, n_mem, heads, nchunk, LANES)
        return jnp.swapaxes(c, 3, 4).reshape(depth, nb, n_mem, R // SUBL, SUBL, LANES)

    mem2d = mem_prompt.reshape(B * n_mem, D)
    k_flat, k_bf = _mem_proj(mem2d, rows(norm_mem), w_xkv, 0, heads)
    v_flat, v_bf = _mem_proj(mem2d, rows(norm_mem), w_xkv, 1, heads)
    cache_mem_k_prompt = cache_view(k_flat, B)
    cache_mem_v_prompt = cache_view(v_flat, B)
    mk_bf = k_bf.reshape(depth, B, n_mem, D)
    mv_bf = v_bf.reshape(depth, B, n_mem, D)

    n_b = state_pool.shape[0]
    buf0 = jnp.zeros((n_b, B, HALO, D), F32)
    buf_s = jnp.pad(state_pool, ((0, 0), (0, 0), (1, 0), (0, 0)))
    gp = _Group(x_prompt.reshape(B * L, D), B, L, 0, None, buf0, mk_bf, mv_bf, True, p)
    gs = _Group(x_sample.reshape(Bs * Ls, D), Bs, Ls, PAST_LEN, state_hgrn, buf_s,
                flat_view(cache_mem_k), flat_view(cache_mem_v), False, p)
    _trunks(gp, gs, p)
    return (gp.x.reshape(B, L, D), gs.x.reshape(Bs, Ls, D), jnp.stack(gp.S_out), jnp.stack(gp.buf_out),
            cache_mem_k_prompt, cache_mem_v_prompt, jnp.stack(gs.S_out), jnp.stack(gs.buf_out))
```

```python
import functools

import numpy as np

import jax
import jax.numpy as jnp
from jax import lax
from jax.experimental import pallas as pl
from jax.experimental.pallas import tpu as pltpu

F32 = jnp.float32
BF16 = jnp.bfloat16
EPS = 1e-6
PAST_LEN = 16384
POOL_WINDOWS = (2, 4, 8, 16)
LANES = 128
SUBL = 8
HEAD_DIM = 128
GLA_BLOCK = 128
GLA_SUB = 16
HALO = 16
MM_RES_SPLIT = 4
SIDE_EVERY = 4

V7X_VMEM_BYTES = 64 * 1024 * 1024
VMEM_CAP = V7X_VMEM_BYTES - 6 * 1024 * 1024


def _cparams(sem, vmem_bytes):
    return pltpu.CompilerParams(dimension_semantics=sem,
                                vmem_limit_bytes=int(min(max(vmem_bytes, 16 * 1024 * 1024), VMEM_CAP)))


def _nbytes(shape, dtype):
    n = 1
    for s in shape:
        n *= s
    return n * jnp.dtype(dtype).itemsize


def _rms(v, w):
    ms = jnp.mean(v * v, axis=-1, keepdims=True)
    return v * lax.rsqrt(ms + EPS) * w


def _sigmoid(x):
    return 1.0 / (1.0 + jnp.exp(-x))


def _dot(a, b):
    return jnp.dot(a, b, preferred_element_type=F32)


def _dot_nt(a, b):
    return lax.dot_general(a, b, (((1,), (1,)), ((), ())), preferred_element_type=F32)


def _dot_tn(a, b, precision=None):
    return lax.dot_general(a, b, (((0,), (0,)), ((), ())), preferred_element_type=F32,
                           precision=precision)


def _rmsnorm_kernel(x_ref, w_ref, o_ref):
    o_ref[...] = _rms(x_ref[...], w_ref[...]).astype(o_ref.dtype)


def _row(layer):
    return lambda width: pl.BlockSpec((None, 1, width), lambda *_: (layer, 0, 0))


def _rmsnorm(x, w, layer, tm=512):
    M, D = x.shape
    tm = min(tm, M)
    return pl.pallas_call(
        _rmsnorm_kernel,
        grid=(M // tm,),
        in_specs=[pl.BlockSpec((tm, D), lambda i: (i, 0)), _row(layer)(D)],
        out_specs=pl.BlockSpec((tm, D), lambda i: (i, 0)),
        out_shape=jax.ShapeDtypeStruct((M, D), BF16),
        compiler_params=_cparams(("parallel",), 4 * _nbytes((tm, D), F32)),
        name="rmsnorm",
    )(x, w)


def _mm_kernel(a_ref, w_ref, *o_refs, hand_off):
    w = w_ref[...].astype(BF16)
    if hand_off:
        o_refs[-1][...] = w
        o_refs = o_refs[:-1]
    acc = _dot(a_ref[...], w)
    for o_ref in o_refs:
        o_ref[...] = acc.astype(o_ref.dtype)


def _matmul(a, w, layer, out_dtypes, tm=1024, tn=1024):
    M, K = a.shape
    N = w.shape[2]
    tm, tn = min(tm, M), min(tn, N)
    hand_off = w.dtype != BF16
    vm = 2 * (_nbytes((tm, K), BF16) + _nbytes((K, tn), w.dtype)) + _nbytes((tm, tn), F32)
    vm += sum(2 * _nbytes((tm, tn), d) for d in out_dtypes)
    out_specs = [pl.BlockSpec((tm, tn), lambda i, j: (i, j)) for _ in out_dtypes]
    out_shape = [jax.ShapeDtypeStruct((M, N), d) for d in out_dtypes]
    if hand_off:
        out_specs.append(pl.BlockSpec((None, K, tn), lambda i, j: (0, 0, j)))
        out_shape.append(jax.ShapeDtypeStruct((1, K, N), BF16))
        vm += 3 * _nbytes((K, tn), BF16)
    outs = pl.pallas_call(
        functools.partial(_mm_kernel, hand_off=hand_off),
        grid=(M // tm, N // tn),
        in_specs=[pl.BlockSpec((tm, K), lambda i, j: (i, 0)),
                  pl.BlockSpec((None, K, tn), lambda i, j: (layer, 0, j))],
        out_specs=out_specs,
        out_shape=out_shape,
        compiler_params=_cparams(("parallel", "parallel"), vm),
        name="matmul",
    )(a, w)
    return outs


def _proj_a_kernel(h_ref, wq_ref, wf_ref, wi_ref, wg_ref, lbl_ref, q_ref, k_ref, lf_ref, v_ref, g_ref,
                   *w_out_refs, layer):
    h = h_ref[...]
    ws = [r[...].astype(BF16) for r in (wq_ref, wf_ref, wi_ref, wg_ref)]
    for r, w in zip(w_out_refs, ws):
        r[...] = w
    aq = _dot(h, ws[0])
    q_ref[...] = (aq * _sigmoid(aq)).astype(q_ref.dtype)
    lg = lbl_ref[...]
    e = jnp.exp(lg - jnp.max(lg, axis=0, keepdims=True))
    lb = jnp.sum(e[:layer + 1], axis=0, keepdims=True) / jnp.sum(e, axis=0, keepdims=True)
    f = lb + (1.0 - lb) * _sigmoid(_dot(h, ws[1]))
    k_ref[...] = (1.0 - f).astype(k_ref.dtype)
    lf_ref[...] = jnp.log(f)
    v_ref[...] = _dot(h, ws[2]).astype(v_ref.dtype)
    ag = _dot(h, ws[3])
    g_ref[...] = (ag * _sigmoid(ag)).astype(g_ref.dtype)


def _proj_a(h, w, mixer, lb_logits, layer, act_dtype, tm=1024, tn=512):
    M, D = h.shape
    tm = min(tm, M)
    parts = isinstance(w, (tuple, list))
    hand_off = not parts and w.dtype != BF16
    if hand_off:
        tn = min(tn, 256)
    nj = D // tn
    nl = lb_logits.shape[0]
    if parts:
        wspec = lambda s: pl.BlockSpec((None, D, tn), lambda i, j: (0, 0, j))
        w_args, w_dtype = list(w), BF16
    else:
        wspec = lambda s: pl.BlockSpec((None, D, tn), lambda i, j: (mixer, 0, j + s * nj))
        w_args, w_dtype = [w] * 4, w.dtype
    ospec = pl.BlockSpec((tm, tn), lambda i, j: (i, j))
    vm = 2 * (_nbytes((tm, D), BF16) + 4 * _nbytes((D, tn), w_dtype) + 5 * _nbytes((tm, tn), F32))
    vm += 4 * _nbytes((tm, tn), F32)
    out_specs = [ospec] * 5
    out_shape = [jax.ShapeDtypeStruct((M, D), act_dtype), jax.ShapeDtypeStruct((M, D), act_dtype),
                 jax.ShapeDtypeStruct((M, D), F32), jax.ShapeDtypeStruct((M, D), act_dtype),
                 jax.ShapeDtypeStruct((M, D), act_dtype)]
    if hand_off:
        out_specs += [pl.BlockSpec((None, D, tn), lambda i, j: (0, 0, j))] * 4
        out_shape += [jax.ShapeDtypeStruct((1, D, D), BF16)] * 4
        vm += 12 * _nbytes((D, tn), BF16)
    outs = pl.pallas_call(
        functools.partial(_proj_a_kernel, layer=layer),
        grid=(M // tm, nj),
        in_specs=[pl.BlockSpec((tm, D), lambda i, j: (i, 0)), wspec(0), wspec(1), wspec(2), wspec(3),
                  pl.BlockSpec((nl, tn), lambda i, j: (0, j))],
        out_specs=out_specs,
        out_shape=out_shape,
        compiler_params=_cparams(("parallel", "parallel"), vm),
        name="hgrn2_proj",
    )(h, *w_args, lb_logits)
    return tuple(outs[:5]) + ((tuple(outs[5:]),) if hand_off else (None,))


def _mm_res_kernel(a_ref, w_ref, x_ref, pw_ref, nw_ref, ox_ref, oh_ref):
    tm = a_ref.shape[0]
    nsplit = MM_RES_SPLIT if tm % (8 * MM_RES_SPLIT) == 0 and tm // MM_RES_SPLIT >= 128 else 1
    for p in range(nsplit):
        rows = slice(p * tm // nsplit, (p + 1) * tm // nsplit)
        m = _dot(a_ref[rows, :].astype(BF16), w_ref[...])
        y = x_ref[rows, :] + _rms(m, pw_ref[...])
        ox_ref[rows, :] = y
        oh_ref[rows, :] = _rms(y, nw_ref[...]).astype(oh_ref.dtype)


def _mm_res(a, w, wl, x, post_w, next_w, layer, tm=512):
    M, K = a.shape
    D = w.shape[2]
    tm = min(tm, M)
    vm = 2 * (_nbytes((tm, K), a.dtype) + _nbytes((K, D), BF16) + 2 * _nbytes((tm, D), F32)
              + _nbytes((tm, D), BF16)) + 2 * _nbytes((tm, D), F32)
    return pl.pallas_call(
        _mm_res_kernel,
        grid=(M // tm,),
        in_specs=[pl.BlockSpec((tm, K), lambda i: (i, 0)),
                  pl.BlockSpec((None, K, D), lambda i: (wl, 0, 0)),
                  pl.BlockSpec((tm, D), lambda i: (i, 0)), _row(layer)(D), _row(layer)(D)],
        out_specs=[pl.BlockSpec((tm, D), lambda i: (i, 0)), pl.BlockSpec((tm, D), lambda i: (i, 0))],
        out_shape=[jax.ShapeDtypeStruct((M, D), F32), jax.ShapeDtypeStruct((M, D), BF16)],
        compiler_params=_cparams(("parallel",), vm),
        name="matmul_norm_residual",
    )(a, w, x, post_w, next_w)


def _mlp_kernel(h_ref, wu_ref, wd_ref, x_ref, pw_ref, nw_ref, ox_ref, *rest, emit_next, hand_off):
    acc_ref = rest[-1]
    f = pl.program_id(1)

    @pl.when(f == 0)
    def _():
        acc_ref[...] = jnp.zeros_like(acc_ref)

    wu = wu_ref[...].astype(BF16)
    wd = wd_ref[...].astype(BF16)
    if hand_off:
        rest[-3][...] = wu
        rest[-2][...] = wd
    u = jnp.maximum(_dot(h_ref[...], wu), 0.0)
    acc_ref[...] += _dot((u * u).astype(BF16), wd)

    @pl.when(f == pl.num_programs(1) - 1)
    def _():
        y = x_ref[...] + _rms(acc_ref[...], pw_ref[...])
        ox_ref[...] = y
        if emit_next:
            rest[0][...] = _rms(y, nw_ref[...]).astype(BF16)


def _mlp(h, w_up, w_down, wl, x, post_w, next_w, layer, tm=512, tf=1024):
    M, D = h.shape
    FF = w_up.shape[2]
    tm = min(tm, M)
    hand_off = w_up.dtype != BF16
    if hand_off:
        tf = min(tf, 512)
    emit_next = layer + 1 < next_w.shape[0]
    tile = lambda: pl.BlockSpec((tm, D), lambda i, f: (i, 0))
    out_specs = [tile()] + ([tile()] if emit_next else [])
    out_shape = [jax.ShapeDtypeStruct((M, D), F32)] + ([jax.ShapeDtypeStruct((M, D), BF16)] if emit_next else [])
    vm = 2 * (2 * _nbytes((tm, D), BF16) + 2 * _nbytes((D, tf), w_up.dtype) + 2 * _nbytes((tm, D), F32))
    vm += 2 * _nbytes((tm, D), F32) + 2 * _nbytes((tm, tf), F32)
    if hand_off:
        out_specs += [pl.BlockSpec((None, D, tf), lambda i, f: (0, 0, f)),
                      pl.BlockSpec((None, tf, D), lambda i, f: (0, f, 0))]
        out_shape += [jax.ShapeDtypeStruct((1, D, FF), BF16), jax.ShapeDtypeStruct((1, FF, D), BF16)]
        vm += 6 * _nbytes((D, tf), BF16)
    outs = pl.pallas_call(
        functools.partial(_mlp_kernel, emit_next=emit_next, hand_off=hand_off),
        grid=(M // tm, FF // tf),
        in_specs=[tile(),
                  pl.BlockSpec((None, D, tf), lambda i, f: (wl, 0, f)),
                  pl.BlockSpec((None, tf, D), lambda i, f: (wl, f, 0)),
                  tile(), _row(layer)(D), _row(layer + 1 if emit_next else layer)(D)],
        out_specs=out_specs,
        out_shape=out_shape,
        scratch_shapes=[pltpu.VMEM((tm, D), F32)],
        compiler_params=_cparams(("parallel", "arbitrary"), vm),
        name="relu2_mlp",
    )(h, w_up, w_down, x, post_w, next_w)
    n = 2 if emit_next else 1
    return outs[0], (outs[1] if emit_next else None), (tuple(outs[n:]) if hand_off else None)


def _split3(x):
    hi = x.astype(BF16)
    r1 = x - hi.astype(F32)
    mid = r1.astype(BF16)
    lo = (r1 - mid.astype(F32)).astype(BF16)
    return hi, mid, lo


def _decay_rows(d, rows):
    hi, mid, lo = _split3(d)
    r = lax.broadcasted_iota(jnp.int32, (rows, d.shape[-1]), 0)
    return jnp.where(r == 0, hi.astype(F32), jnp.where(r == 1, mid.astype(F32),
                                                       jnp.where(r == 2, lo.astype(F32), 0.0)))


def _gla_prompt_kernel(*refs, decode_heads=None):
    if decode_heads is None:
        (q_ref, k_ref, lf_ref, v_ref, g_ref, nw_ref, sel_ref, lvl_ref, o_ref, sout_ref,
         s_scr, qs_scr, ks_scr, ke_scr, fac_scr, a_scr) = refs
    else:
        (q_ref, k_ref, lf_ref, v_ref, g_ref, nw_ref, sel_ref, lvl_ref, dq_ref, dk_ref, dv_ref,
         o_ref, sout_ref, do_ref, s_scr, qs_scr, ks_scr, ke_scr, fac_scr, a_scr) = refs
    T = q_ref.shape[1]
    D = q_ref.shape[2]
    H = D // HEAD_DIM
    SUB = GLA_SUB
    nb = T // SUB
    c_idx = pl.program_id(1)

    @pl.when(c_idx == 0)
    def _():
        s_scr[...] = jnp.zeros_like(s_scr)

    side = iter(()) if decode_heads is None else _attn_decode_pieces(dq_ref, dk_ref, dv_ref, do_ref,
                                                                     heads=decode_heads)
    stage = [0]

    def side_step():
        stage[0] += 1
        if stage[0] % SIDE_EVERY == 0:
            next(side, None)

    chunks = []
    c = 2 * SUB
    while c < T:
        chunks.append(c)
        c *= 2
    nf = 2 * len(chunks) + 3
    sums = _dot(sel_ref[...], jnp.concatenate(_split3(lf_ref[0]), axis=0))
    fac_scr[...] = jnp.exp(sums[T:])
    nq = len(chunks) + 1
    for i in range(nb):
        rows = slice(i * SUB, (i + 1) * SUB)
        r = sums[rows]
        q16 = q_ref[0, rows, :].astype(F32) * jnp.exp(r)
        kd = k_ref[0, rows, :].astype(F32) * jnp.exp(-r)
        f_row = lambda n: fac_scr[n * nb + i:n * nb + i + 1, :]
        qs_scr[0, rows, :] = (q16 * f_row(0)).astype(BF16)
        qs_scr[1, rows, :] = q16.astype(BF16)
        ks_scr[0, rows, :] = kd.astype(BF16)
        ks_scr[1, rows, :] = (kd * f_row(nq)).astype(BF16)
        for n in range(len(chunks)):
            qs_scr[n + 2, rows, :] = (q16 * f_row(n + 1)).astype(BF16)
            ks_scr[n + 2, rows, :] = (kd * f_row(nq + n + 1)).astype(BF16)
        ke_scr[rows, :] = (kd * f_row(nf - 1)).astype(BF16)
        side_step()
    ke_scr[T:, :] = _decay_rows(fac_scr[(nf - 1) * nb:(nf - 1) * nb + 1, :], SUB).astype(BF16)

    rr = lax.broadcasted_iota(jnp.int32, (SUB, 2 * HEAD_DIM), 0)
    cc = lax.broadcasted_iota(jnp.int32, (SUB, 2 * HEAD_DIM), 1)
    pick = jnp.where((rr < 3) & (cc >= HEAD_DIM), 1.0, 0.0).astype(BF16)
    zeros = jnp.zeros((T, HEAD_DIM), BF16)
    nw = nw_ref[...]

    lvl = lvl_ref[...]
    for h in range(H):
        hs = slice(h * HEAD_DIM, (h + 1) * HEAD_DIM)
        r = _dot_nt(qs_scr[1, :, hs], ks_scr[0:2, :, hs].reshape(2 * T, HEAD_DIM))
        a = jnp.where(lvl == 1, r[:, :T], jnp.where(lvl == 2, r[:, T:], 0.0))
        for n in range(len(chunks)):
            a = jnp.where(lvl == n + 3, _dot_nt(qs_scr[n + 2, :, hs], ks_scr[n + 2, :, hs]), a)
        a_scr[h] = a.astype(BF16)
        side_step()
    for h in range(H):
        hs = slice(h * HEAD_DIM, (h + 1) * HEAD_DIM)
        vh = v_ref[0, :, hs]
        s_old = s_scr[h]
        o = _dot(jnp.concatenate([a_scr[h], qs_scr[0, :, hs]], axis=1),
                 jnp.concatenate([vh, s_old.astype(BF16)], axis=0))
        rhs = jnp.concatenate([jnp.concatenate([vh, zeros], axis=1), pick], axis=0)
        upd = _dot_tn(ke_scr[:, hs], rhs)
        s_scr[h] = s_old * upd[:, HEAD_DIM:] + upd[:, :HEAD_DIM]
        o_ref[0, :, hs] = (_rms(o, nw) * g_ref[0, :, hs].astype(F32)).astype(o_ref.dtype)
        side_step()
    for _ in side:
        pass

    @pl.when(c_idx == pl.num_programs(1) - 1)
    def _():
        sout_ref[0] = s_scr[...]


def _gla_range_matrix(T, SUB):
    chunks = []
    c = 2 * SUB
    while c < T:
        chunks.append(c)
        c *= 2
    nb = T // SUB
    t = np.arange(T)
    rows = [((t[None, :] <= t[:, None]) & (t[None, :] // SUB == t[:, None] // SUB))]
    start = np.arange(nb)[:, None] * SUB
    for C in [T] + chunks:
        rows.append((t[None, :] >= (start // C) * C) & (t[None, :] < start))
    for C in [SUB] + chunks + [T]:
        rows.append((t[None, :] >= start) & (t[None, :] < (start // C + 1) * C))
    sel = np.concatenate(rows, axis=0).astype(np.float32)
    row, col = t[:, None], t[None, :]
    lvl = np.where((row // SUB == col // SUB) & (col <= row), 1, 0)
    for j, C in enumerate([SUB] + chunks):
        lvl = np.where(((row // C) % 2 == 1) & (col // C == row // C - 1), 2 + j, lvl)
    return sel, lvl.astype(np.int32), len(chunks)


def _gla_prompt(q, k, lf, v, g, norm_w, mixer, B, L, decode=None):
    D = q.shape[-1]
    H = D // HEAD_DIM
    T = GLA_BLOCK
    assert T == HEAD_DIM
    nc = L // T
    sel, lvl, nchunks = _gla_range_matrix(T, GLA_SUB)
    sel3 = jnp.asarray(np.concatenate([sel] * 3, axis=1), BF16)
    nfac = sel.shape[0] - T
    r3 = lambda a: a.reshape(B, L, D)
    blk = lambda: pl.BlockSpec((1, T, D), lambda b, c: (b, c, 0))
    const = lambda a: pl.BlockSpec(a.shape, lambda b, c: (0, 0))
    vm = 2 * (5 * _nbytes((T, D), BF16) + _nbytes((T, D), F32)) + 3 * _nbytes((H, HEAD_DIM, HEAD_DIM), F32)
    vm += (2 * nchunks + 6) * _nbytes((T, D), BF16) + 8 * _nbytes((T, D), F32)
    in_specs = [blk(), blk(), blk(), blk(), blk(), _row(mixer)(HEAD_DIM), const(sel3), const(lvl)]
    args = [r3(q), r3(k), r3(lf), r3(v), r3(g), norm_w, sel3, jnp.asarray(lvl)]
    out_specs = [blk(), pl.BlockSpec((1, H, HEAD_DIM, HEAD_DIM), lambda b, c: (b, 0, 0, 0))]
    out_shape = [jax.ShapeDtypeStruct((B, L, D), BF16), jax.ShapeDtypeStruct((B, H, HEAD_DIM, HEAD_DIM), F32)]
    heads = None
    if decode is not None:
        dq, ck, cv, layer, Bd, Ld, heads, d_dtype = decode
        assert Ld == SUBL and SUBL % heads == 0 and Bd % (B * nc) == 0
        bb = Bd // (B * nc)
        kv_shape = (bb,) + ck.shape[2:]
        kv_spec = lambda: pl.BlockSpec((None,) + kv_shape, lambda b, c: (layer, b * nc + c, 0, 0, 0, 0))
        dblk = lambda: pl.BlockSpec((bb, Ld, D), lambda b, c: (b * nc + c, 0, 0))
        in_specs += [dblk(), kv_spec(), kv_spec()]
        args += [dq.reshape(Bd, Ld, D), ck, cv]
        out_specs.append(dblk())
        out_shape.append(jax.ShapeDtypeStruct((Bd, Ld, D), d_dtype))
        vm += 4 * _nbytes((bb, Ld, D), F32) + 4 * _nbytes(kv_shape, ck.dtype)
        vm += 4 * _nbytes(kv_shape[1:], BF16) + 6 * _nbytes((SUBL * Ld, ck.shape[2] * SUBL), F32)
    outs = pl.pallas_call(
        functools.partial(_gla_prompt_kernel, decode_heads=heads),
        grid=(B, nc),
        in_specs=in_specs,
        out_specs=out_specs,
        out_shape=out_shape,
        scratch_shapes=[pltpu.VMEM((H, HEAD_DIM, HEAD_DIM), F32),
                        pltpu.VMEM((nchunks + 2, T, D), BF16),
                        pltpu.VMEM((nchunks + 2, T, D), BF16),
                        pltpu.VMEM((T + GLA_SUB, D), BF16),
                        pltpu.VMEM((nfac, D), F32),
                        pltpu.VMEM((H, T, T), BF16)],
        compiler_params=_cparams(("parallel", "arbitrary"), vm),
        name="hgrn2_recurrence_prompt",
    )(*args)
    a_dec = outs[2].reshape(Bd * Ld, D) if decode is not None else None
    return outs[0].reshape(B * L, D), outs[1], a_dec


def _gla_step_kernel(q_ref, k_ref, lf_ref, v_ref, g_ref, s0_ref, nw_ref, o_ref, sout_ref,
                     qs_scr, ks_scr, ke_scr, ds_scr):
    bb, T, D = q_ref.shape
    H = D // HEAD_DIM
    R = bb * T
    pos = lax.broadcasted_iota(jnp.int32, (bb, T, D), 1)
    b = lf_ref[...]
    sh = 1
    while sh < T:
        b = b + jnp.where(pos >= sh, pltpu.roll(b, sh, 1), 0.0)
        sh *= 2
    b_last = b[:, T - 1:T, :]
    k = k_ref[...]
    qs_scr[...] = q_ref[...] * jnp.exp(b)
    ks_scr[...] = k * jnp.exp(-b)
    ke_scr[...] = k * jnp.exp(b_last - b)
    d_hi, d_mid, d_lo = (t.astype(F32) for t in _split3(jnp.exp(b_last)))
    ds_scr[...] = jnp.where(pos == 0, d_hi, jnp.where(pos == 1, d_mid, jnp.where(pos == 2, d_lo, 0.0)))

    row = lax.broadcasted_iota(jnp.int32, (R, R), 0)
    col = lax.broadcasted_iota(jnp.int32, (R, R), 1)
    same_seq_causal = (row // T == col // T) & (col <= row)
    rr = lax.broadcasted_iota(jnp.int32, (T, 2 * HEAD_DIM), 0)
    cc = lax.broadcasted_iota(jnp.int32, (T, 2 * HEAD_DIM), 1)
    pick = jnp.where((rr < 3) & (cc >= HEAD_DIM), 1.0, 0.0)
    zeros = jnp.zeros((T, HEAD_DIM), F32)
    nw = nw_ref[...]

    for h in range(H):
        hs = slice(h * HEAD_DIM, (h + 1) * HEAD_DIM)
        qd = qs_scr[:, :, hs].reshape(R, HEAD_DIM)
        kd = ks_scr[:, :, hs].reshape(R, HEAD_DIM)
        vh = v_ref[:, :, hs].reshape(R, HEAD_DIM)
        a = jnp.where(same_seq_causal, _dot_nt(qd.astype(BF16), kd.astype(BF16)), 0.0)
        o_intra = _dot(a.astype(BF16), vh.astype(BF16))
        outs = []
        for i in range(bb):
            s_old = s0_ref[i, h]
            outs.append(o_intra[i * T:(i + 1) * T]
                        + _dot(qd[i * T:(i + 1) * T].astype(BF16), s_old.astype(BF16)))
            lhs = jnp.concatenate([ke_scr[i, :, hs], ds_scr[i, :, hs]], axis=0).astype(BF16)
            rhs = jnp.concatenate([jnp.concatenate([vh[i * T:(i + 1) * T], zeros], axis=1), pick],
                                  axis=0).astype(BF16)
            upd = _dot_tn(lhs, rhs)
            sout_ref[i, h] = s_old * upd[:, HEAD_DIM:] + upd[:, :HEAD_DIM]
        o = jnp.concatenate(outs, axis=0)
        on = _rms(o, nw) * g_ref[:, :, hs].reshape(R, HEAD_DIM)
        o_ref[:, :, hs] = on.reshape(bb, T, HEAD_DIM).astype(o_ref.dtype)


def _gla_step(q, k, lf, v, g, s0, norm_w, mixer, B, L, bb=8):
    D = q.shape[-1]
    H = D // HEAD_DIM
    assert L == 8 and B % bb == 0
    r3 = lambda a: a.reshape(B, L, D)
    blk = lambda: pl.BlockSpec((bb, L, D), lambda b: (b, 0, 0))
    sblk = lambda: pl.BlockSpec((bb, H, HEAD_DIM, HEAD_DIM), lambda b: (b, 0, 0, 0))
    vm = 2 * (6 * _nbytes((bb, L, D), F32) + 2 * _nbytes((bb, H, HEAD_DIM, HEAD_DIM), F32))
    vm += 12 * _nbytes((bb, L, D), F32)
    o, s = pl.pallas_call(
        _gla_step_kernel,
        grid=(B // bb,),
        in_specs=[blk(), blk(), blk(), blk(), blk(),
                  pl.BlockSpec((None, bb, H, HEAD_DIM, HEAD_DIM), lambda b: (mixer, b, 0, 0, 0)),
                  _row(mixer)(HEAD_DIM)],
        out_specs=[blk(), sblk()],
        out_shape=[jax.ShapeDtypeStruct((B, L, D), F32),
                   jax.ShapeDtypeStruct((B, H, HEAD_DIM, HEAD_DIM), F32)],
        scratch_shapes=[pltpu.VMEM((bb, L, D), F32)] * 4,
        compiler_params=_cparams(("parallel",), vm),
        name="hgrn2_recurrence_step",
    )(r3(q), r3(k), r3(lf), r3(v), r3(g), s0, norm_w)
    return o.reshape(B * L, D), s


def _pool_kernel(*refs, pos0, has_halo):
    if has_halo:
        u_ref, halo_ref, buf_ref, pw_ref, ps_ref, o_ref, nb_ref, ext = refs
    else:
        u_ref, buf_ref, pw_ref, ps_ref, o_ref, nb_ref, ext = refs
    bb, tl, D = u_ref.shape
    G = len(POOL_WINDOWS)
    gc = D // G
    l_idx = pl.program_id(1)
    ext[:, HALO:, :] = u_ref[...]
    if has_halo:
        @pl.when(l_idx == 0)
        def _():
            ext[:, :HALO, :] = buf_ref[...]

        @pl.when(l_idx > 0)
        def _():
            ext[:, :HALO, :] = halo_ref[...]
    else:
        ext[:, :HALO, :] = buf_ref[...]

    pos = pos0 + l_idx * tl + lax.broadcasted_iota(jnp.int32, (tl, gc), 0)
    for gi, w in enumerate(POOL_WINDOWS):
        ls = slice(gi * gc, (gi + 1) * gc)
        cnt = jnp.minimum(w, pos + 1).astype(F32)
        rows = []
        for i in range(bb):
            u = ext[i, HALO:HALO + tl, ls]
            s = u
            for d in range(1, w):
                s = s + ext[i, HALO - d:HALO - d + tl, ls]
            rows.append(s / cnt - u)
        pooled = rows[0] if bb == 1 else jnp.concatenate(rows, axis=0)
        mixed = (_dot(pooled.astype(BF16), pw_ref[gi]) * ps_ref[:, ls]).astype(o_ref.dtype)
        for i in range(bb):
            o_ref[i, :, ls] = mixed[i * tl:(i + 1) * tl]

    @pl.when(l_idx == pl.num_programs(1) - 1)
    def _():
        nb_ref[...] = ext[:, tl:tl + HALO, :]


def _pool(u, buf, pos0, pool_w, pool_scale, mixer, B, L, bb, tl, out_dtype):
    D = u.shape[-1]
    G = len(POOL_WINDOWS)
    gc = D // G
    tl = min(tl, L)
    u3 = u.reshape(B, L, D)
    has_halo = L > tl
    step = tl // HALO
    in_specs = [pl.BlockSpec((bb, tl, D), lambda b, l: (b, l, 0))]
    args = [u3]
    if has_halo:
        in_specs.append(pl.BlockSpec((bb, HALO, D), lambda b, l: (b, jnp.maximum(l * step - 1, 0), 0)))
        args.append(u3)
    in_specs += [pl.BlockSpec((None, bb, HALO, D), lambda b, l: (mixer, b, 0, 0)),
                 pl.BlockSpec((None, G, gc, gc), lambda b, l: (mixer, 0, 0, 0)),
                 _row(mixer)(D)]
    args += [buf, pool_w, pool_scale]
    vm = 2 * (_nbytes((bb, tl, D), F32) + 3 * _nbytes((bb, HALO, D), F32) + _nbytes((G, gc, gc), BF16)
              + _nbytes((bb, tl, D), BF16)) + _nbytes((bb, tl + HALO, D), F32) + 4 * _nbytes((bb, tl, D), F32)
    mixed, newbuf = pl.pallas_call(
        functools.partial(_pool_kernel, pos0=pos0, has_halo=has_halo),
        grid=(B // bb, L // tl),
        in_specs=in_specs,
        out_specs=[pl.BlockSpec((bb, tl, D), lambda b, l: (b, l, 0)),
                   pl.BlockSpec((bb, HALO, D), lambda b, l: (b, 0, 0))],
        out_shape=[jax.ShapeDtypeStruct((B, L, D), out_dtype), jax.ShapeDtypeStruct((B, HALO, D), F32)],
        scratch_shapes=[pltpu.VMEM((bb, tl + HALO, D), F32)],
        compiler_params=_cparams(("parallel", "arbitrary"), vm),
        name="causal_pool",
    )(*args)
    return mixed.reshape(B * L, D), newbuf


def _attn_kernel(q_ref, k_ref, v_ref, o_ref, *, heads):
    bb, tq, D = q_ref.shape
    hd = D // heads
    scale = hd ** -0.5
    for i in range(bb):
        for h in range(heads):
            hs = slice(h * hd, (h + 1) * hd)
            s = _dot_nt(q_ref[i, :, hs].astype(BF16), k_ref[i, :, hs].astype(BF16)) * scale
            p = jnp.exp(s - jnp.max(s, axis=-1, keepdims=True))
            den = jnp.sum(p, axis=-1, keepdims=True)
            o_ref[i, :, hs] = (_dot(p.astype(BF16), v_ref[i, :, hs].astype(BF16)) / den).astype(o_ref.dtype)


def _attn(q, mk, mv, layer, B, L, heads, bb, tq, out_dtype):
    D = q.shape[-1]
    tq = min(tq, L)
    kv_shape = (bb,) + mk.shape[2:]
    kv_blk = (None,) + kv_shape
    vm = 2 * (_nbytes((bb, tq, D), q.dtype) + 2 * _nbytes(kv_shape, mk.dtype) + _nbytes((bb, tq, D), out_dtype))
    vm += 2 * _nbytes(kv_shape, BF16) + 8 * _nbytes((tq, D // heads), F32)
    o = pl.pallas_call(
        functools.partial(_attn_kernel, heads=heads),
        grid=(B // bb, L // tq),
        in_specs=[pl.BlockSpec((bb, tq, D), lambda b, l: (b, l, 0)),
                  pl.BlockSpec(kv_blk, lambda b, l: (layer, b, 0, 0)),
                  pl.BlockSpec(kv_blk, lambda b, l: (layer, b, 0, 0))],
        out_specs=pl.BlockSpec((bb, tq, D), lambda b, l: (b, l, 0)),
        out_shape=jax.ShapeDtypeStruct((B, L, D), out_dtype),
        compiler_params=_cparams(("parallel", "parallel"), vm),
        name="memory_cross_attention",
    )(q.reshape(B, L, D), mk, mv)
    return o.reshape(B * L, D)


def _attn_decode_pieces(q_ref, k_ref, v_ref, o_ref, *, heads):
    bb, tq, D = q_ref.shape
    n_mem, nu = k_ref.shape[1], k_ref.shape[2]
    hd = D // heads
    per = SUBL // heads
    W = n_mem * SUBL
    scale = hd ** -0.5
    res = lax.broadcasted_iota(jnp.int32, (tq, W), 1) % SUBL
    tile = lambda r: jnp.concatenate([r] * (W // LANES), axis=1)

    def feat(j):
        c, h = divmod(j, heads)
        return slice(h * hd + c * LANES, h * hd + (c + 1) * LANES)

    def same_row_reduce(x, op):
        r = x[:, :LANES]
        for j in range(1, W // LANES):
            r = op(r, x[:, j * LANES:(j + 1) * LANES])
        sh = SUBL
        while sh < LANES:
            r = op(r, pltpu.roll(r, sh, 1))
            sh *= 2
        return r

    for i in range(bb):
        w = None
        for u in range(nu):
            slab = k_ref[i, :, u].reshape(W, LANES).astype(BF16)
            qm = jnp.concatenate([q_ref[i, :, feat(u * SUBL + s)] for s in range(SUBL)], axis=0)
            p = _dot_nt(qm.astype(BF16), slab)
            z = p[:tq]
            for s in range(1, SUBL):
                z = jnp.where(res == s, p[s * tq:(s + 1) * tq], z)
            w = z if w is None else w + z
            yield
        sc = w
        for c in range(1, per):
            sc = sc + pltpu.roll(w, W - c * heads, 1)
        sc = sc * scale
        e = jnp.exp(sc - tile(same_row_reduce(sc, jnp.maximum)))
        pn = e / tile(same_row_reduce(e, jnp.add))
        blocks = []
        for s in range(SUBL):
            c_lo = s // heads
            src = pn if c_lo == 0 else pltpu.roll(pn, c_lo * heads, 1)
            blocks.append(jnp.where(res == s, src, 0.0))
        pm = jnp.concatenate(blocks, axis=0).astype(BF16)
        yield
        for u in range(nu):
            o = _dot(pm, v_ref[i, :, u].reshape(W, LANES).astype(BF16))
            for s in range(SUBL):
                o_ref[i, :, feat(u * SUBL + s)] = o[s * tq:(s + 1) * tq].astype(o_ref.dtype)
            yield


def _attn_decode_kernel(q_ref, k_ref, v_ref, o_ref, *, heads):
    for _ in _attn_decode_pieces(q_ref, k_ref, v_ref, o_ref, heads=heads):
        pass


def _attn_decode(q, ck, cv, layer, B, L, heads, bb, out_dtype):
    D = q.shape[-1]
    assert L == SUBL and SUBL % heads == 0 and D % (SUBL * LANES) == 0
    kv_shape = (bb,) + ck.shape[2:]
    vm = 2 * (2 * _nbytes((bb, L, D), F32) + 2 * _nbytes(kv_shape, ck.dtype))
    vm += 4 * _nbytes(kv_shape[1:], BF16) + 6 * _nbytes((SUBL * L, ck.shape[2] * SUBL), F32)
    o = pl.pallas_call(
        functools.partial(_attn_decode_kernel, heads=heads),
        grid=(B // bb,),
        in_specs=[pl.BlockSpec((bb, L, D), lambda b: (b, 0, 0)),
                  pl.BlockSpec((None,) + kv_shape, lambda b: (layer, b, 0, 0, 0, 0)),
                  pl.BlockSpec((None,) + kv_shape, lambda b: (layer, b, 0, 0, 0, 0))],
        out_specs=pl.BlockSpec((bb, L, D), lambda b: (b, 0, 0)),
        out_shape=jax.ShapeDtypeStruct((B, L, D), out_dtype),
        compiler_params=_cparams(("parallel",), vm),
        name="memory_cross_attention_decode",
    )(q.reshape(B, L, D), ck, cv)
    return o.reshape(B * L, D)


def _mem_proj_kernel(x_ref, nw_ref, w_ref, flat_ref, bf_ref, w_scr, *, heads):
    tm, D = x_ref.shape
    R = D // LANES
    nchunk = R // heads

    @pl.when(pl.program_id(1) == 0)
    def _():
        w_scr[...] = w_ref[...].astype(BF16)

    acc = _dot(_rms(x_ref[...], nw_ref[...]).astype(BF16), w_scr[...])
    bf_ref[...] = acc.astype(bf_ref.dtype)
    for h in range(heads):
        for c in range(nchunk):
            j = h * nchunk + c
            flat_ref[pl.ds(c * heads + h, tm, stride=R), :] = acc[:, j * LANES:(j + 1) * LANES]


def _mem_proj(mem, norm_w, w, half, heads, tm=256):
    M, D = mem.shape
    depth = w.shape[0]
    R = D // LANES
    tm = min(tm, M)
    nt = M // tm
    vm = 2 * (_nbytes((tm, D), F32) + _nbytes((D, D), F32) + _nbytes((tm, D), F32) + _nbytes((tm, D), BF16))
    vm += _nbytes((D, D), BF16) + 2 * _nbytes((tm, D), F32)
    return pl.pallas_call(
        functools.partial(_mem_proj_kernel, heads=heads),
        grid=(depth, nt),
        in_specs=[pl.BlockSpec((tm, D), lambda l, i: (i, 0)),
                  pl.BlockSpec((None, 1, D), lambda l, i: (l, 0, 0)),
                  pl.BlockSpec((None, D, D), lambda l, i: (l, 0, half))],
        out_specs=[pl.BlockSpec((tm * R, LANES), lambda l, i: (l * nt + i, 0)),
                   pl.BlockSpec((None, tm, D), lambda l, i: (l, i, 0))],
        out_shape=[jax.ShapeDtypeStruct((depth * M * R, LANES), F32),
                   jax.ShapeDtypeStruct((depth, M, D), BF16)],
        scratch_shapes=[pltpu.VMEM((D, D), BF16)],
        compiler_params=_cparams(("parallel", "arbitrary"), vm),
        name="memory_kv_projection",
    )(mem, norm_w, w)


class _Group:
    def __init__(self, x, B, L, pos0, S_in, buf_in, mk, mv, prompt, p):
        self.B, self.L, self.pos0, self.S_in, self.buf_in, self.mk, self.mv = B, L, pos0, S_in, buf_in, mk, mv
        self.prompt = prompt
        self.act = BF16 if prompt else F32
        self.x = x
        self.hn = _rmsnorm(x, p["norm_mix_pre"], 0)
        self.S_out, self.buf_out = [], []


def _mixer(g, l, p, decode=None):
    j = l // 2
    a_dec = None
    if l % 2 == 0:
        w, wl = _weight(p, "w_in_a", j)
        q, k, lf, v, gate, cast = _proj_a(g.hn, w, wl, p["hg_lb_logits"], l, g.act)
        _hand_off(p, "w_in_a", j, cast)
        if g.prompt:
            o, S, a_dec = _gla_prompt(q, k, lf, v, gate, p["hg_norm"], j, g.B, g.L, decode)
        else:
            o, S = _gla_step(q, k, lf, v, gate, g.S_in, p["hg_norm"], j, g.B, g.L)
        g.S_out.append(S)
        g.x, g.hn = _mm_res(o, p["w_out_a"], j, g.x, p["norm_mix_post"], p["norm_x_pre"], l)
    else:
        u = _project(g, p, "w_in_b", j, F32)
        bb, tl = (1, 512) if g.prompt else (16, g.L)
        mixed, nb = _pool(u, g.buf_in, g.pos0, p["pool_w"], p["pool_scale"], j, g.B, g.L, bb, tl, g.act)
        g.buf_out.append(nb[:, 1:, :])
        g.x, g.hn = _mm_res(mixed, p["w_out_b"], j, g.x, p["norm_mix_post"], p["norm_x_pre"], l)
    return a_dec


def _weight(p, name, idx):
    cast = p["cast"].get((name, idx))
    return (cast, 0) if cast is not None else (p[name], idx)


def _hand_off(p, name, idx, cast):
    if cast is not None:
        p["cast"][(name, idx)] = cast


def _project(g, p, name, idx, dtype):
    w, wl = _weight(p, name, idx)
    outs = _matmul(g.hn, w, wl, [dtype])
    _hand_off(p, name, idx, outs[1] if len(outs) > 1 else None)
    return outs[0]


def _after_attention(g, l, a, p):
    g.x, g.hn = _mm_res(a, p["w_xo"], l, g.x, p["norm_x_post"], p["norm_mlp_pre"], l)
    (wu, wl), (wd, _) = _weight(p, "w_up", l), _weight(p, "w_down", l)
    g.x, g.hn, cast = _mlp(g.hn, wu, wd, wl, g.x, p["norm_mlp_post"], p["norm_mix_pre"], l)
    if cast is not None:
        _hand_off(p, "w_up", l, cast[0])
        _hand_off(p, "w_down", l, cast[1])


def _trunks(gp, gs, p):
    depth = p["w_xq"].shape[0]
    heads = p["mem_heads"]
    for l in range(depth):
        _mixer(gs, l, p)
        qs = _project(gs, p, "w_xq", l, gs.act)
        ride = l % 2 == 0
        a_s = _mixer(gp, l, p, (qs, gs.mk, gs.mv, l, gs.B, gs.L, heads, gs.act) if ride else None)
        if not ride:
            a_s = _attn_decode(qs, gs.mk, gs.mv, l, gs.B, gs.L, heads, 4, gs.act)
        _after_attention(gs, l, a_s, p)
        qp = _project(gp, p, "w_xq", l, gp.act)
        _after_attention(gp, l, _attn(qp, gp.mk, gp.mv, l, gp.B, gp.L, heads, 1, 512, gp.act), p)


def kernel(x_prompt, x_sample, state_hgrn, state_pool, cache_mem_k, cache_mem_v, mem_prompt, w_in_a, hg_lb_logits, hg_norm, w_out_a, w_in_b, pool_w, pool_scale, w_out_b, norm_mem, w_xq, w_xkv, w_xo, norm_mix_pre, norm_mix_post, norm_x_pre, norm_x_post, norm_mlp_pre, norm_mlp_post, w_up, w_down):
    B, L, D = x_prompt.shape
    Bs, Ls, _ = x_sample.shape
    depth = w_xq.shape[0]
    n_mem, heads = cache_mem_k.shape[2], cache_mem_k.shape[3]
    assert L % GLA_BLOCK == 0 and Ls <= GLA_SUB and D % (HEAD_DIM * len(POOL_WINDOWS)) == 0
    assert state_pool.shape[2] == HALO - 1
    bf = lambda a: a.astype(BF16)
    rows = lambda a: a.reshape(a.shape[0], 1, a.shape[1])
    p = dict(w_in_a=w_in_a, hg_lb_logits=hg_lb_logits, hg_norm=rows(hg_norm), w_out_a=bf(w_out_a),
             w_in_b=w_in_b, pool_w=bf(pool_w), pool_scale=rows(pool_scale), w_out_b=bf(w_out_b),
             w_xq=w_xq, w_xo=bf(w_xo), norm_mix_pre=rows(norm_mix_pre), norm_mix_post=rows(norm_mix_post),
             norm_x_pre=rows(norm_x_pre), norm_x_post=rows(norm_x_post), norm_mlp_pre=rows(norm_mlp_pre),
             norm_mlp_post=rows(norm_mlp_post), w_up=w_up, w_down=w_down, mem_heads=heads, cast={})
    R = D // LANES
    nchunk = R // heads

    def cache_view(flat, nb):
        c = flat.reshape(depth, nb, n_mem, nchunk, heads, LANES)
        return jnp.swapaxes(c, 3, 4).reshape(depth, nb, n_mem, heads, D // heads)

    def flat_view(cache):
        nb = cache.shape[1]
        c = cache.reshape(depth, nb, n_mem, heads, nchunk, LANES)
        return jnp.swapaxes(c, 3, 4).reshape(depth, nb, n_mem, R // SUBL, SUBL, LANES)

    mem2d = mem_prompt.reshape(B * n_mem, D)
    k_flat, k_bf = _mem_proj(mem2d, rows(norm_mem), w_xkv, 0, heads)
    v_flat, v_bf = _mem_proj(mem2d, rows(norm_mem), w_xkv, 1, heads)
    cache_mem_k_prompt = cache_view(k_flat, B)
    cache_mem_v_prompt = cache_view(v_flat, B)
    mk_bf = k_bf.reshape(depth, B, n_mem, D)
    mv_bf = v_bf.reshape(depth, B, n_mem, D)

    n_b = state_pool.shape[0]
    buf0 = jnp.zeros((n_b, B, HALO, D), F32)
    buf_s = jnp.pad(state_pool, ((0, 0), (0, 0), (1, 0), (0, 0)))
    gp = _Group(x_prompt.reshape(B * L, D), B, L, 0, None, buf0, mk_bf, mv_bf, True, p)
    gs = _Group(x_sample.reshape(Bs * Ls, D), Bs, Ls, PAST_LEN, state_hgrn, buf_s,
                flat_view(cache_mem_k), flat_view(cache_mem_v), False, p)
    _trunks(gp, gs, p)
    return (gp.x.reshape(B, L, D), gs.x.reshape(Bs, Ls, D), jnp.stack(gp.S_out), jnp.stack(gp.buf_out),
            cache_mem_k_prompt, cache_mem_v_prompt, jnp.stack(gs.S_out), jnp.stack(gs.buf_out))
```

```python
import functools

import numpy as np

import jax
import jax.numpy as jnp
from jax import lax
from jax.experimental import pallas as pl
from jax.experimental.pallas import tpu as pltpu

F32 = jnp.float32
BF16 = jnp.bfloat16
EPS = 1e-6
PAST_LEN = 16384
POOL_WINDOWS = (2, 4, 8, 16)
LANES = 128
SUBL = 8
HEAD_DIM = 128
GLA_BLOCK = 128
GLA_SUB = 16
HALO = 16
MM_RES_SPLIT = 4
SIDE_EVERY = 4

V7X_VMEM_BYTES = 64 * 1024 * 1024
VMEM_CAP = V7X_VMEM_BYTES - 6 * 1024 * 1024


def _cparams(sem, vmem_bytes):
    return pltpu.CompilerParams(dimension_semantics=sem,
                                vmem_limit_bytes=int(min(max(vmem_bytes, 16 * 1024 * 1024), VMEM_CAP)))


def _nbytes(shape, dtype):
    n = 1
    for s in shape:
        n *= s
    return n * jnp.dtype(dtype).itemsize


def _rms(v, w):
    ms = jnp.mean(v * v, axis=-1, keepdims=True)
    return v * lax.rsqrt(ms + EPS) * w


def _sigmoid(x):
    return 1.0 / (1.0 + jnp.exp(-x))


def _dot(a, b):
    return jnp.dot(a, b, preferred_element_type=F32)


def _dot_nt(a, b):
    return lax.dot_general(a, b, (((1,), (1,)), ((), ())), preferred_element_type=F32)


def _dot_tn(a, b, precision=None):
    return lax.dot_general(a, b, (((0,), (0,)), ((), ())), preferred_element_type=F32,
                           precision=precision)


def _rmsnorm_kernel(x_ref, w_ref, o_ref):
    o_ref[...] = _rms(x_ref[...], w_ref[...]).astype(o_ref.dtype)


def _row(layer):
    return lambda width: pl.BlockSpec((None, 1, width), lambda *_: (layer, 0, 0))


def _rmsnorm(x, w, layer, tm=512):
    M, D = x.shape
    tm = min(tm, M)
    return pl.pallas_call(
        _rmsnorm_kernel,
        grid=(M // tm,),
        in_specs=[pl.BlockSpec((tm, D), lambda i: (i, 0)), _row(layer)(D)],
        out_specs=pl.BlockSpec((tm, D), lambda i: (i, 0)),
        out_shape=jax.ShapeDtypeStruct((M, D), BF16),
        compiler_params=_cparams(("parallel",), 4 * _nbytes((tm, D), F32)),
        name="rmsnorm",
    )(x, w)


def _mm_kernel(a_ref, w_ref, *o_refs, hand_off):
    w = w_ref[...].astype(BF16)
    if hand_off:
        o_refs[-1][...] = w
        o_refs = o_refs[:-1]
    acc = _dot(a_ref[...], w)
    for o_ref in o_refs:
        o_ref[...] = acc.astype(o_ref.dtype)


def _matmul(a, w, layer, out_dtypes, tm=1024, tn=1024):
    M, K = a.shape
    N = w.shape[2]
    tm, tn = min(tm, M), min(tn, N)
    hand_off = w.dtype != BF16
    assert not hand_off or M == tm
    vm = 2 * (_nbytes((tm, K), BF16) + _nbytes((K, tn), w.dtype)) + _nbytes((tm, tn), F32)
    vm += sum(2 * _nbytes((tm, tn), d) for d in out_dtypes)
    out_specs = [pl.BlockSpec((tm, tn), lambda i, j: (i, j)) for _ in out_dtypes]
    out_shape = [jax.ShapeDtypeStruct((M, N), d) for d in out_dtypes]
    if hand_off:
        out_specs.append(pl.BlockSpec((None, K, tn), lambda i, j: (0, 0, j)))
        out_shape.append(jax.ShapeDtypeStruct((1, K, N), BF16))
        vm += 3 * _nbytes((K, tn), BF16)
    outs = pl.pallas_call(
        functools.partial(_mm_kernel, hand_off=hand_off),
        grid=(M // tm, N // tn),
        in_specs=[pl.BlockSpec((tm, K), lambda i, j: (i, 0)),
                  pl.BlockSpec((None, K, tn), lambda i, j: (layer, 0, j))],
        out_specs=out_specs,
        out_shape=out_shape,
        compiler_params=_cparams(("parallel", "parallel"), vm),
        name="matmul",
    )(a, w)
    return outs


def _proj_a_kernel(h_ref, wq_ref, wf_ref, wi_ref, wg_ref, lbl_ref, q_ref, k_ref, lf_ref, v_ref, g_ref,
                   *w_out_refs, layer):
    h = h_ref[...]
    ws = [r[...].astype(BF16) for r in (wq_ref, wf_ref, wi_ref, wg_ref)]
    for r, w in zip(w_out_refs, ws):
        r[...] = w
    aq = _dot(h, ws[0])
    q_ref[...] = (aq * _sigmoid(aq)).astype(q_ref.dtype)
    lg = lbl_ref[...]
    e = jnp.exp(lg - jnp.max(lg, axis=0, keepdims=True))
    lb = jnp.sum(e[:layer + 1], axis=0, keepdims=True) / jnp.sum(e, axis=0, keepdims=True)
    f = lb + (1.0 - lb) * _sigmoid(_dot(h, ws[1]))
    k_ref[...] = (1.0 - f).astype(k_ref.dtype)
    lf_ref[...] = jnp.log(f)
    v_ref[...] = _dot(h, ws[2]).astype(v_ref.dtype)
    ag = _dot(h, ws[3])
    g_ref[...] = (ag * _sigmoid(ag)).astype(g_ref.dtype)


def _proj_a(h, w, mixer, lb_logits, layer, act_dtype, tm=1024, tn=512):
    M, D = h.shape
    tm = min(tm, M)
    parts = isinstance(w, (tuple, list))
    hand_off = not parts and w.dtype != BF16
    if hand_off:
        tn = min(tn, 256)
        assert M == tm
    nj = D // tn
    nl = lb_logits.shape[0]
    if parts:
        wspec = lambda s: pl.BlockSpec((None, D, tn), lambda i, j: (0, 0, j))
        w_args, w_dtype = list(w), BF16
    else:
        wspec = lambda s: pl.BlockSpec((None, D, tn), lambda i, j: (mixer, 0, j + s * nj))
        w_args, w_dtype = [w] * 4, w.dtype
    ospec = pl.BlockSpec((tm, tn), lambda i, j: (i, j))
    vm = 2 * (_nbytes((tm, D), BF16) + 4 * _nbytes((D, tn), w_dtype) + 5 * _nbytes((tm, tn), F32))
    vm += 4 * _nbytes((tm, tn), F32)
    out_specs = [ospec] * 5
    out_shape = [jax.ShapeDtypeStruct((M, D), act_dtype), jax.ShapeDtypeStruct((M, D), act_dtype),
                 jax.ShapeDtypeStruct((M, D), F32), jax.ShapeDtypeStruct((M, D), act_dtype),
                 jax.ShapeDtypeStruct((M, D), act_dtype)]
    if hand_off:
        out_specs += [pl.BlockSpec((None, D, tn), lambda i, j: (0, 0, j))] * 4
        out_shape += [jax.ShapeDtypeStruct((1, D, D), BF16)] * 4
        vm += 12 * _nbytes((D, tn), BF16)
    outs = pl.pallas_call(
        functools.partial(_proj_a_kernel, layer=layer),
        grid=(M // tm, nj),
        in_specs=[pl.BlockSpec((tm, D), lambda i, j: (i, 0)), wspec(0), wspec(1), wspec(2), wspec(3),
                  pl.BlockSpec((nl, tn), lambda i, j: (0, j))],
        out_specs=out_specs,
        out_shape=out_shape,
        compiler_params=_cparams(("parallel", "parallel"), vm),
        name="hgrn2_proj",
    )(h, *w_args, lb_logits)
    return tuple(outs[:5]) + ((tuple(outs[5:]),) if hand_off else (None,))


def _mm_res_kernel(a_ref, w_ref, x_ref, pw_ref, nw_ref, ox_ref, oh_ref):
    tm = a_ref.shape[0]
    nsplit = MM_RES_SPLIT if tm % (8 * MM_RES_SPLIT) == 0 and tm // MM_RES_SPLIT >= 128 else 1
    for p in range(nsplit):
        rows = slice(p * tm // nsplit, (p + 1) * tm // nsplit)
        m = _dot(a_ref[rows, :].astype(BF16), w_ref[...])
        y = x_ref[rows, :] + _rms(m, pw_ref[...])
        ox_ref[rows, :] = y
        oh_ref[rows, :] = _rms(y, nw_ref[...]).astype(oh_ref.dtype)


def _mm_res(a, w, wl, x, post_w, next_w, layer, tm=512):
    M, K = a.shape
    D = w.shape[2]
    tm = min(tm, M)
    vm = 2 * (_nbytes((tm, K), a.dtype) + _nbytes((K, D), BF16) + 2 * _nbytes((tm, D), F32)
              + _nbytes((tm, D), BF16)) + 2 * _nbytes((tm, D), F32)
    return pl.pallas_call(
        _mm_res_kernel,
        grid=(M // tm,),
        in_specs=[pl.BlockSpec((tm, K), lambda i: (i, 0)),
                  pl.BlockSpec((None, K, D), lambda i: (wl, 0, 0)),
                  pl.BlockSpec((tm, D), lambda i: (i, 0)), _row(layer)(D), _row(layer)(D)],
        out_specs=[pl.BlockSpec((tm, D), lambda i: (i, 0)), pl.BlockSpec((tm, D), lambda i: (i, 0))],
        out_shape=[jax.ShapeDtypeStruct((M, D), F32), jax.ShapeDtypeStruct((M, D), BF16)],
        compiler_params=_cparams(("parallel",), vm),
        name="matmul_norm_residual",
    )(a, w, x, post_w, next_w)


def _mlp_kernel(h_ref, wu_ref, wd_ref, x_ref, pw_ref, nw_ref, ox_ref, *rest, emit_next, hand_off):
    acc_ref = rest[-1]
    f_axis = 0 if hand_off else 1
    f = pl.program_id(f_axis)
    i = pl.program_id(1 - f_axis)
    ai = i if hand_off else 0

    @pl.when(f == 0)
    def _():
        acc_ref[ai] = jnp.zeros(acc_ref.shape[1:], F32)

    if hand_off:
        wu_ref_b, wd_ref_b = rest[-3], rest[-2]

        @pl.when(i == 0)
        def _():
            wu_ref_b[...] = wu_ref[...].astype(BF16)
            wd_ref_b[...] = wd_ref[...].astype(BF16)
    else:
        wu_ref_b, wd_ref_b = wu_ref, wd_ref
    u = jnp.maximum(_dot(h_ref[...], wu_ref_b[...]), 0.0)
    acc_ref[ai] += _dot((u * u).astype(BF16), wd_ref_b[...])

    @pl.when(f == pl.num_programs(f_axis) - 1)
    def _():
        y = x_ref[...] + _rms(acc_ref[ai], pw_ref[...])
        ox_ref[...] = y
        if emit_next:
            rest[0][...] = _rms(y, nw_ref[...]).astype(BF16)


def _mlp(h, w_up, w_down, wl, x, post_w, next_w, layer, tm=512, tf=1024):
    M, D = h.shape
    FF = w_up.shape[2]
    tm = min(tm, M)
    hand_off = w_up.dtype != BF16
    if hand_off:
        tf = min(tf, 512)
    emit_next = layer + 1 < next_w.shape[0]
    nt, nf = M // tm, FF // tf
    if hand_off:
        ix = lambda fn: (lambda f, i: fn(i, f))
        last = lambda i, f: (jnp.where(f == nf - 1, i, 0), 0)
        grid, sem, n_acc = (nf, nt), ("arbitrary", "arbitrary"), nt
    else:
        ix = lambda fn: fn
        last = lambda i, f: (i, 0)
        grid, sem, n_acc = (nt, nf), ("parallel", "arbitrary"), 1
    tile = lambda: pl.BlockSpec((tm, D), ix(last))
    out_specs = [tile()] + ([tile()] if emit_next else [])
    out_shape = [jax.ShapeDtypeStruct((M, D), F32)] + ([jax.ShapeDtypeStruct((M, D), BF16)] if emit_next else [])
    vm = 2 * (2 * _nbytes((tm, D), BF16) + 2 * _nbytes((D, tf), w_up.dtype) + 2 * _nbytes((tm, D), F32))
    vm += (1 + n_acc) * _nbytes((tm, D), F32) + 2 * _nbytes((tm, tf), F32)
    if hand_off:
        out_specs += [pl.BlockSpec((None, D, tf), ix(lambda i, f: (0, 0, f))),
                      pl.BlockSpec((None, tf, D), ix(lambda i, f: (0, f, 0)))]
        out_shape += [jax.ShapeDtypeStruct((1, D, FF), BF16), jax.ShapeDtypeStruct((1, FF, D), BF16)]
        vm += 4 * _nbytes((D, tf), BF16)
    outs = pl.pallas_call(
        functools.partial(_mlp_kernel, emit_next=emit_next, hand_off=hand_off),
        grid=grid,
        in_specs=[pl.BlockSpec((tm, D), ix(lambda i, f: (i, 0))),
                  pl.BlockSpec((None, D, tf), ix(lambda i, f: (wl, 0, f))),
                  pl.BlockSpec((None, tf, D), ix(lambda i, f: (wl, f, 0))),
                  tile(), _row(layer)(D), _row(layer + 1 if emit_next else layer)(D)],
        out_specs=out_specs,
        out_shape=out_shape,
        scratch_shapes=[pltpu.VMEM((n_acc, tm, D), F32)],
        compiler_params=_cparams(sem, vm),
        name="relu2_mlp",
    )(h, w_up, w_down, x, post_w, next_w)
    n = 2 if emit_next else 1
    return outs[0], (outs[1] if emit_next else None), (tuple(outs[n:]) if hand_off else None)


def _split3(x):
    hi = x.astype(BF16)
    r1 = x - hi.astype(F32)
    mid = r1.astype(BF16)
    lo = (r1 - mid.astype(F32)).astype(BF16)
    return hi, mid, lo


def _decay_rows(d, rows):
    hi, mid, lo = _split3(d)
    r = lax.broadcasted_iota(jnp.int32, (rows, d.shape[-1]), 0)
    return jnp.where(r == 0, hi.astype(F32), jnp.where(r == 1, mid.astype(F32),
                                                       jnp.where(r == 2, lo.astype(F32), 0.0)))


def _gla_prompt_kernel(*refs, decode_heads=None):
    if decode_heads is None:
        (q_ref, k_ref, lf_ref, v_ref, g_ref, nw_ref, sel_ref, lvl_ref, o_ref, sout_ref,
         s_scr, qs_scr, ks_scr, ke_scr, fac_scr, a_scr) = refs
    else:
        (q_ref, k_ref, lf_ref, v_ref, g_ref, nw_ref, sel_ref, lvl_ref, dq_ref, dk_ref, dv_ref,
         o_ref, sout_ref, do_ref, s_scr, qs_scr, ks_scr, ke_scr, fac_scr, a_scr) = refs
    T = q_ref.shape[1]
    D = q_ref.shape[2]
    H = D // HEAD_DIM
    SUB = GLA_SUB
    nb = T // SUB
    c_idx = pl.program_id(1)

    @pl.when(c_idx == 0)
    def _():
        s_scr[...] = jnp.zeros_like(s_scr)

    side = iter(()) if decode_heads is None else _attn_decode_pieces(dq_ref, dk_ref, dv_ref, do_ref,
                                                                     heads=decode_heads)
    stage = [0]

    def side_step():
        stage[0] += 1
        if stage[0] % SIDE_EVERY == 0:
            next(side, None)

    chunks = []
    c = 2 * SUB
    while c < T:
        chunks.append(c)
        c *= 2
    nf = 2 * len(chunks) + 3
    sums = _dot(sel_ref[...], jnp.concatenate(_split3(lf_ref[0]), axis=0))
    fac_scr[...] = jnp.exp(sums[T:])
    nq = len(chunks) + 1
    for i in range(nb):
        rows = slice(i * SUB, (i + 1) * SUB)
        r = sums[rows]
        q16 = q_ref[0, rows, :].astype(F32) * jnp.exp(r)
        kd = k_ref[0, rows, :].astype(F32) * jnp.exp(-r)
        f_row = lambda n: fac_scr[n * nb + i:n * nb + i + 1, :]
        qs_scr[0, rows, :] = (q16 * f_row(0)).astype(BF16)
        qs_scr[1, rows, :] = q16.astype(BF16)
        ks_scr[0, rows, :] = kd.astype(BF16)
        ks_scr[1, rows, :] = (kd * f_row(nq)).astype(BF16)
        for n in range(len(chunks)):
            qs_scr[n + 2, rows, :] = (q16 * f_row(n + 1)).astype(BF16)
            ks_scr[n + 2, rows, :] = (kd * f_row(nq + n + 1)).astype(BF16)
        ke_scr[rows, :] = (kd * f_row(nf - 1)).astype(BF16)
        side_step()
    ke_scr[T:, :] = _decay_rows(fac_scr[(nf - 1) * nb:(nf - 1) * nb + 1, :], SUB).astype(BF16)

    rr = lax.broadcasted_iota(jnp.int32, (SUB, 2 * HEAD_DIM), 0)
    cc = lax.broadcasted_iota(jnp.int32, (SUB, 2 * HEAD_DIM), 1)
    pick = jnp.where((rr < 3) & (cc >= HEAD_DIM), 1.0, 0.0).astype(BF16)
    zeros = jnp.zeros((T, HEAD_DIM), BF16)
    nw = nw_ref[...]

    lvl = lvl_ref[...]
    for h in range(H):
        hs = slice(h * HEAD_DIM, (h + 1) * HEAD_DIM)
        r = _dot_nt(qs_scr[1, :, hs], ks_scr[0:2, :, hs].reshape(2 * T, HEAD_DIM))
        a = jnp.where(lvl == 1, r[:, :T], jnp.where(lvl == 2, r[:, T:], 0.0))
        for n in range(len(chunks)):
            a = jnp.where(lvl == n + 3, _dot_nt(qs_scr[n + 2, :, hs], ks_scr[n + 2, :, hs]), a)
        a_scr[h] = a.astype(BF16)
        side_step()
    for h in range(H):
        hs = slice(h * HEAD_DIM, (h + 1) * HEAD_DIM)
        vh = v_ref[0, :, hs]
        s_old = s_scr[h]
        o = _dot(jnp.concatenate([a_scr[h], qs_scr[0, :, hs]], axis=1),
                 jnp.concatenate([vh, s_old.astype(BF16)], axis=0))
        rhs = jnp.concatenate([jnp.concatenate([vh, zeros], axis=1), pick], axis=0)
        upd = _dot_tn(ke_scr[:, hs], rhs)
        s_scr[h] = s_old * upd[:, HEAD_DIM:] + upd[:, :HEAD_DIM]
        o_ref[0, :, hs] = (_rms(o, nw) * g_ref[0, :, hs].astype(F32)).astype(o_ref.dtype)
        side_step()
    for _ in side:
        pass

    @pl.when(c_idx == pl.num_programs(1) - 1)
    def _():
        sout_ref[0] = s_scr[...]


def _gla_range_matrix(T, SUB):
    chunks = []
    c = 2 * SUB
    while c < T:
        chunks.append(c)
        c *= 2
    nb = T // SUB
    t = np.arange(T)
    rows = [((t[None, :] <= t[:, None]) & (t[None, :] // SUB == t[:, None] // SUB))]
    start = np.arange(nb)[:, None] * SUB
    for C in [T] + chunks:
        rows.append((t[None, :] >= (start // C) * C) & (t[None, :] < start))
    for C in [SUB] + chunks + [T]:
        rows.append((t[None, :] >= start) & (t[None, :] < (start // C + 1) * C))
    sel = np.concatenate(rows, axis=0).astype(np.float32)
    row, col = t[:, None], t[None, :]
    lvl = np.where((row // SUB == col // SUB) & (col <= row), 1, 0)
    for j, C in enumerate([SUB] + chunks):
        lvl = np.where(((row // C) % 2 == 1) & (col // C == row // C - 1), 2 + j, lvl)
    return sel, lvl.astype(np.int32), len(chunks)


def _gla_prompt(q, k, lf, v, g, norm_w, mixer, B, L, decode=None):
    D = q.shape[-1]
    H = D // HEAD_DIM
    T = GLA_BLOCK
    assert T == HEAD_DIM
    nc = L // T
    sel, lvl, nchunks = _gla_range_matrix(T, GLA_SUB)
    sel3 = jnp.asarray(np.concatenate([sel] * 3, axis=1), BF16)
    nfac = sel.shape[0] - T
    r3 = lambda a: a.reshape(B, L, D)
    blk = lambda: pl.BlockSpec((1, T, D), lambda b, c: (b, c, 0))
    const = lambda a: pl.BlockSpec(a.shape, lambda b, c: (0, 0))
    vm = 2 * (5 * _nbytes((T, D), BF16) + _nbytes((T, D), F32)) + 3 * _nbytes((H, HEAD_DIM, HEAD_DIM), F32)
    vm += (2 * nchunks + 6) * _nbytes((T, D), BF16) + 8 * _nbytes((T, D), F32)
    in_specs = [blk(), blk(), blk(), blk(), blk(), _row(mixer)(HEAD_DIM), const(sel3), const(lvl)]
    args = [r3(q), r3(k), r3(lf), r3(v), r3(g), norm_w, sel3, jnp.asarray(lvl)]
    out_specs = [blk(), pl.BlockSpec((1, H, HEAD_DIM, HEAD_DIM), lambda b, c: (b, 0, 0, 0))]
    out_shape = [jax.ShapeDtypeStruct((B, L, D), BF16), jax.ShapeDtypeStruct((B, H, HEAD_DIM, HEAD_DIM), F32)]
    heads = None
    if decode is not None:
        dq, ck, cv, layer, Bd, Ld, heads, d_dtype = decode
        assert Ld == SUBL and SUBL % heads == 0 and Bd % (B * nc) == 0
        bb = Bd // (B * nc)
        kv_shape = (bb,) + ck.shape[2:]
        kv_spec = lambda: pl.BlockSpec((None,) + kv_shape, lambda b, c: (layer, b * nc + c, 0, 0, 0, 0))
        dblk = lambda: pl.BlockSpec((bb, Ld, D), lambda b, c: (b * nc + c, 0, 0))
        in_specs += [dblk(), kv_spec(), kv_spec()]
        args += [dq.reshape(Bd, Ld, D), ck, cv]
        out_specs.append(dblk())
        out_shape.append(jax.ShapeDtypeStruct((Bd, Ld, D), d_dtype))
        vm += 4 * _nbytes((bb, Ld, D), F32) + 4 * _nbytes(kv_shape, ck.dtype)
        vm += 4 * _nbytes(kv_shape[1:], BF16) + 6 * _nbytes((SUBL * Ld, ck.shape[2] * SUBL), F32)
    outs = pl.pallas_call(
        functools.partial(_gla_prompt_kernel, decode_heads=heads),
        grid=(B, nc),
        in_specs=in_specs,
        out_specs=out_specs,
        out_shape=out_shape,
        scratch_shapes=[pltpu.VMEM((H, HEAD_DIM, HEAD_DIM), F32),
                        pltpu.VMEM((nchunks + 2, T, D), BF16),
                        pltpu.VMEM((nchunks + 2, T, D), BF16),
                        pltpu.VMEM((T + GLA_SUB, D), BF16),
                        pltpu.VMEM((nfac, D), F32),
                        pltpu.VMEM((H, T, T), BF16)],
        compiler_params=_cparams(("parallel", "arbitrary"), vm),
        name="hgrn2_recurrence_prompt",
    )(*args)
    a_dec = outs[2].reshape(Bd * Ld, D) if decode is not None else None
    return outs[0].reshape(B * L, D), outs[1], a_dec


def _gla_step_kernel(q_ref, k_ref, lf_ref, v_ref, g_ref, s0_ref, nw_ref, o_ref, sout_ref,
                     qs_scr, ks_scr, ke_scr, ds_scr):
    bb, T, D = q_ref.shape
    H = D // HEAD_DIM
    R = bb * T
    pos = lax.broadcasted_iota(jnp.int32, (bb, T, D), 1)
    b = lf_ref[...]
    sh = 1
    while sh < T:
        b = b + jnp.where(pos >= sh, pltpu.roll(b, sh, 1), 0.0)
        sh *= 2
    b_last = b[:, T - 1:T, :]
    k = k_ref[...]
    qs_scr[...] = q_ref[...] * jnp.exp(b)
    ks_scr[...] = k * jnp.exp(-b)
    ke_scr[...] = k * jnp.exp(b_last - b)
    d_hi, d_mid, d_lo = (t.astype(F32) for t in _split3(jnp.exp(b_last)))
    ds_scr[...] = jnp.where(pos == 0, d_hi, jnp.where(pos == 1, d_mid, jnp.where(pos == 2, d_lo, 0.0)))

    row = lax.broadcasted_iota(jnp.int32, (R, R), 0)
    col = lax.broadcasted_iota(jnp.int32, (R, R), 1)
    same_seq_causal = (row // T == col // T) & (col <= row)
    rr = lax.broadcasted_iota(jnp.int32, (T, 2 * HEAD_DIM), 0)
    cc = lax.broadcasted_iota(jnp.int32, (T, 2 * HEAD_DIM), 1)
    pick = jnp.where((rr < 3) & (cc >= HEAD_DIM), 1.0, 0.0)
    zeros = jnp.zeros((T, HEAD_DIM), F32)
    nw = nw_ref[...]

    for h in range(H):
        hs = slice(h * HEAD_DIM, (h + 1) * HEAD_DIM)
        qd = qs_scr[:, :, hs].reshape(R, HEAD_DIM)
        kd = ks_scr[:, :, hs].reshape(R, HEAD_DIM)
        vh = v_ref[:, :, hs].reshape(R, HEAD_DIM)
        a = jnp.where(same_seq_causal, _dot_nt(qd.astype(BF16), kd.astype(BF16)), 0.0)
        o_intra = _dot(a.astype(BF16), vh.astype(BF16))
        outs = []
        for i in range(bb):
            s_old = s0_ref[i, h]
            outs.append(o_intra[i * T:(i + 1) * T]
                        + _dot(qd[i * T:(i + 1) * T].astype(BF16), s_old.astype(BF16)))
            lhs = jnp.concatenate([ke_scr[i, :, hs], ds_scr[i, :, hs]], axis=0).astype(BF16)
            rhs = jnp.concatenate([jnp.concatenate([vh[i * T:(i + 1) * T], zeros], axis=1), pick],
                                  axis=0).astype(BF16)
            upd = _dot_tn(lhs, rhs)
            sout_ref[i, h] = s_old * upd[:, HEAD_DIM:] + upd[:, :HEAD_DIM]
        o = jnp.concatenate(outs, axis=0)
        on = _rms(o, nw) * g_ref[:, :, hs].reshape(R, HEAD_DIM)
        o_ref[:, :, hs] = on.reshape(bb, T, HEAD_DIM).astype(o_ref.dtype)


def _gla_step(q, k, lf, v, g, s0, norm_w, mixer, B, L, bb=8):
    D = q.shape[-1]
    H = D // HEAD_DIM
    assert L == 8 and B % bb == 0
    r3 = lambda a: a.reshape(B, L, D)
    blk = lambda: pl.BlockSpec((bb, L, D), lambda b: (b, 0, 0))
    sblk = lambda: pl.BlockSpec((bb, H, HEAD_DIM, HEAD_DIM), lambda b: (b, 0, 0, 0))
    vm = 2 * (6 * _nbytes((bb, L, D), F32) + 2 * _nbytes((bb, H, HEAD_DIM, HEAD_DIM), F32))
    vm += 12 * _nbytes((bb, L, D), F32)
    o, s = pl.pallas_call(
        _gla_step_kernel,
        grid=(B // bb,),
        in_specs=[blk(), blk(), blk(), blk(), blk(),
                  pl.BlockSpec((None, bb, H, HEAD_DIM, HEAD_DIM), lambda b: (mixer, b, 0, 0, 0)),
                  _row(mixer)(HEAD_DIM)],
        out_specs=[blk(), sblk()],
        out_shape=[jax.ShapeDtypeStruct((B, L, D), F32),
                   jax.ShapeDtypeStruct((B, H, HEAD_DIM, HEAD_DIM), F32)],
        scratch_shapes=[pltpu.VMEM((bb, L, D), F32)] * 4,
        compiler_params=_cparams(("parallel",), vm),
        name="hgrn2_recurrence_step",
    )(r3(q), r3(k), r3(lf), r3(v), r3(g), s0, norm_w)
    return o.reshape(B * L, D), s


def _pool_kernel(*refs, pos0, has_halo):
    if has_halo:
        u_ref, halo_ref, buf_ref, pw_ref, ps_ref, o_ref, nb_ref, ext = refs
    else:
        u_ref, buf_ref, pw_ref, ps_ref, o_ref, nb_ref, ext = refs
    bb, tl, D = u_ref.shape
    G = len(POOL_WINDOWS)
    gc = D // G
    l_idx = pl.program_id(1)
    ext[:, HALO:, :] = u_ref[...]
    if has_halo:
        @pl.when(l_idx == 0)
        def _():
            ext[:, :HALO, :] = buf_ref[...]

        @pl.when(l_idx > 0)
        def _():
            ext[:, :HALO, :] = halo_ref[...]
    else:
        ext[:, :HALO, :] = buf_ref[...]

    pos = pos0 + l_idx * tl + lax.broadcasted_iota(jnp.int32, (tl, gc), 0)
    for gi, w in enumerate(POOL_WINDOWS):
        ls = slice(gi * gc, (gi + 1) * gc)
        cnt = jnp.minimum(w, pos + 1).astype(F32)
        rows = []
        for i in range(bb):
            u = ext[i, HALO:HALO + tl, ls]
            s = u
            for d in range(1, w):
                s = s + ext[i, HALO - d:HALO - d + tl, ls]
            rows.append(s / cnt - u)
        pooled = rows[0] if bb == 1 else jnp.concatenate(rows, axis=0)
        mixed = (_dot(pooled.astype(BF16), pw_ref[gi]) * ps_ref[:, ls]).astype(o_ref.dtype)
        for i in range(bb):
            o_ref[i, :, ls] = mixed[i * tl:(i + 1) * tl]

    @pl.when(l_idx == pl.num_programs(1) - 1)
    def _():
        nb_ref[...] = ext[:, tl:tl + HALO, :]


def _pool(u, buf, pos0, pool_w, pool_scale, mixer, B, L, bb, tl, out_dtype):
    D = u.shape[-1]
    G = len(POOL_WINDOWS)
    gc = D // G
    tl = min(tl, L)
    u3 = u.reshape(B, L, D)
    has_halo = L > tl
    step = tl // HALO
    in_specs = [pl.BlockSpec((bb, tl, D), lambda b, l: (b, l, 0))]
    args = [u3]
    if has_halo:
        in_specs.append(pl.BlockSpec((bb, HALO, D), lambda b, l: (b, jnp.maximum(l * step - 1, 0), 0)))
        args.append(u3)
    in_specs += [pl.BlockSpec((None, bb, HALO, D), lambda b, l: (mixer, b, 0, 0)),
                 pl.BlockSpec((None, G, gc, gc), lambda b, l: (mixer, 0, 0, 0)),
                 _row(mixer)(D)]
    args += [buf, pool_w, pool_scale]
    vm = 2 * (_nbytes((bb, tl, D), F32) + 3 * _nbytes((bb, HALO, D), F32) + _nbytes((G, gc, gc), BF16)
              + _nbytes((bb, tl, D), BF16)) + _nbytes((bb, tl + HALO, D), F32) + 4 * _nbytes((bb, tl, D), F32)
    mixed, newbuf = pl.pallas_call(
        functools.partial(_pool_kernel, pos0=pos0, has_halo=has_halo),
        grid=(B // bb, L // tl),
        in_specs=in_specs,
        out_specs=[pl.BlockSpec((bb, tl, D), lambda b, l: (b, l, 0)),
                   pl.BlockSpec((bb, HALO, D), lambda b, l: (b, 0, 0))],
        out_shape=[jax.ShapeDtypeStruct((B, L, D), out_dtype), jax.ShapeDtypeStruct((B, HALO, D), F32)],
        scratch_shapes=[pltpu.VMEM((bb, tl + HALO, D), F32)],
        compiler_params=_cparams(("parallel", "arbitrary"), vm),
        name="causal_pool",
    )(*args)
    return mixed.reshape(B * L, D), newbuf


def _attn_kernel(q_ref, k_ref, v_ref, o_ref, *, heads):
    bb, tq, D = q_ref.shape
    hd = D // heads
    scale = hd ** -0.5
    for i in range(bb):
        for h in range(heads):
            hs = slice(h * hd, (h + 1) * hd)
            s = _dot_nt(q_ref[i, :, hs].astype(BF16), k_ref[i, :, hs].astype(BF16)) * scale
            p = jnp.exp(s - jnp.max(s, axis=-1, keepdims=True))
            den = jnp.sum(p, axis=-1, keepdims=True)
            o_ref[i, :, hs] = (_dot(p.astype(BF16), v_ref[i, :, hs].astype(BF16)) / den).astype(o_ref.dtype)


def _attn(q, mk, mv, layer, B, L, heads, bb, tq, out_dtype):
    D = q.shape[-1]
    tq = min(tq, L)
    kv_shape = (bb,) + mk.shape[2:]
    kv_blk = (None,) + kv_shape
    vm = 2 * (_nbytes((bb, tq, D), q.dtype) + 2 * _nbytes(kv_shape, mk.dtype) + _nbytes((bb, tq, D), out_dtype))
    vm += 2 * _nbytes(kv_shape, BF16) + 8 * _nbytes((tq, D // heads), F32)
    o = pl.pallas_call(
        functools.partial(_attn_kernel, heads=heads),
        grid=(B // bb, L // tq),
        in_specs=[pl.BlockSpec((bb, tq, D), lambda b, l: (b, l, 0)),
                  pl.BlockSpec(kv_blk, lambda b, l: (layer, b, 0, 0)),
                  pl.BlockSpec(kv_blk, lambda b, l: (layer, b, 0, 0))],
        out_specs=pl.BlockSpec((bb, tq, D), lambda b, l: (b, l, 0)),
        out_shape=jax.ShapeDtypeStruct((B, L, D), out_dtype),
        compiler_params=_cparams(("parallel", "parallel"), vm),
        name="memory_cross_attention",
    )(q.reshape(B, L, D), mk, mv)
    return o.reshape(B * L, D)


def _attn_decode_pieces(q_ref, k_ref, v_ref, o_ref, *, heads):
    bb, tq, D = q_ref.shape
    n_mem, nu = k_ref.shape[1], k_ref.shape[2]
    hd = D // heads
    per = SUBL // heads
    W = n_mem * SUBL
    scale = hd ** -0.5
    res = lax.broadcasted_iota(jnp.int32, (tq, W), 1) % SUBL
    tile = lambda r: jnp.concatenate([r] * (W // LANES), axis=1)

    def feat(j):
        c, h = divmod(j, heads)
        return slice(h * hd + c * LANES, h * hd + (c + 1) * LANES)

    def same_row_reduce(x, op):
        r = x[:, :LANES]
        for j in range(1, W // LANES):
            r = op(r, x[:, j * LANES:(j + 1) * LANES])
        sh = SUBL
        while sh < LANES:
            r = op(r, pltpu.roll(r, sh, 1))
            sh *= 2
        return r

    for i in range(bb):
        w = None
        for u in range(nu):
            slab = k_ref[i, :, u].reshape(W, LANES).astype(BF16)
            qm = jnp.concatenate([q_ref[i, :, feat(u * SUBL + s)] for s in range(SUBL)], axis=0)
            p = _dot_nt(qm.astype(BF16), slab)
            z = p[:tq]
            for s in range(1, SUBL):
                z = jnp.where(res == s, p[s * tq:(s + 1) * tq], z)
            w = z if w is None else w + z
            yield
        sc = w
        for c in range(1, per):
            sc = sc + pltpu.roll(w, W - c * heads, 1)
        sc = sc * scale
        e = jnp.exp(sc - tile(same_row_reduce(sc, jnp.maximum)))
        pn = e / tile(same_row_reduce(e, jnp.add))
        blocks = []
        for s in range(SUBL):
            c_lo = s // heads
            src = pn if c_lo == 0 else pltpu.roll(pn, c_lo * heads, 1)
            blocks.append(jnp.where(res == s, src, 0.0))
        pm = jnp.concatenate(blocks, axis=0).astype(BF16)
        yield
        for u in range(nu):
            o = _dot(pm, v_ref[i, :, u].reshape(W, LANES).astype(BF16))
            for s in range(SUBL):
                o_ref[i, :, feat(u * SUBL + s)] = o[s * tq:(s + 1) * tq].astype(o_ref.dtype)
            yield


def _attn_decode_kernel(q_ref, k_ref, v_ref, o_ref, *, heads):
    for _ in _attn_decode_pieces(q_ref, k_ref, v_ref, o_ref, heads=heads):
        pass


def _attn_decode(q, ck, cv, layer, B, L, heads, bb, out_dtype):
    D = q.shape[-1]
    assert L == SUBL and SUBL % heads == 0 and D % (SUBL * LANES) == 0
    kv_shape = (bb,) + ck.shape[2:]
    vm = 2 * (2 * _nbytes((bb, L, D), F32) + 2 * _nbytes(kv_shape, ck.dtype))
    vm += 4 * _nbytes(kv_shape[1:], BF16) + 6 * _nbytes((SUBL * L, ck.shape[2] * SUBL), F32)
    o = pl.pallas_call(
        functools.partial(_attn_decode_kernel, heads=heads),
        grid=(B // bb,),
        in_specs=[pl.BlockSpec((bb, L, D), lambda b: (b, 0, 0)),
                  pl.BlockSpec((None,) + kv_shape, lambda b: (layer, b, 0, 0, 0, 0)),
                  pl.BlockSpec((None,) + kv_shape, lambda b: (layer, b, 0, 0, 0, 0))],
        out_specs=pl.BlockSpec((bb, L, D), lambda b: (b, 0, 0)),
        out_shape=jax.ShapeDtypeStruct((B, L, D), out_dtype),
        compiler_params=_cparams(("parallel",), vm),
        name="memory_cross_attention_decode",
    )(q.reshape(B, L, D), ck, cv)
    return o.reshape(B * L, D)


def _mem_proj_kernel(x_ref, nw_ref, w_ref, flat_ref, bf_ref, w_scr, *, heads):
    tm, D = x_ref.shape
    R = D // LANES
    nchunk = R // heads

    @pl.when(pl.program_id(1) == 0)
    def _():
        w_scr[...] = w_ref[...].astype(BF16)

    acc = _dot(_rms(x_ref[...], nw_ref[...]).astype(BF16), w_scr[...])
    bf_ref[...] = acc.astype(bf_ref.dtype)
    for h in range(heads):
        for c in range(nchunk):
            j = h * nchunk + c
            flat_ref[pl.ds(c * heads + h, tm, stride=R), :] = acc[:, j * LANES:(j + 1) * LANES]


def _mem_proj(mem, norm_w, w, half, heads, tm=256):
    M, D = mem.shape
    depth = w.shape[0]
    R = D // LANES
    tm = min(tm, M)
    nt = M // tm
    vm = 2 * (_nbytes((tm, D), F32) + _nbytes((D, D), F32) + _nbytes((tm, D), F32) + _nbytes((tm, D), BF16))
    vm += _nbytes((D, D), BF16) + 2 * _nbytes((tm, D), F32)
    return pl.pallas_call(
        functools.partial(_mem_proj_kernel, heads=heads),
        grid=(depth, nt),
        in_specs=[pl.BlockSpec((tm, D), lambda l, i: (i, 0)),
                  pl.BlockSpec((None, 1, D), lambda l, i: (l, 0, 0)),
                  pl.BlockSpec((None, D, D), lambda l, i: (l, 0, half))],
        out_specs=[pl.BlockSpec((tm * R, LANES), lambda l, i: (l * nt + i, 0)),
                   pl.BlockSpec((None, tm, D), lambda l, i: (l, i, 0))],
        out_shape=[jax.ShapeDtypeStruct((depth * M * R, LANES), F32),
                   jax.ShapeDtypeStruct((depth, M, D), BF16)],
        scratch_shapes=[pltpu.VMEM((D, D), BF16)],
        compiler_params=_cparams(("parallel", "arbitrary"), vm),
        name="memory_kv_projection",
    )(mem, norm_w, w)


class _Group:
    def __init__(self, x, B, L, pos0, S_in, buf_in, mk, mv, prompt, p):
        self.B, self.L, self.pos0, self.S_in, self.buf_in, self.mk, self.mv = B, L, pos0, S_in, buf_in, mk, mv
        self.prompt = prompt
        self.act = BF16 if prompt else F32
        self.x = x
        self.hn = _rmsnorm(x, p["norm_mix_pre"], 0)
        self.S_out, self.buf_out = [], []


def _mixer(g, l, p, decode=None):
    j = l // 2
    a_dec = None
    if l % 2 == 0:
        w, wl = _weight(p, "w_in_a", j)
        q, k, lf, v, gate, cast = _proj_a(g.hn, w, wl, p["hg_lb_logits"], l, g.act)
        _hand_off(p, "w_in_a", j, cast)
        if g.prompt:
            o, S, a_dec = _gla_prompt(q, k, lf, v, gate, p["hg_norm"], j, g.B, g.L, decode)
        else:
            o, S = _gla_step(q, k, lf, v, gate, g.S_in, p["hg_norm"], j, g.B, g.L)
        g.S_out.append(S)
        g.x, g.hn = _mm_res(o, p["w_out_a"], j, g.x, p["norm_mix_post"], p["norm_x_pre"], l)
    else:
        u = _project(g, p, "w_in_b", j, F32)
        bb, tl = (1, 512) if g.prompt else (16, g.L)
        mixed, nb = _pool(u, g.buf_in, g.pos0, p["pool_w"], p["pool_scale"], j, g.B, g.L, bb, tl, g.act)
        g.buf_out.append(nb[:, 1:, :])
        g.x, g.hn = _mm_res(mixed, p["w_out_b"], j, g.x, p["norm_mix_post"], p["norm_x_pre"], l)
    return a_dec


def _weight(p, name, idx):
    cast = p["cast"].get((name, idx))
    return (cast, 0) if cast is not None else (p[name], idx)


def _hand_off(p, name, idx, cast):
    if cast is not None:
        p["cast"][(name, idx)] = cast


def _project(g, p, name, idx, dtype):
    w, wl = _weight(p, name, idx)
    outs = _matmul(g.hn, w, wl, [dtype])
    _hand_off(p, name, idx, outs[1] if len(outs) > 1 else None)
    return outs[0]


def _after_attention(g, l, a, p):
    g.x, g.hn = _mm_res(a, p["w_xo"], l, g.x, p["norm_x_post"], p["norm_mlp_pre"], l)
    (wu, wl), (wd, _) = _weight(p, "w_up", l), _weight(p, "w_down", l)
    g.x, g.hn, cast = _mlp(g.hn, wu, wd, wl, g.x, p["norm_mlp_post"], p["norm_mix_pre"], l)
    if cast is not None:
        _hand_off(p, "w_up", l, cast[0])
        _hand_off(p, "w_down", l, cast[1])


def _trunks(gp, gs, p):
    depth = p["w_xq"].shape[0]
    heads = p["mem_heads"]
    for l in range(depth):
        _mixer(gs, l, p)
        qs = _project(gs, p, "w_xq", l, gs.act)
        ride = l % 2 == 0
        a_s = _mixer(gp, l, p, (qs, gs.mk, gs.mv, l, gs.B, gs.L, heads, gs.act) if ride else None)
        if not ride:
            a_s = _attn_decode(qs, gs.mk, gs.mv, l, gs.B, gs.L, heads, 4, gs.act)
        _after_attention(gs, l, a_s, p)
        qp = _project(gp, p, "w_xq", l, gp.act)
        _after_attention(gp, l, _attn(qp, gp.mk, gp.mv, l, gp.B, gp.L, heads, 1, 512, gp.act), p)


def kernel(x_prompt, x_sample, state_hgrn, state_pool, cache_mem_k, cache_mem_v, mem_prompt, w_in_a, hg_lb_logits, hg_norm, w_out_a, w_in_b, pool_w, pool_scale, w_out_b, norm_mem, w_xq, w_xkv, w_xo, norm_mix_pre, norm_mix_post, norm_x_pre, norm_x_post, norm_mlp_pre, norm_mlp_post, w_up, w_down):
    B, L, D = x_prompt.shape
    Bs, Ls, _ = x_sample.shape
    depth = w_xq.shape[0]
    n_mem, heads = cache_mem_k.shape[2], cache_mem_k.shape[3]
    assert L % GLA_BLOCK == 0 and Ls <= GLA_SUB and D % (HEAD_DIM * len(POOL_WINDOWS)) == 0
    assert state_pool.shape[2] == HALO - 1
    bf = lambda a: a.astype(BF16)
    rows = lambda a: a.reshape(a.shape[0], 1, a.shape[1])
    p = dict(w_in_a=w_in_a, hg_lb_logits=hg_lb_logits, hg_norm=rows(hg_norm), w_out_a=bf(w_out_a),
             w_in_b=w_in_b, pool_w=bf(pool_w), pool_scale=rows(pool_scale), w_out_b=bf(w_out_b),
             w_xq=w_xq, w_xo=bf(w_xo), norm_mix_pre=rows(norm_mix_pre), norm_mix_post=rows(norm_mix_post),
             norm_x_pre=rows(norm_x_pre), norm_x_post=rows(norm_x_post), norm_mlp_pre=rows(norm_mlp_pre),
             norm_mlp_post=rows(norm_mlp_post), w_up=w_up, w_down=w_down, mem_heads=heads, cast={})
    R = D // LANES
    nchunk = R // heads

    def cache_view(flat, nb):
        c = flat.reshape(depth, nb, n_mem, nchunk, heads, LANES)
        return jnp.swapaxes(c, 3, 4).reshape(depth, nb, n_mem, heads, D // heads)

    def flat_view(cache):
        nb = cache.shape[1]
        c = cache.reshape(depth, nb, n_mem, heads, nchunk, LANES)
        return jnp.swapaxes(c, 3, 4).reshape(depth, nb, n_mem, R // SUBL, SUBL, LANES)

    mem2d = mem_prompt.reshape(B * n_mem, D)
    k_flat, k_bf = _mem_proj(mem2d, rows(norm_mem), w_xkv, 0, heads)
    v_flat, v_bf = _mem_proj(mem2d, rows(norm_mem), w_xkv, 1, heads)
    cache_mem_k_prompt = cache_view(k_flat, B)
    cache_mem_v_prompt = cache_view(v_flat, B)
    mk_bf = k_bf.reshape(depth, B, n_mem, D)
    mv_bf = v_bf.reshape(depth, B, n_mem, D)

    n_b = state_pool.shape[0]
    buf0 = jnp.zeros((n_b, B, HALO, D), F32)
    buf_s = jnp.pad(state_pool, ((0, 0), (0, 0), (1, 0), (0, 0)))
    gp = _Group(x_prompt.reshape(B * L, D), B, L, 0, None, buf0, mk_bf, mv_bf, True, p)
    gs = _Group(x_sample.reshape(Bs * Ls, D), Bs, Ls, PAST_LEN, state_hgrn, buf_s,
                flat_view(cache_mem_k), flat_view(cache_mem_v), False, p)
    _trunks(gp, gs, p)
    return (gp.x.reshape(B, L, D), gs.x.reshape(Bs, Ls, D), jnp.stack(gp.S_out), jnp.stack(gp.buf_out),
            cache_mem_k_prompt, cache_mem_v_prompt, jnp.stack(gs.S_out), jnp.stack(gs.buf_out))
```

```python
import functools

import numpy as np

import jax
import jax.numpy as jnp
from jax import lax
from jax.experimental import pallas as pl
from jax.experimental.pallas import tpu as pltpu

F32 = jnp.float32
BF16 = jnp.bfloat16
EPS = 1e-6
PAST_LEN = 16384
POOL_WINDOWS = (2, 4, 8, 16)
LANES = 128
SUBL = 8
HEAD_DIM = 128
GLA_BLOCK = 128
GLA_SUB = 16
HALO = 16
MM_RES_SPLIT = 4
SIDE_EVERY = 4
PROJ_SIDE_PIECES = 3

V7X_VMEM_BYTES = 64 * 1024 * 1024
VMEM_CAP = V7X_VMEM_BYTES - 6 * 1024 * 1024


def _cparams(sem, vmem_bytes):
    return pltpu.CompilerParams(dimension_semantics=sem,
                                vmem_limit_bytes=int(min(max(vmem_bytes, 16 * 1024 * 1024), VMEM_CAP)))


def _nbytes(shape, dtype):
    n = 1
    for s in shape:
        n *= s
    return n * jnp.dtype(dtype).itemsize


def _rms(v, w):
    ms = jnp.mean(v * v, axis=-1, keepdims=True)
    return v * lax.rsqrt(ms + EPS) * w


def _sigmoid(x):
    return 1.0 / (1.0 + jnp.exp(-x))


def _dot(a, b):
    return jnp.dot(a, b, preferred_element_type=F32)


def _dot_nt(a, b):
    return lax.dot_general(a, b, (((1,), (1,)), ((), ())), preferred_element_type=F32)


def _dot_tn(a, b, precision=None):
    return lax.dot_general(a, b, (((0,), (0,)), ((), ())), preferred_element_type=F32,
                           precision=precision)


def _rmsnorm_kernel(x_ref, w_ref, o_ref):
    o_ref[...] = _rms(x_ref[...], w_ref[...]).astype(o_ref.dtype)


def _row(layer):
    return lambda width: pl.BlockSpec((None, 1, width), lambda *_: (layer, 0, 0))


def _rmsnorm(x, w, layer, tm=512):
    M, D = x.shape
    tm = min(tm, M)
    return pl.pallas_call(
        _rmsnorm_kernel,
        grid=(M // tm,),
        in_specs=[pl.BlockSpec((tm, D), lambda i: (i, 0)), _row(layer)(D)],
        out_specs=pl.BlockSpec((tm, D), lambda i: (i, 0)),
        out_shape=jax.ShapeDtypeStruct((M, D), BF16),
        compiler_params=_cparams(("parallel",), 4 * _nbytes((tm, D), F32)),
        name="rmsnorm",
    )(x, w)


def _mm_kernel(a_ref, w_ref, *o_refs, hand_off):
    w = w_ref[...].astype(BF16)
    if hand_off:
        o_refs[-1][...] = w
        o_refs = o_refs[:-1]
    acc = _dot(a_ref[...], w)
    for o_ref in o_refs:
        o_ref[...] = acc.astype(o_ref.dtype)


def _matmul(a, w, layer, out_dtypes, tm=1024, tn=1024):
    M, K = a.shape
    N = w.shape[2]
    tm, tn = min(tm, M), min(tn, N)
    hand_off = w.dtype != BF16
    assert not hand_off or M == tm
    vm = 2 * (_nbytes((tm, K), BF16) + _nbytes((K, tn), w.dtype)) + _nbytes((tm, tn), F32)
    vm += sum(2 * _nbytes((tm, tn), d) for d in out_dtypes)
    out_specs = [pl.BlockSpec((tm, tn), lambda i, j: (i, j)) for _ in out_dtypes]
    out_shape = [jax.ShapeDtypeStruct((M, N), d) for d in out_dtypes]
    if hand_off:
        out_specs.append(pl.BlockSpec((None, K, tn), lambda i, j: (0, 0, j)))
        out_shape.append(jax.ShapeDtypeStruct((1, K, N), BF16))
        vm += 3 * _nbytes((K, tn), BF16)
    outs = pl.pallas_call(
        functools.partial(_mm_kernel, hand_off=hand_off),
        grid=(M // tm, N // tn),
        in_specs=[pl.BlockSpec((tm, K), lambda i, j: (i, 0)),
                  pl.BlockSpec((None, K, tn), lambda i, j: (layer, 0, j))],
        out_specs=out_specs,
        out_shape=out_shape,
        compiler_params=_cparams(("parallel", "parallel"), vm),
        name="matmul",
    )(a, w)
    return outs


def _proj_a_kernel(*refs, layer, n_cast, decode_heads):
    h_ref, wq_ref, wf_ref, wi_ref, wg_ref, lbl_ref = refs[:6]
    n_in = 6 if decode_heads is None else 9
    q_ref, k_ref, lf_ref, v_ref, g_ref = refs[n_in:n_in + 5]
    w_out_refs = refs[n_in + 5:n_in + 5 + n_cast]
    side = iter(()) if decode_heads is None else _attn_decode_pieces(*refs[6:9], refs[-1], heads=decode_heads)

    def side_steps():
        for _ in range(PROJ_SIDE_PIECES):
            next(side, None)

    h = h_ref[...]
    ws = [r[...].astype(BF16) for r in (wq_ref, wf_ref, wi_ref, wg_ref)]
    for r, w in zip(w_out_refs, ws):
        r[...] = w
    aq = _dot(h, ws[0])
    q_ref[...] = (aq * _sigmoid(aq)).astype(q_ref.dtype)
    side_steps()
    lg = lbl_ref[...]
    e = jnp.exp(lg - jnp.max(lg, axis=0, keepdims=True))
    lb = jnp.sum(e[:layer + 1], axis=0, keepdims=True) / jnp.sum(e, axis=0, keepdims=True)
    f = lb + (1.0 - lb) * _sigmoid(_dot(h, ws[1]))
    k_ref[...] = (1.0 - f).astype(k_ref.dtype)
    lf_ref[...] = jnp.log(f)
    side_steps()
    v_ref[...] = _dot(h, ws[2]).astype(v_ref.dtype)
    side_steps()
    ag = _dot(h, ws[3])
    g_ref[...] = (ag * _sigmoid(ag)).astype(g_ref.dtype)
    for _ in side:
        pass


def _proj_a(h, w, mixer, lb_logits, layer, act_dtype, decode=None, tm=1024, tn=512):
    M, D = h.shape
    if decode is not None:
        tm = min(tm, 512)
    tm = min(tm, M)
    parts = isinstance(w, (tuple, list))
    hand_off = not parts and w.dtype != BF16
    if hand_off:
        tn = min(tn, 256)
        assert M == tm
    ni, nj = M // tm, D // tn
    nl = lb_logits.shape[0]
    if parts:
        wspec = lambda s: pl.BlockSpec((None, D, tn), lambda i, j: (0, 0, j))
        w_args, w_dtype = list(w), BF16
    else:
        wspec = lambda s: pl.BlockSpec((None, D, tn), lambda i, j: (mixer, 0, j + s * nj))
        w_args, w_dtype = [w] * 4, w.dtype
    ospec = pl.BlockSpec((tm, tn), lambda i, j: (i, j))
    vm = 2 * (_nbytes((tm, D), BF16) + 4 * _nbytes((D, tn), w_dtype) + 5 * _nbytes((tm, tn), F32))
    vm += 4 * _nbytes((tm, tn), F32)
    in_specs = [pl.BlockSpec((tm, D), lambda i, j: (i, 0)), wspec(0), wspec(1), wspec(2), wspec(3),
                pl.BlockSpec((nl, tn), lambda i, j: (0, j))]
    args = [h, *w_args, lb_logits]
    out_specs = [ospec] * 5
    out_shape = [jax.ShapeDtypeStruct((M, D), act_dtype), jax.ShapeDtypeStruct((M, D), act_dtype),
                 jax.ShapeDtypeStruct((M, D), F32), jax.ShapeDtypeStruct((M, D), act_dtype),
                 jax.ShapeDtypeStruct((M, D), act_dtype)]
    if hand_off:
        out_specs += [pl.BlockSpec((None, D, tn), lambda i, j: (0, 0, j))] * 4
        out_shape += [jax.ShapeDtypeStruct((1, D, D), BF16)] * 4
        vm += 12 * _nbytes((D, tn), BF16)
    heads = None
    if decode is not None:
        dq, ck, cv, dlayer, Bd, Ld, heads, d_dtype = decode
        assert Ld == SUBL and SUBL % heads == 0 and Bd % (ni * nj) == 0
        bb = Bd // (ni * nj)
        kv_shape = (bb,) + ck.shape[2:]
        kv_spec = lambda: pl.BlockSpec((None,) + kv_shape, lambda i, j: (dlayer, i * nj + j, 0, 0, 0, 0))
        dblk = lambda: pl.BlockSpec((bb, Ld, D), lambda i, j: (i * nj + j, 0, 0))
        in_specs += [dblk(), kv_spec(), kv_spec()]
        args += [dq.reshape(Bd, Ld, D), ck, cv]
        out_specs.append(dblk())
        out_shape.append(jax.ShapeDtypeStruct((Bd, Ld, D), d_dtype))
        vm += 4 * _nbytes((bb, Ld, D), F32) + 4 * _nbytes(kv_shape, ck.dtype)
        vm += 4 * _nbytes(kv_shape[1:], BF16) + 6 * _nbytes((SUBL * Ld, ck.shape[2] * SUBL), F32)
    outs = pl.pallas_call(
        functools.partial(_proj_a_kernel, layer=layer, n_cast=4 if hand_off else 0, decode_heads=heads),
        grid=(ni, nj),
        in_specs=in_specs,
        out_specs=out_specs,
        out_shape=out_shape,
        compiler_params=_cparams(("parallel", "parallel"), vm),
        name="hgrn2_proj",
    )(*args)
    cast = tuple(outs[5:9]) if hand_off else None
    a_dec = outs[-1].reshape(Bd * Ld, D) if decode is not None else None
    return tuple(outs[:5]) + (cast, a_dec)


def _mm_res_kernel(a_ref, w_ref, x_ref, pw_ref, nw_ref, ox_ref, oh_ref):
    tm = a_ref.shape[0]
    nsplit = MM_RES_SPLIT if tm % (8 * MM_RES_SPLIT) == 0 and tm // MM_RES_SPLIT >= 128 else 1
    for p in range(nsplit):
        rows = slice(p * tm // nsplit, (p + 1) * tm // nsplit)
        m = _dot(a_ref[rows, :].astype(BF16), w_ref[...])
        y = x_ref[rows, :] + _rms(m, pw_ref[...])
        ox_ref[rows, :] = y
        oh_ref[rows, :] = _rms(y, nw_ref[...]).astype(oh_ref.dtype)


def _mm_res(a, w, wl, x, post_w, next_w, layer, tm=512):
    M, K = a.shape
    D = w.shape[2]
    tm = min(tm, M)
    vm = 2 * (_nbytes((tm, K), a.dtype) + _nbytes((K, D), BF16) + 2 * _nbytes((tm, D), F32)
              + _nbytes((tm, D), BF16)) + 2 * _nbytes((tm, D), F32)
    return pl.pallas_call(
        _mm_res_kernel,
        grid=(M // tm,),
        in_specs=[pl.BlockSpec((tm, K), lambda i: (i, 0)),
                  pl.BlockSpec((None, K, D), lambda i: (wl, 0, 0)),
                  pl.BlockSpec((tm, D), lambda i: (i, 0)), _row(layer)(D), _row(layer)(D)],
        out_specs=[pl.BlockSpec((tm, D), lambda i: (i, 0)), pl.BlockSpec((tm, D), lambda i: (i, 0))],
        out_shape=[jax.ShapeDtypeStruct((M, D), F32), jax.ShapeDtypeStruct((M, D), BF16)],
        compiler_params=_cparams(("parallel",), vm),
        name="matmul_norm_residual",
    )(a, w, x, post_w, next_w)


def _mlp_kernel(h_ref, wu_ref, wd_ref, x_ref, pw_ref, nw_ref, ox_ref, *rest, emit_next, hand_off):
    acc_ref = rest[-1]
    f_axis = 0 if hand_off else 1
    f = pl.program_id(f_axis)
    i = pl.program_id(1 - f_axis)
    ai = i if hand_off else 0

    @pl.when(f == 0)
    def _():
        acc_ref[ai] = jnp.zeros(acc_ref.shape[1:], F32)

    if hand_off:
        wu_ref_b, wd_ref_b = rest[-3], rest[-2]

        @pl.when(i == 0)
        def _():
            wu_ref_b[...] = wu_ref[...].astype(BF16)
            wd_ref_b[...] = wd_ref[...].astype(BF16)
    else:
        wu_ref_b, wd_ref_b = wu_ref, wd_ref
    u = jnp.maximum(_dot(h_ref[...], wu_ref_b[...]), 0.0)
    acc_ref[ai] += _dot((u * u).astype(BF16), wd_ref_b[...])

    @pl.when(f == pl.num_programs(f_axis) - 1)
    def _():
        y = x_ref[...] + _rms(acc_ref[ai], pw_ref[...])
        ox_ref[...] = y
        if emit_next:
            rest[0][...] = _rms(y, nw_ref[...]).astype(BF16)


def _mlp(h, w_up, w_down, wl, x, post_w, next_w, layer, tm=512, tf=1024):
    M, D = h.shape
    FF = w_up.shape[2]
    tm = min(tm, M)
    hand_off = w_up.dtype != BF16
    if hand_off:
        tf = min(tf, 512)
    emit_next = layer + 1 < next_w.shape[0]
    nt, nf = M // tm, FF // tf
    if hand_off:
        ix = lambda fn: (lambda f, i: fn(i, f))
        last = lambda i, f: (jnp.where(f == nf - 1, i, 0), 0)
        grid, sem, n_acc = (nf, nt), ("arbitrary", "arbitrary"), nt
    else:
        ix = lambda fn: fn
        last = lambda i, f: (i, 0)
        grid, sem, n_acc = (nt, nf), ("parallel", "arbitrary"), 1
    tile = lambda: pl.BlockSpec((tm, D), ix(last))
    out_specs = [tile()] + ([tile()] if emit_next else [])
    out_shape = [jax.ShapeDtypeStruct((M, D), F32)] + ([jax.ShapeDtypeStruct((M, D), BF16)] if emit_next else [])
    vm = 2 * (2 * _nbytes((tm, D), BF16) + 2 * _nbytes((D, tf), w_up.dtype) + 2 * _nbytes((tm, D), F32))
    vm += (1 + n_acc) * _nbytes((tm, D), F32) + 2 * _nbytes((tm, tf), F32)
    if hand_off:
        out_specs += [pl.BlockSpec((None, D, tf), ix(lambda i, f: (0, 0, f))),
                      pl.BlockSpec((None, tf, D), ix(lambda i, f: (0, f, 0)))]
        out_shape += [jax.ShapeDtypeStruct((1, D, FF), BF16), jax.ShapeDtypeStruct((1, FF, D), BF16)]
        vm += 4 * _nbytes((D, tf), BF16)
    outs = pl.pallas_call(
        functools.partial(_mlp_kernel, emit_next=emit_next, hand_off=hand_off),
        grid=grid,
        in_specs=[pl.BlockSpec((tm, D), ix(lambda i, f: (i, 0))),
                  pl.BlockSpec((None, D, tf), ix(lambda i, f: (wl, 0, f))),
                  pl.BlockSpec((None, tf, D), ix(lambda i, f: (wl, f, 0))),
                  tile(), _row(layer)(D), _row(layer + 1 if emit_next else layer)(D)],
        out_specs=out_specs,
        out_shape=out_shape,
        scratch_shapes=[pltpu.VMEM((n_acc, tm, D), F32)],
        compiler_params=_cparams(sem, vm),
        name="relu2_mlp",
    )(h, w_up, w_down, x, post_w, next_w)
    n = 2 if emit_next else 1
    return outs[0], (outs[1] if emit_next else None), (tuple(outs[n:]) if hand_off else None)


def _split3(x):
    hi = x.astype(BF16)
    r1 = x - hi.astype(F32)
    mid = r1.astype(BF16)
    lo = (r1 - mid.astype(F32)).astype(BF16)
    return hi, mid, lo


def _decay_rows(d, rows):
    hi, mid, lo = _split3(d)
    r = lax.broadcasted_iota(jnp.int32, (rows, d.shape[-1]), 0)
    return jnp.where(r == 0, hi.astype(F32), jnp.where(r == 1, mid.astype(F32),
                                                       jnp.where(r == 2, lo.astype(F32), 0.0)))


def _gla_prompt_kernel(*refs, decode_heads=None):
    if decode_heads is None:
        (q_ref, k_ref, lf_ref, v_ref, g_ref, nw_ref, sel_ref, lvl_ref, o_ref, sout_ref,
         s_scr, qs_scr, ks_scr, ke_scr, fac_scr, a_scr) = refs
    else:
        (q_ref, k_ref, lf_ref, v_ref, g_ref, nw_ref, sel_ref, lvl_ref, dq_ref, dk_ref, dv_ref,
         o_ref, sout_ref, do_ref, s_scr, qs_scr, ks_scr, ke_scr, fac_scr, a_scr) = refs
    T = q_ref.shape[1]
    D = q_ref.shape[2]
    H = D // HEAD_DIM
    SUB = GLA_SUB
    nb = T // SUB
    c_idx = pl.program_id(1)

    @pl.when(c_idx == 0)
    def _():
        s_scr[...] = jnp.zeros_like(s_scr)

    side = iter(()) if decode_heads is None else _attn_decode_pieces(dq_ref, dk_ref, dv_ref, do_ref,
                                                                     heads=decode_heads)
    stage = [0]

    def side_step():
        stage[0] += 1
        if stage[0] % SIDE_EVERY == 0:
            next(side, None)

    chunks = []
    c = 2 * SUB
    while c < T:
        chunks.append(c)
        c *= 2
    nf = 2 * len(chunks) + 3
    sums = _dot(sel_ref[...], jnp.concatenate(_split3(lf_ref[0]), axis=0))
    fac_scr[...] = jnp.exp(sums[T:])
    nq = len(chunks) + 1
    for i in range(nb):
        rows = slice(i * SUB, (i + 1) * SUB)
        r = sums[rows]
        q16 = q_ref[0, rows, :].astype(F32) * jnp.exp(r)
        kd = k_ref[0, rows, :].astype(F32) * jnp.exp(-r)
        f_row = lambda n: fac_scr[n * nb + i:n * nb + i + 1, :]
        qs_scr[0, rows, :] = (q16 * f_row(0)).astype(BF16)
        qs_scr[1, rows, :] = q16.astype(BF16)
        ks_scr[0, rows, :] = kd.astype(BF16)
        ks_scr[1, rows, :] = (kd * f_row(nq)).astype(BF16)
        for n in range(len(chunks)):
            qs_scr[n + 2, rows, :] = (q16 * f_row(n + 1)).astype(BF16)
            ks_scr[n + 2, rows, :] = (kd * f_row(nq + n + 1)).astype(BF16)
        ke_scr[rows, :] = (kd * f_row(nf - 1)).astype(BF16)
        side_step()
    ke_scr[T:, :] = _decay_rows(fac_scr[(nf - 1) * nb:(nf - 1) * nb + 1, :], SUB).astype(BF16)

    rr = lax.broadcasted_iota(jnp.int32, (SUB, 2 * HEAD_DIM), 0)
    cc = lax.broadcasted_iota(jnp.int32, (SUB, 2 * HEAD_DIM), 1)
    pick = jnp.where((rr < 3) & (cc >= HEAD_DIM), 1.0, 0.0).astype(BF16)
    zeros = jnp.zeros((T, HEAD_DIM), BF16)
    nw = nw_ref[...]

    lvl = lvl_ref[...]
    for h in range(H):
        hs = slice(h * HEAD_DIM, (h + 1) * HEAD_DIM)
        r = _dot_nt(qs_scr[1, :, hs], ks_scr[0:2, :, hs].reshape(2 * T, HEAD_DIM))
        a = jnp.where(lvl == 1, r[:, :T], jnp.where(lvl == 2, r[:, T:], 0.0))
        for n in range(len(chunks)):
            a = jnp.where(lvl == n + 3, _dot_nt(qs_scr[n + 2, :, hs], ks_scr[n + 2, :, hs]), a)
        a_scr[h] = a.astype(BF16)
        side_step()
    for h in range(H):
        hs = slice(h * HEAD_DIM, (h + 1) * HEAD_DIM)
        vh = v_ref[0, :, hs]
        s_old = s_scr[h]
        o = _dot(jnp.concatenate([a_scr[h], qs_scr[0, :, hs]], axis=1),
                 jnp.concatenate([vh, s_old.astype(BF16)], axis=0))
        rhs = jnp.concatenate([jnp.concatenate([vh, zeros], axis=1), pick], axis=0)
        upd = _dot_tn(ke_scr[:, hs], rhs)
        s_scr[h] = s_old * upd[:, HEAD_DIM:] + upd[:, :HEAD_DIM]
        o_ref[0, :, hs] = (_rms(o, nw) * g_ref[0, :, hs].astype(F32)).astype(o_ref.dtype)
        side_step()
    for _ in side:
        pass

    @pl.when(c_idx == pl.num_programs(1) - 1)
    def _():
        sout_ref[0] = s_scr[...]


def _gla_range_matrix(T, SUB):
    chunks = []
    c = 2 * SUB
    while c < T:
        chunks.append(c)
        c *= 2
    nb = T // SUB
    t = np.arange(T)
    rows = [((t[None, :] <= t[:, None]) & (t[None, :] // SUB == t[:, None] // SUB))]
    start = np.arange(nb)[:, None] * SUB
    for C in [T] + chunks:
        rows.append((t[None, :] >= (start // C) * C) & (t[None, :] < start))
    for C in [SUB] + chunks + [T]:
        rows.append((t[None, :] >= start) & (t[None, :] < (start // C + 1) * C))
    sel = np.concatenate(rows, axis=0).astype(np.float32)
    row, col = t[:, None], t[None, :]
    lvl = np.where((row // SUB == col // SUB) & (col <= row), 1, 0)
    for j, C in enumerate([SUB] + chunks):
        lvl = np.where(((row // C) % 2 == 1) & (col // C == row // C - 1), 2 + j, lvl)
    return sel, lvl.astype(np.int32), len(chunks)


def _gla_prompt(q, k, lf, v, g, norm_w, mixer, B, L, decode=None):
    D = q.shape[-1]
    H = D // HEAD_DIM
    T = GLA_BLOCK
    assert T == HEAD_DIM
    nc = L // T
    sel, lvl, nchunks = _gla_range_matrix(T, GLA_SUB)
    sel3 = jnp.asarray(np.concatenate([sel] * 3, axis=1), BF16)
    nfac = sel.shape[0] - T
    r3 = lambda a: a.reshape(B, L, D)
    blk = lambda: pl.BlockSpec((1, T, D), lambda b, c: (b, c, 0))
    const = lambda a: pl.BlockSpec(a.shape, lambda b, c: (0, 0))
    vm = 2 * (5 * _nbytes((T, D), BF16) + _nbytes((T, D), F32)) + 3 * _nbytes((H, HEAD_DIM, HEAD_DIM), F32)
    vm += (2 * nchunks + 6) * _nbytes((T, D), BF16) + 8 * _nbytes((T, D), F32)
    in_specs = [blk(), blk(), blk(), blk(), blk(), _row(mixer)(HEAD_DIM), const(sel3), const(lvl)]
    args = [r3(q), r3(k), r3(lf), r3(v), r3(g), norm_w, sel3, jnp.asarray(lvl)]
    out_specs = [blk(), pl.BlockSpec((1, H, HEAD_DIM, HEAD_DIM), lambda b, c: (b, 0, 0, 0))]
    out_shape = [jax.ShapeDtypeStruct((B, L, D), BF16), jax.ShapeDtypeStruct((B, H, HEAD_DIM, HEAD_DIM), F32)]
    heads = None
    if decode is not None:
        dq, ck, cv, layer, Bd, Ld, heads, d_dtype = decode
        assert Ld == SUBL and SUBL % heads == 0 and Bd % (B * nc) == 0
        bb = Bd // (B * nc)
        kv_shape = (bb,) + ck.shape[2:]
        kv_spec = lambda: pl.BlockSpec((None,) + kv_shape, lambda b, c: (layer, b * nc + c, 0, 0, 0, 0))
        dblk = lambda: pl.BlockSpec((bb, Ld, D), lambda b, c: (b * nc + c, 0, 0))
        in_specs += [dblk(), kv_spec(), kv_spec()]
        args += [dq.reshape(Bd, Ld, D), ck, cv]
        out_specs.append(dblk())
        out_shape.append(jax.ShapeDtypeStruct((Bd, Ld, D), d_dtype))
        vm += 4 * _nbytes((bb, Ld, D), F32) + 4 * _nbytes(kv_shape, ck.dtype)
        vm += 4 * _nbytes(kv_shape[1:], BF16) + 6 * _nbytes((SUBL * Ld, ck.shape[2] * SUBL), F32)
    outs = pl.pallas_call(
        functools.partial(_gla_prompt_kernel, decode_heads=heads),
        grid=(B, nc),
        in_specs=in_specs,
        out_specs=out_specs,
        out_shape=out_shape,
        scratch_shapes=[pltpu.VMEM((H, HEAD_DIM, HEAD_DIM), F32),
                        pltpu.VMEM((nchunks + 2, T, D), BF16),
                        pltpu.VMEM((nchunks + 2, T, D), BF16),
                        pltpu.VMEM((T + GLA_SUB, D), BF16),
                        pltpu.VMEM((nfac, D), F32),
                        pltpu.VMEM((H, T, T), BF16)],
        compiler_params=_cparams(("parallel", "arbitrary"), vm),
        name="hgrn2_recurrence_prompt",
    )(*args)
    a_dec = outs[2].reshape(Bd * Ld, D) if decode is not None else None
    return outs[0].reshape(B * L, D), outs[1], a_dec


def _gla_step_kernel(q_ref, k_ref, lf_ref, v_ref, g_ref, s0_ref, nw_ref, o_ref, sout_ref,
                     qs_scr, ks_scr, ke_scr, ds_scr):
    bb, T, D = q_ref.shape
    H = D // HEAD_DIM
    R = bb * T
    pos = lax.broadcasted_iota(jnp.int32, (bb, T, D), 1)
    b = lf_ref[...]
    sh = 1
    while sh < T:
        b = b + jnp.where(pos >= sh, pltpu.roll(b, sh, 1), 0.0)
        sh *= 2
    b_last = b[:, T - 1:T, :]
    k = k_ref[...]
    qs_scr[...] = q_ref[...] * jnp.exp(b)
    ks_scr[...] = k * jnp.exp(-b)
    ke_scr[...] = k * jnp.exp(b_last - b)
    d_hi, d_mid, d_lo = (t.astype(F32) for t in _split3(jnp.exp(b_last)))
    ds_scr[...] = jnp.where(pos == 0, d_hi, jnp.where(pos == 1, d_mid, jnp.where(pos == 2, d_lo, 0.0)))

    row = lax.broadcasted_iota(jnp.int32, (R, R), 0)
    col = lax.broadcasted_iota(jnp.int32, (R, R), 1)
    same_seq_causal = (row // T == col // T) & (col <= row)
    rr = lax.broadcasted_iota(jnp.int32, (T, 2 * HEAD_DIM), 0)
    cc = lax.broadcasted_iota(jnp.int32, (T, 2 * HEAD_DIM), 1)
    pick = jnp.where((rr < 3) & (cc >= HEAD_DIM), 1.0, 0.0)
    zeros = jnp.zeros((T, HEAD_DIM), F32)
    nw = nw_ref[...]

    for h in range(H):
        hs = slice(h * HEAD_DIM, (h + 1) * HEAD_DIM)
        qd = qs_scr[:, :, hs].reshape(R, HEAD_DIM)
        kd = ks_scr[:, :, hs].reshape(R, HEAD_DIM)
        vh = v_ref[:, :, hs].reshape(R, HEAD_DIM)
        a = jnp.where(same_seq_causal, _dot_nt(qd.astype(BF16), kd.astype(BF16)), 0.0)
        o_intra = _dot(a.astype(BF16), vh.astype(BF16))
        outs = []
        for i in range(bb):
            s_old = s0_ref[i, h]
            outs.append(o_intra[i * T:(i + 1) * T]
                        + _dot(qd[i * T:(i + 1) * T].astype(BF16), s_old.astype(BF16)))
            lhs = jnp.concatenate([ke_scr[i, :, hs], ds_scr[i, :, hs]], axis=0).astype(BF16)
            rhs = jnp.concatenate([jnp.concatenate([vh[i * T:(i + 1) * T], zeros], axis=1), pick],
                                  axis=0).astype(BF16)
            upd = _dot_tn(lhs, rhs)
            sout_ref[i, h] = s_old * upd[:, HEAD_DIM:] + upd[:, :HEAD_DIM]
        o = jnp.concatenate(outs, axis=0)
        on = _rms(o, nw) * g_ref[:, :, hs].reshape(R, HEAD_DIM)
        o_ref[:, :, hs] = on.reshape(bb, T, HEAD_DIM).astype(o_ref.dtype)


def _gla_step(q, k, lf, v, g, s0, norm_w, mixer, B, L, bb=8):
    D = q.shape[-1]
    H = D // HEAD_DIM
    assert L == 8 and B % bb == 0
    r3 = lambda a: a.reshape(B, L, D)
    blk = lambda: pl.BlockSpec((bb, L, D), lambda b: (b, 0, 0))
    sblk = lambda: pl.BlockSpec((bb, H, HEAD_DIM, HEAD_DIM), lambda b: (b, 0, 0, 0))
    vm = 2 * (6 * _nbytes((bb, L, D), F32) + 2 * _nbytes((bb, H, HEAD_DIM, HEAD_DIM), F32))
    vm += 12 * _nbytes((bb, L, D), F32)
    o, s = pl.pallas_call(
        _gla_step_kernel,
        grid=(B // bb,),
        in_specs=[blk(), blk(), blk(), blk(), blk(),
                  pl.BlockSpec((None, bb, H, HEAD_DIM, HEAD_DIM), lambda b: (mixer, b, 0, 0, 0)),
                  _row(mixer)(HEAD_DIM)],
        out_specs=[blk(), sblk()],
        out_shape=[jax.ShapeDtypeStruct((B, L, D), F32),
                   jax.ShapeDtypeStruct((B, H, HEAD_DIM, HEAD_DIM), F32)],
        scratch_shapes=[pltpu.VMEM((bb, L, D), F32)] * 4,
        compiler_params=_cparams(("parallel",), vm),
        name="hgrn2_recurrence_step",
    )(r3(q), r3(k), r3(lf), r3(v), r3(g), s0, norm_w)
    return o.reshape(B * L, D), s


def _pool_kernel(*refs, pos0, has_halo):
    if has_halo:
        u_ref, halo_ref, buf_ref, pw_ref, ps_ref, o_ref, nb_ref, ext = refs
    else:
        u_ref, buf_ref, pw_ref, ps_ref, o_ref, nb_ref, ext = refs
    bb, tl, D = u_ref.shape
    G = len(POOL_WINDOWS)
    gc = D // G
    l_idx = pl.program_id(1)
    ext[:, HALO:, :] = u_ref[...]
    if has_halo:
        @pl.when(l_idx == 0)
        def _():
            ext[:, :HALO, :] = buf_ref[...]

        @pl.when(l_idx > 0)
        def _():
            ext[:, :HALO, :] = halo_ref[...]
    else:
        ext[:, :HALO, :] = buf_ref[...]

    pos = pos0 + l_idx * tl + lax.broadcasted_iota(jnp.int32, (tl, gc), 0)
    for gi, w in enumerate(POOL_WINDOWS):
        ls = slice(gi * gc, (gi + 1) * gc)
        cnt = jnp.minimum(w, pos + 1).astype(F32)
        rows = []
        for i in range(bb):
            u = ext[i, HALO:HALO + tl, ls]
            s = u
            for d in range(1, w):
                s = s + ext[i, HALO - d:HALO - d + tl, ls]
            rows.append(s / cnt - u)
        pooled = rows[0] if bb == 1 else jnp.concatenate(rows, axis=0)
        mixed = (_dot(pooled.astype(BF16), pw_ref[gi]) * ps_ref[:, ls]).astype(o_ref.dtype)
        for i in range(bb):
            o_ref[i, :, ls] = mixed[i * tl:(i + 1) * tl]

    @pl.when(l_idx == pl.num_programs(1) - 1)
    def _():
        nb_ref[...] = ext[:, tl:tl + HALO, :]


def _pool(u, buf, pos0, pool_w, pool_scale, mixer, B, L, bb, tl, out_dtype):
    D = u.shape[-1]
    G = len(POOL_WINDOWS)
    gc = D // G
    tl = min(tl, L)
    u3 = u.reshape(B, L, D)
    has_halo = L > tl
    step = tl // HALO
    in_specs = [pl.BlockSpec((bb, tl, D), lambda b, l: (b, l, 0))]
    args = [u3]
    if has_halo:
        in_specs.append(pl.BlockSpec((bb, HALO, D), lambda b, l: (b, jnp.maximum(l * step - 1, 0), 0)))
        args.append(u3)
    in_specs += [pl.BlockSpec((None, bb, HALO, D), lambda b, l: (mixer, b, 0, 0)),
                 pl.BlockSpec((None, G, gc, gc), lambda b, l: (mixer, 0, 0, 0)),
                 _row(mixer)(D)]
    args += [buf, pool_w, pool_scale]
    vm = 2 * (_nbytes((bb, tl, D), F32) + 3 * _nbytes((bb, HALO, D), F32) + _nbytes((G, gc, gc), BF16)
              + _nbytes((bb, tl, D), BF16)) + _nbytes((bb, tl + HALO, D), F32) + 4 * _nbytes((bb, tl, D), F32)
    mixed, newbuf = pl.pallas_call(
        functools.partial(_pool_kernel, pos0=pos0, has_halo=has_halo),
        grid=(B // bb, L // tl),
        in_specs=in_specs,
        out_specs=[pl.BlockSpec((bb, tl, D), lambda b, l: (b, l, 0)),
                   pl.BlockSpec((bb, HALO, D), lambda b, l: (b, 0, 0))],
        out_shape=[jax.ShapeDtypeStruct((B, L, D), out_dtype), jax.ShapeDtypeStruct((B, HALO, D), F32)],
        scratch_shapes=[pltpu.VMEM((bb, tl + HALO, D), F32)],
        compiler_params=_cparams(("parallel", "arbitrary"), vm),
        name="causal_pool",
    )(*args)
    return mixed.reshape(B * L, D), newbuf


def _attn_kernel(q_ref, k_ref, v_ref, o_ref, *, heads):
    bb, tq, D = q_ref.shape
    hd = D // heads
    scale = hd ** -0.5
    for i in range(bb):
        for h in range(heads):
            hs = slice(h * hd, (h + 1) * hd)
            s = _dot_nt(q_ref[i, :, hs].astype(BF16), k_ref[i, :, hs].astype(BF16)) * scale
            p = jnp.exp(s - jnp.max(s, axis=-1, keepdims=True))
            den = jnp.sum(p, axis=-1, keepdims=True)
            o_ref[i, :, hs] = (_dot(p.astype(BF16), v_ref[i, :, hs].astype(BF16)) / den).astype(o_ref.dtype)


def _attn(q, mk, mv, layer, B, L, heads, bb, tq, out_dtype):
    D = q.shape[-1]
    tq = min(tq, L)
    kv_shape = (bb,) + mk.shape[2:]
    kv_blk = (None,) + kv_shape
    vm = 2 * (_nbytes((bb, tq, D), q.dtype) + 2 * _nbytes(kv_shape, mk.dtype) + _nbytes((bb, tq, D), out_dtype))
    vm += 2 * _nbytes(kv_shape, BF16) + 8 * _nbytes((tq, D // heads), F32)
    o = pl.pallas_call(
        functools.partial(_attn_kernel, heads=heads),
        grid=(B // bb, L // tq),
        in_specs=[pl.BlockSpec((bb, tq, D), lambda b, l: (b, l, 0)),
                  pl.BlockSpec(kv_blk, lambda b, l: (layer, b, 0, 0)),
                  pl.BlockSpec(kv_blk, lambda b, l: (layer, b, 0, 0))],
        out_specs=pl.BlockSpec((bb, tq, D), lambda b, l: (b, l, 0)),
        out_shape=jax.ShapeDtypeStruct((B, L, D), out_dtype),
        compiler_params=_cparams(("parallel", "parallel"), vm),
        name="memory_cross_attention",
    )(q.reshape(B, L, D), mk, mv)
    return o.reshape(B * L, D)


def _attn_decode_pieces(q_ref, k_ref, v_ref, o_ref, *, heads):
    bb, tq, D = q_ref.shape
    n_mem, nu = k_ref.shape[1], k_ref.shape[2]
    hd = D // heads
    per = SUBL // heads
    W = n_mem * SUBL
    scale = hd ** -0.5
    res = lax.broadcasted_iota(jnp.int32, (tq, W), 1) % SUBL
    tile = lambda r: jnp.concatenate([r] * (W // LANES), axis=1)

    def feat(j):
        c, h = divmod(j, heads)
        return slice(h * hd + c * LANES, h * hd + (c + 1) * LANES)

    def same_row_reduce(x, op):
        r = x[:, :LANES]
        for j in range(1, W // LANES):
            r = op(r, x[:, j * LANES:(j + 1) * LANES])
        sh = SUBL
        while sh < LANES:
            r = op(r, pltpu.roll(r, sh, 1))
            sh *= 2
        return r

    for i in range(bb):
        w = None
        for u in range(nu):
            slab = k_ref[i, :, u].reshape(W, LANES).astype(BF16)
            qm = jnp.concatenate([q_ref[i, :, feat(u * SUBL + s)] for s in range(SUBL)], axis=0)
            p = _dot_nt(qm.astype(BF16), slab)
            z = p[:tq]
            for s in range(1, SUBL):
                z = jnp.where(res == s, p[s * tq:(s + 1) * tq], z)
            w = z if w is None else w + z
            yield
        sc = w
        for c in range(1, per):
            sc = sc + pltpu.roll(w, W - c * heads, 1)
        sc = sc * scale
        e = jnp.exp(sc - tile(same_row_reduce(sc, jnp.maximum)))
        pn = e / tile(same_row_reduce(e, jnp.add))
        blocks = []
        for s in range(SUBL):
            c_lo = s // heads
            src = pn if c_lo == 0 else pltpu.roll(pn, c_lo * heads, 1)
            blocks.append(jnp.where(res == s, src, 0.0))
        pm = jnp.concatenate(blocks, axis=0).astype(BF16)
        yield
        for u in range(nu):
            o = _dot(pm, v_ref[i, :, u].reshape(W, LANES).astype(BF16))
            for s in range(SUBL):
                o_ref[i, :, feat(u * SUBL + s)] = o[s * tq:(s + 1) * tq].astype(o_ref.dtype)
            yield


def _attn_decode_kernel(q_ref, k_ref, v_ref, o_ref, *, heads):
    for _ in _attn_decode_pieces(q_ref, k_ref, v_ref, o_ref, heads=heads):
        pass


def _attn_decode(q, ck, cv, layer, B, L, heads, bb, out_dtype):
    D = q.shape[-1]
    assert L == SUBL and SUBL % heads == 0 and D % (SUBL * LANES) == 0
    kv_shape = (bb,) + ck.shape[2:]
    vm = 2 * (2 * _nbytes((bb, L, D), F32) + 2 * _nbytes(kv_shape, ck.dtype))
    vm += 4 * _nbytes(kv_shape[1:], BF16) + 6 * _nbytes((SUBL * L, ck.shape[2] * SUBL), F32)
    o = pl.pallas_call(
        functools.partial(_attn_decode_kernel, heads=heads),
        grid=(B // bb,),
        in_specs=[pl.BlockSpec((bb, L, D), lambda b: (b, 0, 0)),
                  pl.BlockSpec((None,) + kv_shape, lambda b: (layer, b, 0, 0, 0, 0)),
                  pl.BlockSpec((None,) + kv_shape, lambda b: (layer, b, 0, 0, 0, 0))],
        out_specs=pl.BlockSpec((bb, L, D), lambda b: (b, 0, 0)),
        out_shape=jax.ShapeDtypeStruct((B, L, D), out_dtype),
        compiler_params=_cparams(("parallel",), vm),
        name="memory_cross_attention_decode",
    )(q.reshape(B, L, D), ck, cv)
    return o.reshape(B * L, D)


def _mem_proj_kernel(x_ref, nw_ref, w_ref, flat_ref, bf_ref, w_scr, *, heads):
    tm, D = x_ref.shape
    R = D // LANES
    nchunk = R // heads

    @pl.when(pl.program_id(1) == 0)
    def _():
        w_scr[...] = w_ref[...].astype(BF16)

    acc = _dot(_rms(x_ref[...], nw_ref[...]).astype(BF16), w_scr[...])
    bf_ref[...] = acc.astype(bf_ref.dtype)
    for h in range(heads):
        for c in range(nchunk):
            j = h * nchunk + c
            flat_ref[pl.ds(c * heads + h, tm, stride=R), :] = acc[:, j * LANES:(j + 1) * LANES]


def _mem_proj(mem, norm_w, w, half, heads, tm=256):
    M, D = mem.shape
    depth = w.shape[0]
    R = D // LANES
    tm = min(tm, M)
    nt = M // tm
    vm = 2 * (_nbytes((tm, D), F32) + _nbytes((D, D), F32) + _nbytes((tm, D), F32) + _nbytes((tm, D), BF16))
    vm += _nbytes((D, D), BF16) + 2 * _nbytes((tm, D), F32)
    return pl.pallas_call(
        functools.partial(_mem_proj_kernel, heads=heads),
        grid=(depth, nt),
        in_specs=[pl.BlockSpec((tm, D), lambda l, i: (i, 0)),
                  pl.BlockSpec((None, 1, D), lambda l, i: (l, 0, 0)),
                  pl.BlockSpec((None, D, D), lambda l, i: (l, 0, half))],
        out_specs=[pl.BlockSpec((tm * R, LANES), lambda l, i: (l * nt + i, 0)),
                   pl.BlockSpec((None, tm, D), lambda l, i: (l, i, 0))],
        out_shape=[jax.ShapeDtypeStruct((depth * M * R, LANES), F32),
                   jax.ShapeDtypeStruct((depth, M, D), BF16)],
        scratch_shapes=[pltpu.VMEM((D, D), BF16)],
        compiler_params=_cparams(("parallel", "arbitrary"), vm),
        name="memory_kv_projection",
    )(mem, norm_w, w)


class _Group:
    def __init__(self, x, B, L, pos0, S_in, buf_in, mk, mv, prompt, p):
        self.B, self.L, self.pos0, self.S_in, self.buf_in, self.mk, self.mv = B, L, pos0, S_in, buf_in, mk, mv
        self.prompt = prompt
        self.act = BF16 if prompt else F32
        self.x = x
        self.hn = _rmsnorm(x, p["norm_mix_pre"], 0)
        self.S_out, self.buf_out = [], []


class _NoRider:
    def job(self):
        return None

    def served(self, a):
        pass


def _mixer(g, l, p, rider=_NoRider()):
    j = l // 2
    if l % 2 == 0:
        w, wl = _weight(p, "w_in_a", j)
        q, k, lf, v, gate, cast, a = _proj_a(g.hn, w, wl, p["hg_lb_logits"], l, g.act,
                                             rider.job() if g.prompt else None)
        _hand_off(p, "w_in_a", j, cast)
        rider.served(a)
        if g.prompt:
            o, S, a = _gla_prompt(q, k, lf, v, gate, p["hg_norm"], j, g.B, g.L, rider.job())
            rider.served(a)
        else:
            o, S = _gla_step(q, k, lf, v, gate, g.S_in, p["hg_norm"], j, g.B, g.L)
        g.S_out.append(S)
        g.x, g.hn = _mm_res(o, p["w_out_a"], j, g.x, p["norm_mix_post"], p["norm_x_pre"], l)
    else:
        u = _project(g, p, "w_in_b", j, F32)
        bb, tl = (1, 512) if g.prompt else (16, g.L)
        mixed, nb = _pool(u, g.buf_in, g.pos0, p["pool_w"], p["pool_scale"], j, g.B, g.L, bb, tl, g.act)
        g.buf_out.append(nb[:, 1:, :])
        g.x, g.hn = _mm_res(mixed, p["w_out_b"], j, g.x, p["norm_mix_post"], p["norm_x_pre"], l)


def _weight(p, name, idx):
    cast = p["cast"].get((name, idx))
    return (cast, 0) if cast is not None else (p[name], idx)


def _hand_off(p, name, idx, cast):
    if cast is not None:
        p["cast"][(name, idx)] = cast


def _project(g, p, name, idx, dtype):
    w, wl = _weight(p, name, idx)
    outs = _matmul(g.hn, w, wl, [dtype])
    _hand_off(p, name, idx, outs[1] if len(outs) > 1 else None)
    return outs[0]


def _after_attention(g, l, a, p):
    g.x, g.hn = _mm_res(a, p["w_xo"], l, g.x, p["norm_x_post"], p["norm_mlp_pre"], l)
    (wu, wl), (wd, _) = _weight(p, "w_up", l), _weight(p, "w_down", l)
    g.x, g.hn, cast = _mlp(g.hn, wu, wd, wl, g.x, p["norm_mlp_post"], p["norm_mix_pre"], l)
    if cast is not None:
        _hand_off(p, "w_up", l, cast[0])
        _hand_off(p, "w_down", l, cast[1])


class _SampleWalk:
    def __init__(self, g, p):
        self.g, self.p = g, p
        self._walk = self._layers()
        self._pending = next(self._walk, None)

    def _layers(self):
        g, p = self.g, self.p
        for l in range(p["w_xq"].shape[0]):
            _mixer(g, l, p)
            q = _project(g, p, "w_xq", l, g.act)
            a = yield (q, g.mk, g.mv, l, g.B, g.L, p["mem_heads"], g.act)
            _after_attention(g, l, a, p)

    def job(self):
        return self._pending

    def served(self, a):
        if a is not None:
            try:
                self._pending = self._walk.send(a)
            except StopIteration:
                self._pending = None

    def serve_standalone(self):
        q, mk, mv, l, B, L, heads, dtype = self._pending
        self.served(_attn_decode(q, mk, mv, l, B, L, heads, 4, dtype))


def _trunks(gp, gs, p):
    heads = p["mem_heads"]
    rider = _SampleWalk(gs, p)
    for l in range(p["w_xq"].shape[0]):
        if l % 2 == 1 and rider.job() is not None:
            rider.serve_standalone()
        _mixer(gp, l, p, rider)
        qp = _project(gp, p, "w_xq", l, gp.act)
        _after_attention(gp, l, _attn(qp, gp.mk, gp.mv, l, gp.B, gp.L, heads, 1, 512, gp.act), p)
    while rider.job() is not None:
        rider.serve_standalone()


def kernel(x_prompt, x_sample, state_hgrn, state_pool, cache_mem_k, cache_mem_v, mem_prompt, w_in_a, hg_lb_logits, hg_norm, w_out_a, w_in_b, pool_w, pool_scale, w_out_b, norm_mem, w_xq, w_xkv, w_xo, norm_mix_pre, norm_mix_post, norm_x_pre, norm_x_post, norm_mlp_pre, norm_mlp_post, w_up, w_down):
    B, L, D = x_prompt.shape
    Bs, Ls, _ = x_sample.shape
    depth = w_xq.shape[0]
    n_mem, heads = cache_mem_k.shape[2], cache_mem_k.shape[3]
    assert L % GLA_BLOCK == 0 and Ls <= GLA_SUB and D % (HEAD_DIM * len(POOL_WINDOWS)) == 0
    assert state_pool.shape[2] == HALO - 1
    bf = lambda a: a.astype(BF16)
    rows = lambda a: a.reshape(a.shape[0], 1, a.shape[1])
    p = dict(w_in_a=w_in_a, hg_lb_logits=hg_lb_logits, hg_norm=rows(hg_norm), w_out_a=bf(w_out_a),
             w_in_b=w_in_b, pool_w=bf(pool_w), pool_scale=rows(pool_scale), w_out_b=bf(w_out_b),
             w_xq=w_xq, w_xo=bf(w_xo), norm_mix_pre=rows(norm_mix_pre), norm_mix_post=rows(norm_mix_post),
             norm_x_pre=rows(norm_x_pre), norm_x_post=rows(norm_x_post), norm_mlp_pre=rows(norm_mlp_pre),
             norm_mlp_post=rows(norm_mlp_post), w_up=w_up, w_down=w_down, mem_heads=heads, cast={})
    R = D // LANES
    nchunk = R // heads

    def cache_view(flat, nb):
        c = flat.reshape(depth, nb, n_mem, nchunk, heads, LANES)
        return jnp.swapaxes(c, 3, 4).reshape(depth, nb, n_mem, heads, D // heads)

    def flat_view(cache):
        nb = cache.shape[1]
        c = cache.reshape(depth, nb, n_mem, heads, nchunk, LANES)
        return jnp.swapaxes(c, 3, 4).reshape(depth, nb, n_mem, R // SUBL, SUBL, LANES)

    mem2d = mem_prompt.reshape(B * n_mem, D)
    k_flat, k_bf = _mem_proj(mem2d, rows(norm_mem), w_xkv, 0, heads)
    v_flat, v_bf = _mem_proj(mem2d, rows(norm_mem), w_xkv, 1, heads)
    cache_mem_k_prompt = cache_view(k_flat, B)
    cache_mem_v_prompt = cache_view(v_flat, B)
    mk_bf = k_bf.reshape(depth, B, n_mem, D)
    mv_bf = v_bf.reshape(depth, B, n_mem, D)

    n_b = state_pool.shape[0]
    buf0 = jnp.zeros((n_b, B, HALO, D), F32)
    buf_s = jnp.pad(state_pool, ((0, 0), (0, 0), (1, 0), (0, 0)))
    gp = _Group(x_prompt.reshape(B * L, D), B, L, 0, None, buf0, mk_bf, mv_bf, True, p)
    gs = _Group(x_sample.reshape(Bs * Ls, D), Bs, Ls, PAST_LEN, state_hgrn, buf_s,
                flat_view(cache_mem_k), flat_view(cache_mem_v), False, p)
    _trunks(gp, gs, p)
    return (gp.x.reshape(B, L, D), gs.x.reshape(Bs, Ls, D), jnp.stack(gp.S_out), jnp.stack(gp.buf_out),
            cache_mem_k_prompt, cache_mem_v_prompt, jnp.stack(gs.S_out), jnp.stack(gs.buf_out))
```

```python
import functools

import numpy as np

import jax
import jax.numpy as jnp
from jax import lax
from jax.experimental import pallas as pl
from jax.experimental.pallas import tpu as pltpu

F32 = jnp.float32
BF16 = jnp.bfloat16
EPS = 1e-6
PAST_LEN = 16384
POOL_WINDOWS = (2, 4, 8, 16)
LANES = 128
SUBL = 8
HEAD_DIM = 128
GLA_BLOCK = 128
GLA_SUB = 16
HALO = 16
MM_RES_SPLIT = 4
SIDE_EVERY = 4
PROJ_SIDE_PIECES = 2

V7X_VMEM_BYTES = 64 * 1024 * 1024
VMEM_CAP = V7X_VMEM_BYTES - 6 * 1024 * 1024


def _cparams(sem, vmem_bytes):
    return pltpu.CompilerParams(dimension_semantics=sem,
                                vmem_limit_bytes=int(min(max(vmem_bytes, 16 * 1024 * 1024), VMEM_CAP)))


def _nbytes(shape, dtype):
    n = 1
    for s in shape:
        n *= s
    return n * jnp.dtype(dtype).itemsize


def _rms(v, w):
    ms = jnp.mean(v * v, axis=-1, keepdims=True)
    return v * lax.rsqrt(ms + EPS) * w


def _sigmoid(x):
    return 1.0 / (1.0 + jnp.exp(-x))


def _dot(a, b):
    return jnp.dot(a, b, preferred_element_type=F32)


def _dot_nt(a, b):
    return lax.dot_general(a, b, (((1,), (1,)), ((), ())), preferred_element_type=F32)


def _dot_tn(a, b, precision=None):
    return lax.dot_general(a, b, (((0,), (0,)), ((), ())), preferred_element_type=F32,
                           precision=precision)


def _rmsnorm_kernel(x_ref, w_ref, o_ref):
    o_ref[...] = _rms(x_ref[...], w_ref[...]).astype(o_ref.dtype)


def _row(layer):
    return lambda width: pl.BlockSpec((None, 1, width), lambda *_: (layer, 0, 0))


def _rmsnorm(x, w, layer, tm=512):
    M, D = x.shape
    tm = min(tm, M)
    return pl.pallas_call(
        _rmsnorm_kernel,
        grid=(M // tm,),
        in_specs=[pl.BlockSpec((tm, D), lambda i: (i, 0)), _row(layer)(D)],
        out_specs=pl.BlockSpec((tm, D), lambda i: (i, 0)),
        out_shape=jax.ShapeDtypeStruct((M, D), BF16),
        compiler_params=_cparams(("parallel",), 4 * _nbytes((tm, D), F32)),
        name="rmsnorm",
    )(x, w)


def _mm_kernel(a_ref, w_ref, *o_refs, hand_off):
    w = w_ref[...].astype(BF16)
    if hand_off:
        o_refs[-1][...] = w
        o_refs = o_refs[:-1]
    acc = _dot(a_ref[...], w)
    for o_ref in o_refs:
        o_ref[...] = acc.astype(o_ref.dtype)


def _matmul(a, w, layer, out_dtypes, tm=1024, tn=1024):
    M, K = a.shape
    N = w.shape[2]
    tm, tn = min(tm, M), min(tn, N)
    hand_off = w.dtype != BF16
    assert not hand_off or M == tm
    vm = 2 * (_nbytes((tm, K), BF16) + _nbytes((K, tn), w.dtype)) + _nbytes((tm, tn), F32)
    vm += sum(2 * _nbytes((tm, tn), d) for d in out_dtypes)
    out_specs = [pl.BlockSpec((tm, tn), lambda i, j: (i, j)) for _ in out_dtypes]
    out_shape = [jax.ShapeDtypeStruct((M, N), d) for d in out_dtypes]
    if hand_off:
        out_specs.append(pl.BlockSpec((None, K, tn), lambda i, j: (0, 0, j)))
        out_shape.append(jax.ShapeDtypeStruct((1, K, N), BF16))
        vm += 3 * _nbytes((K, tn), BF16)
    outs = pl.pallas_call(
        functools.partial(_mm_kernel, hand_off=hand_off),
        grid=(M // tm, N // tn),
        in_specs=[pl.BlockSpec((tm, K), lambda i, j: (i, 0)),
                  pl.BlockSpec((None, K, tn), lambda i, j: (layer, 0, j))],
        out_specs=out_specs,
        out_shape=out_shape,
        compiler_params=_cparams(("parallel", "parallel"), vm),
        name="matmul",
    )(a, w)
    return outs


def _proj_a_kernel(*refs, layer, n_cast, decode_heads):
    h_ref, wq_ref, wf_ref, wi_ref, wg_ref, lbl_ref = refs[:6]
    n_in = 6 if decode_heads is None else 9
    q_ref, k_ref, lf_ref, v_ref, g_ref = refs[n_in:n_in + 5]
    w_out_refs = refs[n_in + 5:n_in + 5 + n_cast]
    side = iter(()) if decode_heads is None else _attn_decode_pieces(*refs[6:9], refs[-1], heads=decode_heads)

    def side_steps():
        for _ in range(PROJ_SIDE_PIECES):
            next(side, None)

    h = h_ref[...]
    ws = [r[...].astype(BF16) for r in (wq_ref, wf_ref, wi_ref, wg_ref)]
    for r, w in zip(w_out_refs, ws):
        r[...] = w
    aq = _dot(h, ws[0])
    q_ref[...] = (aq * _sigmoid(aq)).astype(q_ref.dtype)
    side_steps()
    lg = lbl_ref[...]
    e = jnp.exp(lg - jnp.max(lg, axis=0, keepdims=True))
    lb = jnp.sum(e[:layer + 1], axis=0, keepdims=True) / jnp.sum(e, axis=0, keepdims=True)
    f = lb + (1.0 - lb) * _sigmoid(_dot(h, ws[1]))
    k_ref[...] = (1.0 - f).astype(k_ref.dtype)
    lf_ref[...] = jnp.log(f)
    side_steps()
    v_ref[...] = _dot(h, ws[2]).astype(v_ref.dtype)
    side_steps()
    ag = _dot(h, ws[3])
    g_ref[...] = (ag * _sigmoid(ag)).astype(g_ref.dtype)
    for _ in side:
        pass


def _proj_a(h, w, mixer, lb_logits, layer, act_dtype, decode=None, tm=1024, tn=512):
    M, D = h.shape
    if decode is not None:
        tm = min(tm, 512)
    tm = min(tm, M)
    parts = isinstance(w, (tuple, list))
    hand_off = not parts and w.dtype != BF16
    if hand_off:
        tn = min(tn, 256)
        assert M == tm
    ni, nj = M // tm, D // tn
    nl = lb_logits.shape[0]
    if parts:
        wspec = lambda s: pl.BlockSpec((None, D, tn), lambda i, j: (0, 0, j))
        w_args, w_dtype = list(w), BF16
    else:
        wspec = lambda s: pl.BlockSpec((None, D, tn), lambda i, j: (mixer, 0, j + s * nj))
        w_args, w_dtype = [w] * 4, w.dtype
    ospec = pl.BlockSpec((tm, tn), lambda i, j: (i, j))
    vm = 2 * (_nbytes((tm, D), BF16) + 4 * _nbytes((D, tn), w_dtype) + 5 * _nbytes((tm, tn), F32))
    vm += 4 * _nbytes((tm, tn), F32)
    in_specs = [pl.BlockSpec((tm, D), lambda i, j: (i, 0)), wspec(0), wspec(1), wspec(2), wspec(3),
                pl.BlockSpec((nl, tn), lambda i, j: (0, j))]
    args = [h, *w_args, lb_logits]
    out_specs = [ospec] * 5
    out_shape = [jax.ShapeDtypeStruct((M, D), act_dtype), jax.ShapeDtypeStruct((M, D), act_dtype),
                 jax.ShapeDtypeStruct((M, D), F32), jax.ShapeDtypeStruct((M, D), act_dtype),
                 jax.ShapeDtypeStruct((M, D), act_dtype)]
    if hand_off:
        out_specs += [pl.BlockSpec((None, D, tn), lambda i, j: (0, 0, j))] * 4
        out_shape += [jax.ShapeDtypeStruct((1, D, D), BF16)] * 4
        vm += 12 * _nbytes((D, tn), BF16)
    heads = None
    if decode is not None:
        dq, ck, cv, dlayer, Bd, Ld, heads, d_dtype = decode
        assert Ld == SUBL and SUBL % heads == 0 and Bd % (ni * nj) == 0
        bb = Bd // (ni * nj)
        kv_shape = (bb,) + ck.shape[2:]
        kv_spec = lambda: pl.BlockSpec((None,) + kv_shape, lambda i, j: (dlayer, i * nj + j, 0, 0, 0, 0))
        dblk = lambda: pl.BlockSpec((bb, Ld, D), lambda i, j: (i * nj + j, 0, 0))
        in_specs += [dblk(), kv_spec(), kv_spec()]
        args += [dq.reshape(Bd, Ld, D), ck, cv]
        out_specs.append(dblk())
        out_shape.append(jax.ShapeDtypeStruct((Bd, Ld, D), d_dtype))
        vm += 4 * _nbytes((bb, Ld, D), F32) + 4 * _nbytes(kv_shape, ck.dtype)
        vm += 4 * _nbytes(kv_shape[1:], BF16) + 6 * _nbytes((SUBL * Ld, ck.shape[2] * SUBL), F32)
    outs = pl.pallas_call(
        functools.partial(_proj_a_kernel, layer=layer, n_cast=4 if hand_off else 0, decode_heads=heads),
        grid=(ni, nj),
        in_specs=in_specs,
        out_specs=out_specs,
        out_shape=out_shape,
        compiler_params=_cparams(("parallel", "parallel"), vm),
        name="hgrn2_proj",
    )(*args)
    cast = tuple(outs[5:9]) if hand_off else None
    a_dec = outs[-1].reshape(Bd * Ld, D) if decode is not None else None
    return tuple(outs[:5]) + (cast, a_dec)


def _mm_res_kernel(a_ref, w_ref, x_ref, pw_ref, nw_ref, ox_ref, oh_ref):
    tm = a_ref.shape[0]
    nsplit = MM_RES_SPLIT if tm % (8 * MM_RES_SPLIT) == 0 and tm // MM_RES_SPLIT >= 128 else 1
    for p in range(nsplit):
        rows = slice(p * tm // nsplit, (p + 1) * tm // nsplit)
        m = _dot(a_ref[rows, :].astype(BF16), w_ref[...])
        y = x_ref[rows, :] + _rms(m, pw_ref[...])
        ox_ref[rows, :] = y
        oh_ref[rows, :] = _rms(y, nw_ref[...]).astype(oh_ref.dtype)


def _mm_res(a, w, wl, x, post_w, next_w, layer, tm=512):
    M, K = a.shape
    D = w.shape[2]
    tm = min(tm, M)
    vm = 2 * (_nbytes((tm, K), a.dtype) + _nbytes((K, D), BF16) + 2 * _nbytes((tm, D), F32)
              + _nbytes((tm, D), BF16)) + 2 * _nbytes((tm, D), F32)
    return pl.pallas_call(
        _mm_res_kernel,
        grid=(M // tm,),
        in_specs=[pl.BlockSpec((tm, K), lambda i: (i, 0)),
                  pl.BlockSpec((None, K, D), lambda i: (wl, 0, 0)),
                  pl.BlockSpec((tm, D), lambda i: (i, 0)), _row(layer)(D), _row(layer)(D)],
        out_specs=[pl.BlockSpec((tm, D), lambda i: (i, 0)), pl.BlockSpec((tm, D), lambda i: (i, 0))],
        out_shape=[jax.ShapeDtypeStruct((M, D), F32), jax.ShapeDtypeStruct((M, D), BF16)],
        compiler_params=_cparams(("parallel",), vm),
        name="matmul_norm_residual",
    )(a, w, x, post_w, next_w)


def _mlp_kernel(h_ref, wu_ref, wd_ref, x_ref, pw_ref, nw_ref, ox_ref, *rest, emit_next, hand_off):
    acc_ref = rest[-1]
    f_axis = 0 if hand_off else 1
    f = pl.program_id(f_axis)
    i = pl.program_id(1 - f_axis)
    ai = i if hand_off else 0

    @pl.when(f == 0)
    def _():
        acc_ref[ai] = jnp.zeros(acc_ref.shape[1:], F32)

    if hand_off:
        wu_ref_b, wd_ref_b = rest[-3], rest[-2]

        @pl.when(i == 0)
        def _():
            wu_ref_b[...] = wu_ref[...].astype(BF16)
            wd_ref_b[...] = wd_ref[...].astype(BF16)
    else:
        wu_ref_b, wd_ref_b = wu_ref, wd_ref
    tm = ox_ref.shape[0]
    h = h_ref[pl.ds(pl.multiple_of(i * tm, tm), tm), :] if hand_off else h_ref[...]
    u = jnp.maximum(_dot(h, wu_ref_b[...]), 0.0)
    acc_ref[ai] += _dot((u * u).astype(BF16), wd_ref_b[...])

    @pl.when(f == pl.num_programs(f_axis) - 1)
    def _():
        y = x_ref[...] + _rms(acc_ref[ai], pw_ref[...])
        ox_ref[...] = y
        if emit_next:
            rest[0][...] = _rms(y, nw_ref[...]).astype(BF16)


def _mlp(h, w_up, w_down, wl, x, post_w, next_w, layer, tm=512, tf=1024):
    M, D = h.shape
    FF = w_up.shape[2]
    tm = min(tm, M)
    hand_off = w_up.dtype != BF16
    if hand_off:
        tf = min(tf, 512)
    emit_next = layer + 1 < next_w.shape[0]
    nt, nf = M // tm, FF // tf
    if hand_off:
        ix = lambda fn: (lambda f, i: fn(i, f))
        last = lambda i, f: (jnp.where(f == nf - 1, i, 0), 0)
        grid, sem, n_acc = (nf, nt), ("arbitrary", "arbitrary"), nt
    else:
        ix = lambda fn: fn
        last = lambda i, f: (i, 0)
        grid, sem, n_acc = (nt, nf), ("parallel", "arbitrary"), 1
    tile = lambda: pl.BlockSpec((tm, D), ix(last))
    out_specs = [tile()] + ([tile()] if emit_next else [])
    out_shape = [jax.ShapeDtypeStruct((M, D), F32)] + ([jax.ShapeDtypeStruct((M, D), BF16)] if emit_next else [])
    vm = 2 * (2 * _nbytes((tm, D), BF16) + 2 * _nbytes((D, tf), w_up.dtype) + 2 * _nbytes((tm, D), F32))
    vm += (1 + n_acc) * _nbytes((tm, D), F32) + 2 * _nbytes((tm, tf), F32)
    if hand_off:
        out_specs += [pl.BlockSpec((None, D, tf), ix(lambda i, f: (0, 0, f))),
                      pl.BlockSpec((None, tf, D), ix(lambda i, f: (0, f, 0)))]
        out_shape += [jax.ShapeDtypeStruct((1, D, FF), BF16), jax.ShapeDtypeStruct((1, FF, D), BF16)]
        vm += 4 * _nbytes((D, tf), BF16)
    outs = pl.pallas_call(
        functools.partial(_mlp_kernel, emit_next=emit_next, hand_off=hand_off),
        grid=grid,
        in_specs=[pl.BlockSpec((M, D), lambda f, i: (0, 0)) if hand_off
                  else pl.BlockSpec((tm, D), lambda i, f: (i, 0)),
                  pl.BlockSpec((None, D, tf), ix(lambda i, f: (wl, 0, f))),
                  pl.BlockSpec((None, tf, D), ix(lambda i, f: (wl, f, 0))),
                  tile(), _row(layer)(D), _row(layer + 1 if emit_next else layer)(D)],
        out_specs=out_specs,
        out_shape=out_shape,
        scratch_shapes=[pltpu.VMEM((n_acc, tm, D), F32)],
        compiler_params=_cparams(sem, vm),
        name="relu2_mlp",
    )(h, w_up, w_down, x, post_w, next_w)
    n = 2 if emit_next else 1
    return outs[0], (outs[1] if emit_next else None), (tuple(outs[n:]) if hand_off else None)


def _split3(x):
    hi = x.astype(BF16)
    r1 = x - hi.astype(F32)
    mid = r1.astype(BF16)
    lo = (r1 - mid.astype(F32)).astype(BF16)
    return hi, mid, lo


def _decay_rows(d, rows):
    hi, mid, lo = _split3(d)
    r = lax.broadcasted_iota(jnp.int32, (rows, d.shape[-1]), 0)
    return jnp.where(r == 0, hi.astype(F32), jnp.where(r == 1, mid.astype(F32),
                                                       jnp.where(r == 2, lo.astype(F32), 0.0)))


def _gla_prompt_kernel(*refs, decode_heads=None):
    if decode_heads is None:
        (q_ref, k_ref, lf_ref, v_ref, g_ref, nw_ref, sel_ref, lvl_ref, o_ref, sout_ref,
         s_scr, qs_scr, ks_scr, ke_scr, fac_scr, a_scr) = refs
    else:
        (q_ref, k_ref, lf_ref, v_ref, g_ref, nw_ref, sel_ref, lvl_ref, dq_ref, dk_ref, dv_ref,
         o_ref, sout_ref, do_ref, s_scr, qs_scr, ks_scr, ke_scr, fac_scr, a_scr) = refs
    T = q_ref.shape[1]
    D = q_ref.shape[2]
    H = D // HEAD_DIM
    SUB = GLA_SUB
    nb = T // SUB
    c_idx = pl.program_id(1)

    @pl.when(c_idx == 0)
    def _():
        s_scr[...] = jnp.zeros_like(s_scr)

    side = iter(()) if decode_heads is None else _attn_decode_pieces(dq_ref, dk_ref, dv_ref, do_ref,
                                                                     heads=decode_heads)
    stage = [0]

    def side_step():
        stage[0] += 1
        if stage[0] % SIDE_EVERY == 0:
            next(side, None)

    chunks = []
    c = 2 * SUB
    while c < T:
        chunks.append(c)
        c *= 2
    nf = 2 * len(chunks) + 3
    sums = _dot(sel_ref[...], jnp.concatenate(_split3(lf_ref[0]), axis=0))
    fac_scr[...] = jnp.exp(sums[T:])
    nq = len(chunks) + 1
    for i in range(nb):
        rows = slice(i * SUB, (i + 1) * SUB)
        r = sums[rows]
        q16 = q_ref[0, rows, :].astype(F32) * jnp.exp(r)
        kd = k_ref[0, rows, :].astype(F32) * jnp.exp(-r)
        f_row = lambda n: fac_scr[n * nb + i:n * nb + i + 1, :]
        qs_scr[0, rows, :] = (q16 * f_row(0)).astype(BF16)
        qs_scr[1, rows, :] = q16.astype(BF16)
        ks_scr[0, rows, :] = kd.astype(BF16)
        ks_scr[1, rows, :] = (kd * f_row(nq)).astype(BF16)
        for n in range(len(chunks)):
            qs_scr[n + 2, rows, :] = (q16 * f_row(n + 1)).astype(BF16)
            ks_scr[n + 2, rows, :] = (kd * f_row(nq + n + 1)).astype(BF16)
        ke_scr[rows, :] = (kd * f_row(nf - 1)).astype(BF16)
        side_step()
    ke_scr[T:, :] = _decay_rows(fac_scr[(nf - 1) * nb:(nf - 1) * nb + 1, :], SUB).astype(BF16)

    rr = lax.broadcasted_iota(jnp.int32, (SUB, 2 * HEAD_DIM), 0)
    cc = lax.broadcasted_iota(jnp.int32, (SUB, 2 * HEAD_DIM), 1)
    pick = jnp.where((rr < 3) & (cc >= HEAD_DIM), 1.0, 0.0).astype(BF16)
    zeros = jnp.zeros((T, HEAD_DIM), BF16)
    nw = nw_ref[...]

    lvl = lvl_ref[...]
    for h in range(H):
        hs = slice(h * HEAD_DIM, (h + 1) * HEAD_DIM)
        r = _dot_nt(qs_scr[1, :, hs], ks_scr[0:2, :, hs].reshape(2 * T, HEAD_DIM))
        a = jnp.where(lvl == 1, r[:, :T], jnp.where(lvl == 2, r[:, T:], 0.0))
        for n in range(len(chunks)):
            a = jnp.where(lvl == n + 3, _dot_nt(qs_scr[n + 2, :, hs], ks_scr[n + 2, :, hs]), a)
        a_scr[h] = a.astype(BF16)
        side_step()
    for h in range(H):
        hs = slice(h * HEAD_DIM, (h + 1) * HEAD_DIM)
        vh = v_ref[0, :, hs]
        s_old = s_scr[h]
        o = _dot(jnp.concatenate([a_scr[h], qs_scr[0, :, hs]], axis=1),
                 jnp.concatenate([vh, s_old.astype(BF16)], axis=0))
        rhs = jnp.concatenate([jnp.concatenate([vh, zeros], axis=1), pick], axis=0)
        upd = _dot_tn(ke_scr[:, hs], rhs)
        s_scr[h] = s_old * upd[:, HEAD_DIM:] + upd[:, :HEAD_DIM]
        o_ref[0, :, hs] = (_rms(o, nw) * g_ref[0, :, hs].astype(F32)).astype(o_ref.dtype)
        side_step()
    for _ in side:
        pass

    @pl.when(c_idx == pl.num_programs(1) - 1)
    def _():
        sout_ref[0] = s_scr[...]


def _gla_range_matrix(T, SUB):
    chunks = []
    c = 2 * SUB
    while c < T:
        chunks.append(c)
        c *= 2
    nb = T // SUB
    t = np.arange(T)
    rows = [((t[None, :] <= t[:, None]) & (t[None, :] // SUB == t[:, None] // SUB))]
    start = np.arange(nb)[:, None] * SUB
    for C in [T] + chunks:
        rows.append((t[None, :] >= (start // C) * C) & (t[None, :] < start))
    for C in [SUB] + chunks + [T]:
        rows.append((t[None, :] >= start) & (t[None, :] < (start // C + 1) * C))
    sel = np.concatenate(rows, axis=0).astype(np.float32)
    row, col = t[:, None], t[None, :]
    lvl = np.where((row // SUB == col // SUB) & (col <= row), 1, 0)
    for j, C in enumerate([SUB] + chunks):
        lvl = np.where(((row // C) % 2 == 1) & (col // C == row // C - 1), 2 + j, lvl)
    return sel, lvl.astype(np.int32), len(chunks)


def _gla_prompt(q, k, lf, v, g, norm_w, mixer, B, L, decode=None):
    D = q.shape[-1]
    H = D // HEAD_DIM
    T = GLA_BLOCK
    assert T == HEAD_DIM
    nc = L // T
    sel, lvl, nchunks = _gla_range_matrix(T, GLA_SUB)
    sel3 = jnp.asarray(np.concatenate([sel] * 3, axis=1), BF16)
    nfac = sel.shape[0] - T
    r3 = lambda a: a.reshape(B, L, D)
    blk = lambda: pl.BlockSpec((1, T, D), lambda b, c: (b, c, 0))
    const = lambda a: pl.BlockSpec(a.shape, lambda b, c: (0, 0))
    vm = 2 * (5 * _nbytes((T, D), BF16) + _nbytes((T, D), F32)) + 3 * _nbytes((H, HEAD_DIM, HEAD_DIM), F32)
    vm += (2 * nchunks + 6) * _nbytes((T, D), BF16) + 8 * _nbytes((T, D), F32)
    in_specs = [blk(), blk(), blk(), blk(), blk(), _row(mixer)(HEAD_DIM), const(sel3), const(lvl)]
    args = [r3(q), r3(k), r3(lf), r3(v), r3(g), norm_w, sel3, jnp.asarray(lvl)]
    out_specs = [blk(), pl.BlockSpec((1, H, HEAD_DIM, HEAD_DIM), lambda b, c: (b, 0, 0, 0))]
    out_shape = [jax.ShapeDtypeStruct((B, L, D), BF16), jax.ShapeDtypeStruct((B, H, HEAD_DIM, HEAD_DIM), F32)]
    heads = None
    if decode is not None:
        dq, ck, cv, layer, Bd, Ld, heads, d_dtype = decode
        assert Ld == SUBL and SUBL % heads == 0 and Bd % (B * nc) == 0
        bb = Bd // (B * nc)
        kv_shape = (bb,) + ck.shape[2:]
        kv_spec = lambda: pl.BlockSpec((None,) + kv_shape, lambda b, c: (layer, b * nc + c, 0, 0, 0, 0))
        dblk = lambda: pl.BlockSpec((bb, Ld, D), lambda b, c: (b * nc + c, 0, 0))
        in_specs += [dblk(), kv_spec(), kv_spec()]
        args += [dq.reshape(Bd, Ld, D), ck, cv]
        out_specs.append(dblk())
        out_shape.append(jax.ShapeDtypeStruct((Bd, Ld, D), d_dtype))
        vm += 4 * _nbytes((bb, Ld, D), F32) + 4 * _nbytes(kv_shape, ck.dtype)
        vm += 4 * _nbytes(kv_shape[1:], BF16) + 6 * _nbytes((SUBL * Ld, ck.shape[2] * SUBL), F32)
    outs = pl.pallas_call(
        functools.partial(_gla_prompt_kernel, decode_heads=heads),
        grid=(B, nc),
        in_specs=in_specs,
        out_specs=out_specs,
        out_shape=out_shape,
        scratch_shapes=[pltpu.VMEM((H, HEAD_DIM, HEAD_DIM), F32),
                        pltpu.VMEM((nchunks + 2, T, D), BF16),
                        pltpu.VMEM((nchunks + 2, T, D), BF16),
                        pltpu.VMEM((T + GLA_SUB, D), BF16),
                        pltpu.VMEM((nfac, D), F32),
                        pltpu.VMEM((H, T, T), BF16)],
        compiler_params=_cparams(("parallel", "arbitrary"), vm),
        name="hgrn2_recurrence_prompt",
    )(*args)
    a_dec = outs[2].reshape(Bd * Ld, D) if decode is not None else None
    return outs[0].reshape(B * L, D), outs[1], a_dec


def _gla_step_kernel(q_ref, k_ref, lf_ref, v_ref, g_ref, s0_ref, nw_ref, o_ref, sout_ref,
                     qs_scr, ks_scr, ke_scr, ds_scr):
    bb, T, D = q_ref.shape
    H = D // HEAD_DIM
    R = bb * T
    pos = lax.broadcasted_iota(jnp.int32, (bb, T, D), 1)
    b = lf_ref[...]
    sh = 1
    while sh < T:
        b = b + jnp.where(pos >= sh, pltpu.roll(b, sh, 1), 0.0)
        sh *= 2
    b_last = b[:, T - 1:T, :]
    k = k_ref[...]
    qs_scr[...] = q_ref[...] * jnp.exp(b)
    ks_scr[...] = k * jnp.exp(-b)
    ke_scr[...] = k * jnp.exp(b_last - b)
    d_hi, d_mid, d_lo = (t.astype(F32) for t in _split3(jnp.exp(b_last)))
    ds_scr[...] = jnp.where(pos == 0, d_hi, jnp.where(pos == 1, d_mid, jnp.where(pos == 2, d_lo, 0.0)))

    row = lax.broadcasted_iota(jnp.int32, (R, R), 0)
    col = lax.broadcasted_iota(jnp.int32, (R, R), 1)
    same_seq_causal = (row // T == col // T) & (col <= row)
    rr = lax.broadcasted_iota(jnp.int32, (T, 2 * HEAD_DIM), 0)
    cc = lax.broadcasted_iota(jnp.int32, (T, 2 * HEAD_DIM), 1)
    pick = jnp.where((rr < 3) & (cc >= HEAD_DIM), 1.0, 0.0)
    zeros = jnp.zeros((T, HEAD_DIM), F32)
    nw = nw_ref[...]

    for h in range(H):
        hs = slice(h * HEAD_DIM, (h + 1) * HEAD_DIM)
        qd = qs_scr[:, :, hs].reshape(R, HEAD_DIM)
        kd = ks_scr[:, :, hs].reshape(R, HEAD_DIM)
        vh = v_ref[:, :, hs].reshape(R, HEAD_DIM)
        a = jnp.where(same_seq_causal, _dot_nt(qd.astype(BF16), kd.astype(BF16)), 0.0)
        o_intra = _dot(a.astype(BF16), vh.astype(BF16))
        outs = []
        for i in range(bb):
            s_old = s0_ref[i, h]
            outs.append(o_intra[i * T:(i + 1) * T]
                        + _dot(qd[i * T:(i + 1) * T].astype(BF16), s_old.astype(BF16)))
            lhs = jnp.concatenate([ke_scr[i, :, hs], ds_scr[i, :, hs]], axis=0).astype(BF16)
            rhs = jnp.concatenate([jnp.concatenate([vh[i * T:(i + 1) * T], zeros], axis=1), pick],
                                  axis=0).astype(BF16)
            upd = _dot_tn(lhs, rhs)
            sout_ref[i, h] = s_old * upd[:, HEAD_DIM:] + upd[:, :HEAD_DIM]
        o = jnp.concatenate(outs, axis=0)
        on = _rms(o, nw) * g_ref[:, :, hs].reshape(R, HEAD_DIM)
        o_ref[:, :, hs] = on.reshape(bb, T, HEAD_DIM).astype(o_ref.dtype)


def _gla_step(q, k, lf, v, g, s0, norm_w, mixer, B, L, bb=8):
    D = q.shape[-1]
    H = D // HEAD_DIM
    assert L == 8 and B % bb == 0
    r3 = lambda a: a.reshape(B, L, D)
    blk = lambda: pl.BlockSpec((bb, L, D), lambda b: (b, 0, 0))
    sblk = lambda: pl.BlockSpec((bb, H, HEAD_DIM, HEAD_DIM), lambda b: (b, 0, 0, 0))
    vm = 2 * (6 * _nbytes((bb, L, D), F32) + 2 * _nbytes((bb, H, HEAD_DIM, HEAD_DIM), F32))
    vm += 12 * _nbytes((bb, L, D), F32)
    o, s = pl.pallas_call(
        _gla_step_kernel,
        grid=(B // bb,),
        in_specs=[blk(), blk(), blk(), blk(), blk(),
                  pl.BlockSpec((None, bb, H, HEAD_DIM, HEAD_DIM), lambda b: (mixer, b, 0, 0, 0)),
                  _row(mixer)(HEAD_DIM)],
        out_specs=[blk(), sblk()],
        out_shape=[jax.ShapeDtypeStruct((B, L, D), F32),
                   jax.ShapeDtypeStruct((B, H, HEAD_DIM, HEAD_DIM), F32)],
        scratch_shapes=[pltpu.VMEM((bb, L, D), F32)] * 4,
        compiler_params=_cparams(("parallel",), vm),
        name="hgrn2_recurrence_step",
    )(r3(q), r3(k), r3(lf), r3(v), r3(g), s0, norm_w)
    return o.reshape(B * L, D), s


def _pool_kernel(*refs, pos0, has_halo):
    if has_halo:
        u_ref, halo_ref, buf_ref, pw_ref, ps_ref, o_ref, nb_ref, ext = refs
    else:
        u_ref, buf_ref, pw_ref, ps_ref, o_ref, nb_ref, ext = refs
    bb, tl, D = u_ref.shape
    G = len(POOL_WINDOWS)
    gc = D // G
    l_idx = pl.program_id(1)
    ext[:, HALO:, :] = u_ref[...]
    if has_halo:
        @pl.when(l_idx == 0)
        def _():
            ext[:, :HALO, :] = buf_ref[...]

        @pl.when(l_idx > 0)
        def _():
            ext[:, :HALO, :] = halo_ref[...]
    else:
        ext[:, :HALO, :] = buf_ref[...]

    pos = pos0 + l_idx * tl + lax.broadcasted_iota(jnp.int32, (tl, gc), 0)
    for gi, w in enumerate(POOL_WINDOWS):
        ls = slice(gi * gc, (gi + 1) * gc)
        cnt = jnp.minimum(w, pos + 1).astype(F32)
        rows = []
        for i in range(bb):
            u = ext[i, HALO:HALO + tl, ls]
            s = u
            for d in range(1, w):
                s = s + ext[i, HALO - d:HALO - d + tl, ls]
            rows.append(s / cnt - u)
        pooled = rows[0] if bb == 1 else jnp.concatenate(rows, axis=0)
        mixed = (_dot(pooled.astype(BF16), pw_ref[gi]) * ps_ref[:, ls]).astype(o_ref.dtype)
        for i in range(bb):
            o_ref[i, :, ls] = mixed[i * tl:(i + 1) * tl]

    @pl.when(l_idx == pl.num_programs(1) - 1)
    def _():
        nb_ref[...] = ext[:, tl:tl + HALO, :]


def _pool(u, buf, pos0, pool_w, pool_scale, mixer, B, L, bb, tl, out_dtype):
    D = u.shape[-1]
    G = len(POOL_WINDOWS)
    gc = D // G
    tl = min(tl, L)
    u3 = u.reshape(B, L, D)
    has_halo = L > tl
    step = tl // HALO
    in_specs = [pl.BlockSpec((bb, tl, D), lambda b, l: (b, l, 0))]
    args = [u3]
    if has_halo:
        in_specs.append(pl.BlockSpec((bb, HALO, D), lambda b, l: (b, jnp.maximum(l * step - 1, 0), 0)))
        args.append(u3)
    in_specs += [pl.BlockSpec((None, bb, HALO, D), lambda b, l: (mixer, b, 0, 0)),
                 pl.BlockSpec((None, G, gc, gc), lambda b, l: (mixer, 0, 0, 0)),
                 _row(mixer)(D)]
    args += [buf, pool_w, pool_scale]
    vm = 2 * (_nbytes((bb, tl, D), F32) + 3 * _nbytes((bb, HALO, D), F32) + _nbytes((G, gc, gc), BF16)
              + _nbytes((bb, tl, D), BF16)) + _nbytes((bb, tl + HALO, D), F32) + 4 * _nbytes((bb, tl, D), F32)
    mixed, newbuf = pl.pallas_call(
        functools.partial(_pool_kernel, pos0=pos0, has_halo=has_halo),
        grid=(B // bb, L // tl),
        in_specs=in_specs,
        out_specs=[pl.BlockSpec((bb, tl, D), lambda b, l: (b, l, 0)),
                   pl.BlockSpec((bb, HALO, D), lambda b, l: (b, 0, 0))],
        out_shape=[jax.ShapeDtypeStruct((B, L, D), out_dtype), jax.ShapeDtypeStruct((B, HALO, D), F32)],
        scratch_shapes=[pltpu.VMEM((bb, tl + HALO, D), F32)],
        compiler_params=_cparams(("parallel", "arbitrary"), vm),
        name="causal_pool",
    )(*args)
    return mixed.reshape(B * L, D), newbuf


def _attn_kernel(q_ref, k_ref, v_ref, o_ref, *, heads):
    bb, tq, D = q_ref.shape
    hd = D // heads
    scale = hd ** -0.5
    for i in range(bb):
        for h in range(heads):
            hs = slice(h * hd, (h + 1) * hd)
            s = _dot_nt(q_ref[i, :, hs].astype(BF16), k_ref[i, :, hs].astype(BF16)) * scale
            p = jnp.exp(s - jnp.max(s, axis=-1, keepdims=True))
            den = jnp.sum(p, axis=-1, keepdims=True)
            o_ref[i, :, hs] = (_dot(p.astype(BF16), v_ref[i, :, hs].astype(BF16)) / den).astype(o_ref.dtype)


def _attn(q, mk, mv, layer, B, L, heads, bb, tq, out_dtype):
    D = q.shape[-1]
    tq = min(tq, L)
    kv_shape = (bb,) + mk.shape[2:]
    kv_blk = (None,) + kv_shape
    vm = 2 * (_nbytes((bb, tq, D), q.dtype) + 2 * _nbytes(kv_shape, mk.dtype) + _nbytes((bb, tq, D), out_dtype))
    vm += 2 * _nbytes(kv_shape, BF16) + 8 * _nbytes((tq, D // heads), F32)
    o = pl.pallas_call(
        functools.partial(_attn_kernel, heads=heads),
        grid=(B // bb, L // tq),
        in_specs=[pl.BlockSpec((bb, tq, D), lambda b, l: (b, l, 0)),
                  pl.BlockSpec(kv_blk, lambda b, l: (layer, b, 0, 0)),
                  pl.BlockSpec(kv_blk, lambda b, l: (layer, b, 0, 0))],
        out_specs=pl.BlockSpec((bb, tq, D), lambda b, l: (b, l, 0)),
        out_shape=jax.ShapeDtypeStruct((B, L, D), out_dtype),
        compiler_params=_cparams(("parallel", "parallel"), vm),
        name="memory_cross_attention",
    )(q.reshape(B, L, D), mk, mv)
    return o.reshape(B * L, D)


def _attn_decode_pieces(q_ref, k_ref, v_ref, o_ref, *, heads):
    bb, tq, D = q_ref.shape
    n_mem, nu = k_ref.shape[1], k_ref.shape[2]
    hd = D // heads
    per = SUBL // heads
    W = n_mem * SUBL
    scale = hd ** -0.5
    res = lax.broadcasted_iota(jnp.int32, (tq, W), 1) % SUBL
    tile = lambda r: jnp.concatenate([r] * (W // LANES), axis=1)

    def feat(j):
        c, h = divmod(j, heads)
        return slice(h * hd + c * LANES, h * hd + (c + 1) * LANES)

    def same_row_reduce(x, op):
        r = x[:, :LANES]
        for j in range(1, W // LANES):
            r = op(r, x[:, j * LANES:(j + 1) * LANES])
        sh = SUBL
        while sh < LANES:
            r = op(r, pltpu.roll(r, sh, 1))
            sh *= 2
        return r

    for i in range(bb):
        w = None
        for u in range(nu):
            slab = k_ref[i, :, u].reshape(W, LANES).astype(BF16)
            qm = jnp.concatenate([q_ref[i, :, feat(u * SUBL + s)] for s in range(SUBL)], axis=0)
            p = _dot_nt(qm.astype(BF16), slab)
            z = p[:tq]
            for s in range(1, SUBL):
                z = jnp.where(res == s, p[s * tq:(s + 1) * tq], z)
            w = z if w is None else w + z
            yield
        sc = w
        for c in range(1, per):
            sc = sc + pltpu.roll(w, W - c * heads, 1)
        sc = sc * scale
        e = jnp.exp(sc - tile(same_row_reduce(sc, jnp.maximum)))
        pn = e / tile(same_row_reduce(e, jnp.add))
        blocks = []
        for s in range(SUBL):
            c_lo = s // heads
            src = pn if c_lo == 0 else pltpu.roll(pn, c_lo * heads, 1)
            blocks.append(jnp.where(res == s, src, 0.0))
        pm = jnp.concatenate(blocks, axis=0).astype(BF16)
        yield
        for u in range(nu):
            o = _dot(pm, v_ref[i, :, u].reshape(W, LANES).astype(BF16))
            for s in range(SUBL):
                o_ref[i, :, feat(u * SUBL + s)] = o[s * tq:(s + 1) * tq].astype(o_ref.dtype)
            yield


def _attn_decode_kernel(q_ref, k_ref, v_ref, o_ref, *, heads):
    for _ in _attn_decode_pieces(q_ref, k_ref, v_ref, o_ref, heads=heads):
        pass


def _attn_decode(q, ck, cv, layer, B, L, heads, bb, out_dtype):
    D = q.shape[-1]
    assert L == SUBL and SUBL % heads == 0 and D % (SUBL * LANES) == 0
    kv_shape = (bb,) + ck.shape[2:]
    vm = 2 * (2 * _nbytes((bb, L, D), F32) + 2 * _nbytes(kv_shape, ck.dtype))
    vm += 4 * _nbytes(kv_shape[1:], BF16) + 6 * _nbytes((SUBL * L, ck.shape[2] * SUBL), F32)
    o = pl.pallas_call(
        functools.partial(_attn_decode_kernel, heads=heads),
        grid=(B // bb,),
        in_specs=[pl.BlockSpec((bb, L, D), lambda b: (b, 0, 0)),
                  pl.BlockSpec((None,) + kv_shape, lambda b: (layer, b, 0, 0, 0, 0)),
                  pl.BlockSpec((None,) + kv_shape, lambda b: (layer, b, 0, 0, 0, 0))],
        out_specs=pl.BlockSpec((bb, L, D), lambda b: (b, 0, 0)),
        out_shape=jax.ShapeDtypeStruct((B, L, D), out_dtype),
        compiler_params=_cparams(("parallel",), vm),
        name="memory_cross_attention_decode",
    )(q.reshape(B, L, D), ck, cv)
    return o.reshape(B * L, D)


def _mem_proj_kernel(x_ref, nw_ref, w_ref, flat_ref, bf_ref, w_scr, *, heads):
    tm, D = x_ref.shape
    R = D // LANES
    nchunk = R // heads

    @pl.when(pl.program_id(1) == 0)
    def _():
        w_scr[...] = w_ref[...].astype(BF16)

    acc = _dot(_rms(x_ref[...], nw_ref[...]).astype(BF16), w_scr[...])
    bf_ref[...] = acc.astype(bf_ref.dtype)
    for h in range(heads):
        for c in range(nchunk):
            j = h * nchunk + c
            flat_ref[pl.ds(c * heads + h, tm, stride=R), :] = acc[:, j * LANES:(j + 1) * LANES]


def _mem_proj(mem, norm_w, w, half, heads, tm=256):
    M, D = mem.shape
    depth = w.shape[0]
    R = D // LANES
    tm = min(tm, M)
    nt = M // tm
    vm = 2 * (_nbytes((tm, D), F32) + _nbytes((D, D), F32) + _nbytes((tm, D), F32) + _nbytes((tm, D), BF16))
    vm += _nbytes((D, D), BF16) + 2 * _nbytes((tm, D), F32)
    return pl.pallas_call(
        functools.partial(_mem_proj_kernel, heads=heads),
        grid=(depth, nt),
        in_specs=[pl.BlockSpec((tm, D), lambda l, i: (i, 0)),
                  pl.BlockSpec((None, 1, D), lambda l, i: (l, 0, 0)),
                  pl.BlockSpec((None, D, D), lambda l, i: (l, 0, half))],
        out_specs=[pl.BlockSpec((tm * R, LANES), lambda l, i: (l * nt + i, 0)),
                   pl.BlockSpec((None, tm, D), lambda l, i: (l, i, 0))],
        out_shape=[jax.ShapeDtypeStruct((depth * M * R, LANES), F32),
                   jax.ShapeDtypeStruct((depth, M, D), BF16)],
        scratch_shapes=[pltpu.VMEM((D, D), BF16)],
        compiler_params=_cparams(("parallel", "arbitrary"), vm),
        name="memory_kv_projection",
    )(mem, norm_w, w)


class _Group:
    def __init__(self, x, B, L, pos0, S_in, buf_in, mk, mv, prompt, p):
        self.B, self.L, self.pos0, self.S_in, self.buf_in, self.mk, self.mv = B, L, pos0, S_in, buf_in, mk, mv
        self.prompt = prompt
        self.act = BF16 if prompt else F32
        self.x = x
        self.hn = _rmsnorm(x, p["norm_mix_pre"], 0)
        self.S_out, self.buf_out = [], []


class _NoRider:
    def job(self):
        return None

    def served(self, a):
        pass


def _mixer(g, l, p, rider=_NoRider()):
    j = l // 2
    if l % 2 == 0:
        w, wl = _weight(p, "w_in_a", j)
        q, k, lf, v, gate, cast, a = _proj_a(g.hn, w, wl, p["hg_lb_logits"], l, g.act,
                                             rider.job() if g.prompt else None)
        _hand_off(p, "w_in_a", j, cast)
        rider.served(a)
        if g.prompt:
            o, S, a = _gla_prompt(q, k, lf, v, gate, p["hg_norm"], j, g.B, g.L, rider.job())
            rider.served(a)
        else:
            o, S = _gla_step(q, k, lf, v, gate, g.S_in, p["hg_norm"], j, g.B, g.L)
        g.S_out.append(S)
        g.x, g.hn = _mm_res(o, p["w_out_a"], j, g.x, p["norm_mix_post"], p["norm_x_pre"], l)
    else:
        u = _project(g, p, "w_in_b", j, F32)
        bb, tl = (1, 512) if g.prompt else (16, g.L)
        mixed, nb = _pool(u, g.buf_in, g.pos0, p["pool_w"], p["pool_scale"], j, g.B, g.L, bb, tl, g.act)
        g.buf_out.append(nb[:, 1:, :])
        g.x, g.hn = _mm_res(mixed, p["w_out_b"], j, g.x, p["norm_mix_post"], p["norm_x_pre"], l)


def _weight(p, name, idx):
    cast = p["cast"].get((name, idx))
    return (cast, 0) if cast is not None else (p[name], idx)


def _hand_off(p, name, idx, cast):
    if cast is not None:
        p["cast"][(name, idx)] = cast


def _project(g, p, name, idx, dtype):
    w, wl = _weight(p, name, idx)
    outs = _matmul(g.hn, w, wl, [dtype])
    _hand_off(p, name, idx, outs[1] if len(outs) > 1 else None)
    return outs[0]


def _after_attention(g, l, a, p):
    g.x, g.hn = _mm_res(a, p["w_xo"], l, g.x, p["norm_x_post"], p["norm_mlp_pre"], l)
    (wu, wl), (wd, _) = _weight(p, "w_up", l), _weight(p, "w_down", l)
    g.x, g.hn, cast = _mlp(g.hn, wu, wd, wl, g.x, p["norm_mlp_post"], p["norm_mix_pre"], l)
    if cast is not None:
        _hand_off(p, "w_up", l, cast[0])
        _hand_off(p, "w_down", l, cast[1])


class _SampleWalk:
    def __init__(self, g, p):
        self.g, self.p = g, p
        self._walk = self._layers()
        self._pending = next(self._walk, None)

    def _layers(self):
        g, p = self.g, self.p
        for l in range(p["w_xq"].shape[0]):
            _mixer(g, l, p)
            q = _project(g, p, "w_xq", l, g.act)
            a = yield (q, g.mk, g.mv, l, g.B, g.L, p["mem_heads"], g.act)
            _after_attention(g, l, a, p)

    def job(self):
        return self._pending

    def served(self, a):
        if a is not None:
            try:
                self._pending = self._walk.send(a)
            except StopIteration:
                self._pending = None

    def serve_standalone(self):
        q, mk, mv, l, B, L, heads, dtype = self._pending
        self.served(_attn_decode(q, mk, mv, l, B, L, heads, 4, dtype))


def _trunks(gp, gs, p):
    heads = p["mem_heads"]
    rider = _SampleWalk(gs, p)
    for l in range(p["w_xq"].shape[0]):
        if l % 2 == 1 and rider.job() is not None:
            rider.serve_standalone()
        _mixer(gp, l, p, rider)
        qp = _project(gp, p, "w_xq", l, gp.act)
        _after_attention(gp, l, _attn(qp, gp.mk, gp.mv, l, gp.B, gp.L, heads, 1, 512, gp.act), p)
    while rider.job() is not None:
        rider.serve_standalone()


def kernel(x_prompt, x_sample, state_hgrn, state_pool, cache_mem_k, cache_mem_v, mem_prompt, w_in_a, hg_lb_logits, hg_norm, w_out_a, w_in_b, pool_w, pool_scale, w_out_b, norm_mem, w_xq, w_xkv, w_xo, norm_mix_pre, norm_mix_post, norm_x_pre, norm_x_post, norm_mlp_pre, norm_mlp_post, w_up, w_down):
    B, L, D = x_prompt.shape
    Bs, Ls, _ = x_sample.shape
    depth = w_xq.shape[0]
    n_mem, heads = cache_mem_k.shape[2], cache_mem_k.shape[3]
    assert L % GLA_BLOCK == 0 and Ls <= GLA_SUB and D % (HEAD_DIM * len(POOL_WINDOWS)) == 0
    assert state_pool.shape[2] == HALO - 1
    bf = lambda a: a.astype(BF16)
    rows = lambda a: a.reshape(a.shape[0], 1, a.shape[1])
    p = dict(w_in_a=w_in_a, hg_lb_logits=hg_lb_logits, hg_norm=rows(hg_norm), w_out_a=bf(w_out_a),
             w_in_b=w_in_b, pool_w=bf(pool_w), pool_scale=rows(pool_scale), w_out_b=bf(w_out_b),
             w_xq=w_xq, w_xo=bf(w_xo), norm_mix_pre=rows(norm_mix_pre), norm_mix_post=rows(norm_mix_post),
             norm_x_pre=rows(norm_x_pre), norm_x_post=rows(norm_x_post), norm_mlp_pre=rows(norm_mlp_pre),
             norm_mlp_post=rows(norm_mlp_post), w_up=w_up, w_down=w_down, mem_heads=heads, cast={})
    R = D // LANES
    nchunk = R // heads

    def cache_view(flat, nb):
        c = flat.reshape(depth, nb, n_mem, nchunk, heads, LANES)
        return jnp.swapaxes(c, 3, 4).reshape(depth, nb, n_mem, heads, D // heads)

    def flat_view(cache):
        nb = cache.shape[1]
        c = cache.reshape(depth, nb, n_mem, heads, nchunk, LANES)
        return jnp.swapaxes(c, 3, 4).reshape(depth, nb, n_mem, R // SUBL, SUBL, LANES)

    mem2d = mem_prompt.reshape(B * n_mem, D)
    k_flat, k_bf = _mem_proj(mem2d, rows(norm_mem), w_xkv, 0, heads)
    v_flat, v_bf = _mem_proj(mem2d, rows(norm_mem), w_xkv, 1, heads)
    cache_mem_k_prompt = cache_view(k_flat, B)
    cache_mem_v_prompt = cache_view(v_flat, B)
    mk_bf = k_bf.reshape(depth, B, n_mem, D)
    mv_bf = v_bf.reshape(depth, B, n_mem, D)

    n_b = state_pool.shape[0]
    buf0 = jnp.zeros((n_b, B, HALO, D), F32)
    buf_s = jnp.pad(state_pool, ((0, 0), (0, 0), (1, 0), (0, 0)))
    gp = _Group(x_prompt.reshape(B * L, D), B, L, 0, None, buf0, mk_bf, mv_bf, True, p)
    gs = _Group(x_sample.reshape(Bs * Ls, D), Bs, Ls, PAST_LEN, state_hgrn, buf_s,
                flat_view(cache_mem_k), flat_view(cache_mem_v), False, p)
    _trunks(gp, gs, p)
    return (gp.x.reshape(B, L, D), gs.x.reshape(Bs, Ls, D), jnp.stack(gp.S_out), jnp.stack(gp.buf_out),
            cache_mem_k_prompt, cache_mem_v_prompt, jnp.stack(gs.S_out), jnp.stack(gs.buf_out))
```

```python
import functools

import numpy as np

import jax
import jax.numpy as jnp
from jax import lax
from jax.experimental import pallas as pl
from jax.experimental.pallas import tpu as pltpu

F32 = jnp.float32
BF16 = jnp.bfloat16
EPS = 1e-6
PAST_LEN = 16384
POOL_WINDOWS = (2, 4, 8, 16)
LANES = 128
SUBL = 8
HEAD_DIM = 128
GLA_BLOCK = 128
GLA_SUB = 16
HALO = 16
MM_RES_SPLIT = 4
SIDE_EVERY = 4
PROJ_SIDE_PIECES = 2

V7X_VMEM_BYTES = 64 * 1024 * 1024
VMEM_CAP = V7X_VMEM_BYTES - 6 * 1024 * 1024


def _cparams(sem, vmem_bytes):
    return pltpu.CompilerParams(dimension_semantics=sem,
                                vmem_limit_bytes=int(min(max(vmem_bytes, 16 * 1024 * 1024), VMEM_CAP)))


def _nbytes(shape, dtype):
    n = 1
    for s in shape:
        n *= s
    return n * jnp.dtype(dtype).itemsize


def _rms(v, w):
    ms = jnp.mean(v * v, axis=-1, keepdims=True)
    return v * lax.rsqrt(ms + EPS) * w


def _sigmoid(x):
    return 1.0 / (1.0 + jnp.exp(-x))


def _dot(a, b):
    return jnp.dot(a, b, preferred_element_type=F32)


def _dot_nt(a, b):
    return lax.dot_general(a, b, (((1,), (1,)), ((), ())), preferred_element_type=F32)


def _dot_tn(a, b, precision=None):
    return lax.dot_general(a, b, (((0,), (0,)), ((), ())), preferred_element_type=F32,
                           precision=precision)


def _rmsnorm_kernel(x_ref, w_ref, o_ref):
    o_ref[...] = _rms(x_ref[...], w_ref[...]).astype(o_ref.dtype)


def _row(layer):
    return lambda width: pl.BlockSpec((None, 1, width), lambda *_: (layer, 0, 0))


def _rmsnorm(x, w, layer, tm=512):
    M, D = x.shape
    tm = min(tm, M)
    return pl.pallas_call(
        _rmsnorm_kernel,
        grid=(M // tm,),
        in_specs=[pl.BlockSpec((tm, D), lambda i: (i, 0)), _row(layer)(D)],
        out_specs=pl.BlockSpec((tm, D), lambda i: (i, 0)),
        out_shape=jax.ShapeDtypeStruct((M, D), BF16),
        compiler_params=_cparams(("parallel",), 4 * _nbytes((tm, D), F32)),
        name="rmsnorm",
    )(x, w)


def _mm_kernel(a_ref, w_ref, *o_refs, hand_off):
    w = w_ref[...].astype(BF16)
    if hand_off:
        o_refs[-1][...] = w
        o_refs = o_refs[:-1]
    acc = _dot(a_ref[...], w)
    for o_ref in o_refs:
        o_ref[...] = acc.astype(o_ref.dtype)


def _matmul(a, w, layer, out_dtypes, tm=1024, tn=1024):
    M, K = a.shape
    N = w.shape[2]
    tm, tn = min(tm, M), min(tn, N)
    hand_off = w.dtype != BF16
    assert not hand_off or M == tm
    vm = 2 * (_nbytes((tm, K), BF16) + _nbytes((K, tn), w.dtype)) + _nbytes((tm, tn), F32)
    vm += sum(2 * _nbytes((tm, tn), d) for d in out_dtypes)
    out_specs = [pl.BlockSpec((tm, tn), lambda i, j: (i, j)) for _ in out_dtypes]
    out_shape = [jax.ShapeDtypeStruct((M, N), d) for d in out_dtypes]
    if hand_off:
        out_specs.append(pl.BlockSpec((None, K, tn), lambda i, j: (0, 0, j)))
        out_shape.append(jax.ShapeDtypeStruct((1, K, N), BF16))
        vm += 3 * _nbytes((K, tn), BF16)
    outs = pl.pallas_call(
        functools.partial(_mm_kernel, hand_off=hand_off),
        grid=(M // tm, N // tn),
        in_specs=[pl.BlockSpec((tm, K), lambda i, j: (i, 0)),
                  pl.BlockSpec((None, K, tn), lambda i, j: (layer, 0, j))],
        out_specs=out_specs,
        out_shape=out_shape,
        compiler_params=_cparams(("parallel", "parallel"), vm),
        name="matmul",
    )(a, w)
    return outs


def _proj_a_kernel(*refs, layer, n_cast, decode_heads, prenorm):
    h_ref, wq_ref, wf_ref, wi_ref, wg_ref, lbl_ref = refs[:6]
    n_pre = 6 + (1 if prenorm else 0)
    n_in = n_pre + (0 if decode_heads is None else 3)
    q_ref, k_ref, lf_ref, v_ref, g_ref = refs[n_in:n_in + 5]
    w_out_refs = refs[n_in + 5:n_in + 5 + n_cast]
    side = iter(()) if decode_heads is None else _attn_decode_pieces(*refs[n_pre:n_in], refs[-1],
                                                                     heads=decode_heads)

    def side_steps():
        for _ in range(PROJ_SIDE_PIECES):
            next(side, None)

    h = _rms(h_ref[...], refs[6][...]).astype(BF16) if prenorm else h_ref[...]
    ws = [r[...].astype(BF16) for r in (wq_ref, wf_ref, wi_ref, wg_ref)]
    for r, w in zip(w_out_refs, ws):
        r[...] = w
    aq = _dot(h, ws[0])
    q_ref[...] = (aq * _sigmoid(aq)).astype(q_ref.dtype)
    side_steps()
    lg = lbl_ref[...]
    e = jnp.exp(lg - jnp.max(lg, axis=0, keepdims=True))
    lb = jnp.sum(e[:layer + 1], axis=0, keepdims=True) / jnp.sum(e, axis=0, keepdims=True)
    f = lb + (1.0 - lb) * _sigmoid(_dot(h, ws[1]))
    k_ref[...] = (1.0 - f).astype(k_ref.dtype)
    lf_ref[...] = jnp.log(f)
    side_steps()
    v_ref[...] = _dot(h, ws[2]).astype(v_ref.dtype)
    side_steps()
    ag = _dot(h, ws[3])
    g_ref[...] = (ag * _sigmoid(ag)).astype(g_ref.dtype)
    for _ in side:
        pass


def _proj_a(h, w, mixer, lb_logits, layer, act_dtype, decode=None, prenorm=None, tm=1024, tn=512):
    M, D = h.shape
    if decode is not None:
        tm = min(tm, 512)
    tm = min(tm, M)
    parts = isinstance(w, (tuple, list))
    hand_off = not parts and w.dtype != BF16
    if hand_off:
        tn = min(tn, 256)
        assert M == tm
    ni, nj = M // tm, D // tn
    nl = lb_logits.shape[0]
    if parts:
        wspec = lambda s: pl.BlockSpec((None, D, tn), lambda i, j: (0, 0, j))
        w_args, w_dtype = list(w), BF16
    else:
        wspec = lambda s: pl.BlockSpec((None, D, tn), lambda i, j: (mixer, 0, j + s * nj))
        w_args, w_dtype = [w] * 4, w.dtype
    ospec = pl.BlockSpec((tm, tn), lambda i, j: (i, j))
    vm = 2 * (_nbytes((tm, D), h.dtype) + 4 * _nbytes((D, tn), w_dtype) + 5 * _nbytes((tm, tn), F32))
    vm += 4 * _nbytes((tm, tn), F32)
    in_specs = [pl.BlockSpec((tm, D), lambda i, j: (i, 0)), wspec(0), wspec(1), wspec(2), wspec(3),
                pl.BlockSpec((nl, tn), lambda i, j: (0, j))]
    args = [h, *w_args, lb_logits]
    if prenorm is not None:
        in_specs.append(_row(prenorm[1])(D))
        args.append(prenorm[0])
        vm += 2 * _nbytes((tm, D), F32)
    out_specs = [ospec] * 5
    out_shape = [jax.ShapeDtypeStruct((M, D), act_dtype), jax.ShapeDtypeStruct((M, D), act_dtype),
                 jax.ShapeDtypeStruct((M, D), F32), jax.ShapeDtypeStruct((M, D), act_dtype),
                 jax.ShapeDtypeStruct((M, D), act_dtype)]
    if hand_off:
        out_specs += [pl.BlockSpec((None, D, tn), lambda i, j: (0, 0, j))] * 4
        out_shape += [jax.ShapeDtypeStruct((1, D, D), BF16)] * 4
        vm += 12 * _nbytes((D, tn), BF16)
    heads = None
    if decode is not None:
        dq, ck, cv, dlayer, Bd, Ld, heads, d_dtype = decode
        assert Ld == SUBL and SUBL % heads == 0 and Bd % (ni * nj) == 0
        bb = Bd // (ni * nj)
        kv_shape = (bb,) + ck.shape[2:]
        kv_spec = lambda: pl.BlockSpec((None,) + kv_shape, lambda i, j: (dlayer, i * nj + j, 0, 0, 0, 0))
        dblk = lambda: pl.BlockSpec((bb, Ld, D), lambda i, j: (i * nj + j, 0, 0))
        in_specs += [dblk(), kv_spec(), kv_spec()]
        args += [dq.reshape(Bd, Ld, D), ck, cv]
        out_specs.append(dblk())
        out_shape.append(jax.ShapeDtypeStruct((Bd, Ld, D), d_dtype))
        vm += 4 * _nbytes((bb, Ld, D), F32) + 4 * _nbytes(kv_shape, ck.dtype)
        vm += 4 * _nbytes(kv_shape[1:], BF16) + 6 * _nbytes((SUBL * Ld, ck.shape[2] * SUBL), F32)
    outs = pl.pallas_call(
        functools.partial(_proj_a_kernel, layer=layer, n_cast=4 if hand_off else 0, decode_heads=heads,
                          prenorm=prenorm is not None),
        grid=(ni, nj),
        in_specs=in_specs,
        out_specs=out_specs,
        out_shape=out_shape,
        compiler_params=_cparams(("parallel", "parallel"), vm),
        name="hgrn2_proj",
    )(*args)
    cast = tuple(outs[5:9]) if hand_off else None
    a_dec = outs[-1].reshape(Bd * Ld, D) if decode is not None else None
    return tuple(outs[:5]) + (cast, a_dec)


def _mm_res_kernel(a_ref, w_ref, x_ref, pw_ref, nw_ref, ox_ref, oh_ref):
    tm = a_ref.shape[0]
    nsplit = MM_RES_SPLIT if tm % (8 * MM_RES_SPLIT) == 0 and tm // MM_RES_SPLIT >= 128 else 1
    for p in range(nsplit):
        rows = slice(p * tm // nsplit, (p + 1) * tm // nsplit)
        m = _dot(a_ref[rows, :].astype(BF16), w_ref[...])
        y = x_ref[rows, :] + _rms(m, pw_ref[...])
        ox_ref[rows, :] = y
        oh_ref[rows, :] = _rms(y, nw_ref[...]).astype(oh_ref.dtype)


def _mm_res(a, w, wl, x, post_w, next_w, layer, tm=512):
    M, K = a.shape
    D = w.shape[2]
    tm = min(tm, M)
    vm = 2 * (_nbytes((tm, K), a.dtype) + _nbytes((K, D), BF16) + 2 * _nbytes((tm, D), F32)
              + _nbytes((tm, D), BF16)) + 2 * _nbytes((tm, D), F32)
    return pl.pallas_call(
        _mm_res_kernel,
        grid=(M // tm,),
        in_specs=[pl.BlockSpec((tm, K), lambda i: (i, 0)),
                  pl.BlockSpec((None, K, D), lambda i: (wl, 0, 0)),
                  pl.BlockSpec((tm, D), lambda i: (i, 0)), _row(layer)(D), _row(layer)(D)],
        out_specs=[pl.BlockSpec((tm, D), lambda i: (i, 0)), pl.BlockSpec((tm, D), lambda i: (i, 0))],
        out_shape=[jax.ShapeDtypeStruct((M, D), F32), jax.ShapeDtypeStruct((M, D), BF16)],
        compiler_params=_cparams(("parallel",), vm),
        name="matmul_norm_residual",
    )(a, w, x, post_w, next_w)


def _mlp_kernel(h_ref, wu_ref, wd_ref, x_ref, pw_ref, nw_ref, ox_ref, *rest, emit_next, hand_off):
    acc_ref = rest[-1]
    f_axis = 0 if hand_off else 1
    f = pl.program_id(f_axis)
    i = pl.program_id(1 - f_axis)
    ai = i if hand_off else 0

    @pl.when(f == 0)
    def _():
        acc_ref[ai] = jnp.zeros(acc_ref.shape[1:], F32)

    if hand_off:
        wu_ref_b, wd_ref_b = rest[-3], rest[-2]

        @pl.when(i == 0)
        def _():
            wu_ref_b[...] = wu_ref[...].astype(BF16)
            wd_ref_b[...] = wd_ref[...].astype(BF16)
    else:
        wu_ref_b, wd_ref_b = wu_ref, wd_ref
    tm = ox_ref.shape[0]
    h = h_ref[pl.ds(pl.multiple_of(i * tm, tm), tm), :] if hand_off else h_ref[...]
    u = jnp.maximum(_dot(h, wu_ref_b[...]), 0.0)
    acc_ref[ai] += _dot((u * u).astype(BF16), wd_ref_b[...])

    @pl.when(f == pl.num_programs(f_axis) - 1)
    def _():
        y = x_ref[...] + _rms(acc_ref[ai], pw_ref[...])
        ox_ref[...] = y
        if emit_next:
            rest[0][...] = _rms(y, nw_ref[...]).astype(BF16)


def _mlp(h, w_up, w_down, wl, x, post_w, next_w, layer, tm=512, tf=1024):
    M, D = h.shape
    FF = w_up.shape[2]
    tm = min(tm, M)
    hand_off = w_up.dtype != BF16
    if hand_off:
        tf = min(tf, 512)
    emit_next = layer + 1 < next_w.shape[0]
    nt, nf = M // tm, FF // tf
    if hand_off:
        ix = lambda fn: (lambda f, i: fn(i, f))
        last = lambda i, f: (jnp.where(f == nf - 1, i, 0), 0)
        grid, sem, n_acc = (nf, nt), ("arbitrary", "arbitrary"), nt
    else:
        ix = lambda fn: fn
        last = lambda i, f: (i, 0)
        grid, sem, n_acc = (nt, nf), ("parallel", "arbitrary"), 1
    tile = lambda: pl.BlockSpec((tm, D), ix(last))
    out_specs = [tile()] + ([tile()] if emit_next else [])
    out_shape = [jax.ShapeDtypeStruct((M, D), F32)] + ([jax.ShapeDtypeStruct((M, D), BF16)] if emit_next else [])
    vm = 2 * (2 * _nbytes((tm, D), BF16) + 2 * _nbytes((D, tf), w_up.dtype) + 2 * _nbytes((tm, D), F32))
    vm += (1 + n_acc) * _nbytes((tm, D), F32) + 2 * _nbytes((tm, tf), F32)
    if hand_off:
        out_specs += [pl.BlockSpec((None, D, tf), ix(lambda i, f: (0, 0, f))),
                      pl.BlockSpec((None, tf, D), ix(lambda i, f: (0, f, 0)))]
        out_shape += [jax.ShapeDtypeStruct((1, D, FF), BF16), jax.ShapeDtypeStruct((1, FF, D), BF16)]
        vm += 4 * _nbytes((D, tf), BF16)
    outs = pl.pallas_call(
        functools.partial(_mlp_kernel, emit_next=emit_next, hand_off=hand_off),
        grid=grid,
        in_specs=[pl.BlockSpec((M, D), lambda f, i: (0, 0)) if hand_off
                  else pl.BlockSpec((tm, D), lambda i, f: (i, 0)),
                  pl.BlockSpec((None, D, tf), ix(lambda i, f: (wl, 0, f))),
                  pl.BlockSpec((None, tf, D), ix(lambda i, f: (wl, f, 0))),
                  tile(), _row(layer)(D), _row(layer + 1 if emit_next else layer)(D)],
        out_specs=out_specs,
        out_shape=out_shape,
        scratch_shapes=[pltpu.VMEM((n_acc, tm, D), F32)],
        compiler_params=_cparams(sem, vm),
        name="relu2_mlp",
    )(h, w_up, w_down, x, post_w, next_w)
    n = 2 if emit_next else 1
    return outs[0], (outs[1] if emit_next else None), (tuple(outs[n:]) if hand_off else None)


def _split3(x):
    hi = x.astype(BF16)
    r1 = x - hi.astype(F32)
    mid = r1.astype(BF16)
    lo = (r1 - mid.astype(F32)).astype(BF16)
    return hi, mid, lo


def _decay_rows(d, rows):
    hi, mid, lo = _split3(d)
    r = lax.broadcasted_iota(jnp.int32, (rows, d.shape[-1]), 0)
    return jnp.where(r == 0, hi.astype(F32), jnp.where(r == 1, mid.astype(F32),
                                                       jnp.where(r == 2, lo.astype(F32), 0.0)))


def _gla_prompt_kernel(*refs, decode_heads=None):
    if decode_heads is None:
        (q_ref, k_ref, lf_ref, v_ref, g_ref, nw_ref, sel_ref, lvl_ref, o_ref, sout_ref,
         s_scr, qs_scr, ks_scr, ke_scr, fac_scr, a_scr) = refs
    else:
        (q_ref, k_ref, lf_ref, v_ref, g_ref, nw_ref, sel_ref, lvl_ref, dq_ref, dk_ref, dv_ref,
         o_ref, sout_ref, do_ref, s_scr, qs_scr, ks_scr, ke_scr, fac_scr, a_scr) = refs
    T = q_ref.shape[1]
    D = q_ref.shape[2]
    H = D // HEAD_DIM
    SUB = GLA_SUB
    nb = T // SUB
    c_idx = pl.program_id(1)

    @pl.when(c_idx == 0)
    def _():
        s_scr[...] = jnp.zeros_like(s_scr)

    side = iter(()) if decode_heads is None else _attn_decode_pieces(dq_ref, dk_ref, dv_ref, do_ref,
                                                                     heads=decode_heads)
    stage = [0]

    def side_step():
        stage[0] += 1
        if stage[0] % SIDE_EVERY == 0:
            next(side, None)

    chunks = []
    c = 2 * SUB
    while c < T:
        chunks.append(c)
        c *= 2
    nf = 2 * len(chunks) + 3
    sums = _dot(sel_ref[...], jnp.concatenate(_split3(lf_ref[0]), axis=0))
    fac_scr[...] = jnp.exp(sums[T:])
    nq = len(chunks) + 1
    for i in range(nb):
        rows = slice(i * SUB, (i + 1) * SUB)
        r = sums[rows]
        q16 = q_ref[0, rows, :].astype(F32) * jnp.exp(r)
        kd = k_ref[0, rows, :].astype(F32) * jnp.exp(-r)
        f_row = lambda n: fac_scr[n * nb + i:n * nb + i + 1, :]
        qs_scr[0, rows, :] = (q16 * f_row(0)).astype(BF16)
        qs_scr[1, rows, :] = q16.astype(BF16)
        ks_scr[0, rows, :] = kd.astype(BF16)
        ks_scr[1, rows, :] = (kd * f_row(nq)).astype(BF16)
        for n in range(len(chunks)):
            qs_scr[n + 2, rows, :] = (q16 * f_row(n + 1)).astype(BF16)
            ks_scr[n + 2, rows, :] = (kd * f_row(nq + n + 1)).astype(BF16)
        ke_scr[rows, :] = (kd * f_row(nf - 1)).astype(BF16)
        side_step()
    ke_scr[T:, :] = _decay_rows(fac_scr[(nf - 1) * nb:(nf - 1) * nb + 1, :], SUB).astype(BF16)

    rr = lax.broadcasted_iota(jnp.int32, (SUB, 2 * HEAD_DIM), 0)
    cc = lax.broadcasted_iota(jnp.int32, (SUB, 2 * HEAD_DIM), 1)
    pick = jnp.where((rr < 3) & (cc >= HEAD_DIM), 1.0, 0.0).astype(BF16)
    zeros = jnp.zeros((T, HEAD_DIM), BF16)
    nw = nw_ref[...]

    lvl = lvl_ref[...]
    for h in range(H):
        hs = slice(h * HEAD_DIM, (h + 1) * HEAD_DIM)
        r = _dot_nt(qs_scr[1, :, hs], ks_scr[0:2, :, hs].reshape(2 * T, HEAD_DIM))
        a = jnp.where(lvl == 1, r[:, :T], jnp.where(lvl == 2, r[:, T:], 0.0))
        for n in range(len(chunks)):
            a = jnp.where(lvl == n + 3, _dot_nt(qs_scr[n + 2, :, hs], ks_scr[n + 2, :, hs]), a)
        a_scr[h] = a.astype(BF16)
        side_step()
    for h in range(H):
        hs = slice(h * HEAD_DIM, (h + 1) * HEAD_DIM)
        vh = v_ref[0, :, hs]
        s_old = s_scr[h]
        o = _dot(jnp.concatenate([a_scr[h], qs_scr[0, :, hs]], axis=1),
                 jnp.concatenate([vh, s_old.astype(BF16)], axis=0))
        rhs = jnp.concatenate([jnp.concatenate([vh, zeros], axis=1), pick], axis=0)
        upd = _dot_tn(ke_scr[:, hs], rhs)
        s_scr[h] = s_old * upd[:, HEAD_DIM:] + upd[:, :HEAD_DIM]
        o_ref[0, :, hs] = (_rms(o, nw) * g_ref[0, :, hs].astype(F32)).astype(o_ref.dtype)
        side_step()
    for _ in side:
        pass

    @pl.when(c_idx == pl.num_programs(1) - 1)
    def _():
        sout_ref[0] = s_scr[...]


def _gla_range_matrix(T, SUB):
    chunks = []
    c = 2 * SUB
    while c < T:
        chunks.append(c)
        c *= 2
    nb = T // SUB
    t = np.arange(T)
    rows = [((t[None, :] <= t[:, None]) & (t[None, :] // SUB == t[:, None] // SUB))]
    start = np.arange(nb)[:, None] * SUB
    for C in [T] + chunks:
        rows.append((t[None, :] >= (start // C) * C) & (t[None, :] < start))
    for C in [SUB] + chunks + [T]:
        rows.append((t[None, :] >= start) & (t[None, :] < (start // C + 1) * C))
    sel = np.concatenate(rows, axis=0).astype(np.float32)
    row, col = t[:, None], t[None, :]
    lvl = np.where((row // SUB == col // SUB) & (col <= row), 1, 0)
    for j, C in enumerate([SUB] + chunks):
        lvl = np.where(((row // C) % 2 == 1) & (col // C == row // C - 1), 2 + j, lvl)
    return sel, lvl.astype(np.int32), len(chunks)


def _gla_prompt(q, k, lf, v, g, norm_w, mixer, B, L, decode=None):
    D = q.shape[-1]
    H = D // HEAD_DIM
    T = GLA_BLOCK
    assert T == HEAD_DIM
    nc = L // T
    sel, lvl, nchunks = _gla_range_matrix(T, GLA_SUB)
    sel3 = jnp.asarray(np.concatenate([sel] * 3, axis=1), BF16)
    nfac = sel.shape[0] - T
    r3 = lambda a: a.reshape(B, L, D)
    blk = lambda: pl.BlockSpec((1, T, D), lambda b, c: (b, c, 0))
    const = lambda a: pl.BlockSpec(a.shape, lambda b, c: (0, 0))
    vm = 2 * (5 * _nbytes((T, D), BF16) + _nbytes((T, D), F32)) + 3 * _nbytes((H, HEAD_DIM, HEAD_DIM), F32)
    vm += (2 * nchunks + 6) * _nbytes((T, D), BF16) + 8 * _nbytes((T, D), F32)
    in_specs = [blk(), blk(), blk(), blk(), blk(), _row(mixer)(HEAD_DIM), const(sel3), const(lvl)]
    args = [r3(q), r3(k), r3(lf), r3(v), r3(g), norm_w, sel3, jnp.asarray(lvl)]
    out_specs = [blk(), pl.BlockSpec((1, H, HEAD_DIM, HEAD_DIM), lambda b, c: (b, 0, 0, 0))]
    out_shape = [jax.ShapeDtypeStruct((B, L, D), BF16), jax.ShapeDtypeStruct((B, H, HEAD_DIM, HEAD_DIM), F32)]
    heads = None
    if decode is not None:
        dq, ck, cv, layer, Bd, Ld, heads, d_dtype = decode
        assert Ld == SUBL and SUBL % heads == 0 and Bd % (B * nc) == 0
        bb = Bd // (B * nc)
        kv_shape = (bb,) + ck.shape[2:]
        kv_spec = lambda: pl.BlockSpec((None,) + kv_shape, lambda b, c: (layer, b * nc + c, 0, 0, 0, 0))
        dblk = lambda: pl.BlockSpec((bb, Ld, D), lambda b, c: (b * nc + c, 0, 0))
        in_specs += [dblk(), kv_spec(), kv_spec()]
        args += [dq.reshape(Bd, Ld, D), ck, cv]
        out_specs.append(dblk())
        out_shape.append(jax.ShapeDtypeStruct((Bd, Ld, D), d_dtype))
        vm += 4 * _nbytes((bb, Ld, D), F32) + 4 * _nbytes(kv_shape, ck.dtype)
        vm += 4 * _nbytes(kv_shape[1:], BF16) + 6 * _nbytes((SUBL * Ld, ck.shape[2] * SUBL), F32)
    outs = pl.pallas_call(
        functools.partial(_gla_prompt_kernel, decode_heads=heads),
        grid=(B, nc),
        in_specs=in_specs,
        out_specs=out_specs,
        out_shape=out_shape,
        scratch_shapes=[pltpu.VMEM((H, HEAD_DIM, HEAD_DIM), F32),
                        pltpu.VMEM((nchunks + 2, T, D), BF16),
                        pltpu.VMEM((nchunks + 2, T, D), BF16),
                        pltpu.VMEM((T + GLA_SUB, D), BF16),
                        pltpu.VMEM((nfac, D), F32),
                        pltpu.VMEM((H, T, T), BF16)],
        compiler_params=_cparams(("parallel", "arbitrary"), vm),
        name="hgrn2_recurrence_prompt",
    )(*args)
    a_dec = outs[2].reshape(Bd * Ld, D) if decode is not None else None
    return outs[0].reshape(B * L, D), outs[1], a_dec


def _gla_step_kernel(q_ref, k_ref, lf_ref, v_ref, g_ref, s0_ref, nw_ref, o_ref, sout_ref,
                     qs_scr, ks_scr, ke_scr, ds_scr):
    bb, T, D = q_ref.shape
    H = D // HEAD_DIM
    R = bb * T
    pos = lax.broadcasted_iota(jnp.int32, (bb, T, D), 1)
    b = lf_ref[...]
    sh = 1
    while sh < T:
        b = b + jnp.where(pos >= sh, pltpu.roll(b, sh, 1), 0.0)
        sh *= 2
    b_last = b[:, T - 1:T, :]
    k = k_ref[...]
    qs_scr[...] = q_ref[...] * jnp.exp(b)
    ks_scr[...] = k * jnp.exp(-b)
    ke_scr[...] = k * jnp.exp(b_last - b)
    d_hi, d_mid, d_lo = (t.astype(F32) for t in _split3(jnp.exp(b_last)))
    ds_scr[...] = jnp.where(pos == 0, d_hi, jnp.where(pos == 1, d_mid, jnp.where(pos == 2, d_lo, 0.0)))

    row = lax.broadcasted_iota(jnp.int32, (R, R), 0)
    col = lax.broadcasted_iota(jnp.int32, (R, R), 1)
    same_seq_causal = (row // T == col // T) & (col <= row)
    rr = lax.broadcasted_iota(jnp.int32, (T, 2 * HEAD_DIM), 0)
    cc = lax.broadcasted_iota(jnp.int32, (T, 2 * HEAD_DIM), 1)
    pick = jnp.where((rr < 3) & (cc >= HEAD_DIM), 1.0, 0.0)
    zeros = jnp.zeros((T, HEAD_DIM), F32)
    nw = nw_ref[...]

    for h in range(H):
        hs = slice(h * HEAD_DIM, (h + 1) * HEAD_DIM)
        qd = qs_scr[:, :, hs].reshape(R, HEAD_DIM)
        kd = ks_scr[:, :, hs].reshape(R, HEAD_DIM)
        vh = v_ref[:, :, hs].reshape(R, HEAD_DIM)
        a = jnp.where(same_seq_causal, _dot_nt(qd.astype(BF16), kd.astype(BF16)), 0.0)
        o_intra = _dot(a.astype(BF16), vh.astype(BF16))
        outs = []
        for i in range(bb):
            s_old = s0_ref[i, h]
            outs.append(o_intra[i * T:(i + 1) * T]
                        + _dot(qd[i * T:(i + 1) * T].astype(BF16), s_old.astype(BF16)))
            lhs = jnp.concatenate([ke_scr[i, :, hs], ds_scr[i, :, hs]], axis=0).astype(BF16)
            rhs = jnp.concatenate([jnp.concatenate([vh[i * T:(i + 1) * T], zeros], axis=1), pick],
                                  axis=0).astype(BF16)
            upd = _dot_tn(lhs, rhs)
            sout_ref[i, h] = s_old * upd[:, HEAD_DIM:] + upd[:, :HEAD_DIM]
        o = jnp.concatenate(outs, axis=0)
        on = _rms(o, nw) * g_ref[:, :, hs].reshape(R, HEAD_DIM)
        o_ref[:, :, hs] = on.reshape(bb, T, HEAD_DIM).astype(o_ref.dtype)


def _gla_step(q, k, lf, v, g, s0, norm_w, mixer, B, L, bb=8):
    D = q.shape[-1]
    H = D // HEAD_DIM
    assert L == 8 and B % bb == 0
    r3 = lambda a: a.reshape(B, L, D)
    blk = lambda: pl.BlockSpec((bb, L, D), lambda b: (b, 0, 0))
    sblk = lambda: pl.BlockSpec((bb, H, HEAD_DIM, HEAD_DIM), lambda b: (b, 0, 0, 0))
    vm = 2 * (6 * _nbytes((bb, L, D), F32) + 2 * _nbytes((bb, H, HEAD_DIM, HEAD_DIM), F32))
    vm += 12 * _nbytes((bb, L, D), F32)
    o, s = pl.pallas_call(
        _gla_step_kernel,
        grid=(B // bb,),
        in_specs=[blk(), blk(), blk(), blk(), blk(),
                  pl.BlockSpec((None, bb, H, HEAD_DIM, HEAD_DIM), lambda b: (mixer, b, 0, 0, 0)),
                  _row(mixer)(HEAD_DIM)],
        out_specs=[blk(), sblk()],
        out_shape=[jax.ShapeDtypeStruct((B, L, D), F32),
                   jax.ShapeDtypeStruct((B, H, HEAD_DIM, HEAD_DIM), F32)],
        scratch_shapes=[pltpu.VMEM((bb, L, D), F32)] * 4,
        compiler_params=_cparams(("parallel",), vm),
        name="hgrn2_recurrence_step",
    )(r3(q), r3(k), r3(lf), r3(v), r3(g), s0, norm_w)
    return o.reshape(B * L, D), s


def _pool_kernel(*refs, pos0, has_halo):
    if has_halo:
        u_ref, halo_ref, buf_ref, pw_ref, ps_ref, o_ref, nb_ref, ext = refs
    else:
        u_ref, buf_ref, pw_ref, ps_ref, o_ref, nb_ref, ext = refs
    bb, tl, D = u_ref.shape
    G = len(POOL_WINDOWS)
    gc = D // G
    l_idx = pl.program_id(1)
    ext[:, HALO:, :] = u_ref[...]
    if has_halo:
        @pl.when(l_idx == 0)
        def _():
            ext[:, :HALO, :] = buf_ref[...]

        @pl.when(l_idx > 0)
        def _():
            ext[:, :HALO, :] = halo_ref[...]
    else:
        ext[:, :HALO, :] = buf_ref[...]

    pos = pos0 + l_idx * tl + lax.broadcasted_iota(jnp.int32, (tl, gc), 0)
    for gi, w in enumerate(POOL_WINDOWS):
        ls = slice(gi * gc, (gi + 1) * gc)
        cnt = jnp.minimum(w, pos + 1).astype(F32)
        rows = []
        for i in range(bb):
            u = ext[i, HALO:HALO + tl, ls]
            s = u
            for d in range(1, w):
                s = s + ext[i, HALO - d:HALO - d + tl, ls]
            rows.append(s / cnt - u)
        pooled = rows[0] if bb == 1 else jnp.concatenate(rows, axis=0)
        mixed = (_dot(pooled.astype(BF16), pw_ref[gi]) * ps_ref[:, ls]).astype(o_ref.dtype)
        for i in range(bb):
            o_ref[i, :, ls] = mixed[i * tl:(i + 1) * tl]

    @pl.when(l_idx == pl.num_programs(1) - 1)
    def _():
        nb_ref[...] = ext[:, tl:tl + HALO, :]


def _pool(u, buf, pos0, pool_w, pool_scale, mixer, B, L, bb, tl, out_dtype):
    D = u.shape[-1]
    G = len(POOL_WINDOWS)
    gc = D // G
    tl = min(tl, L)
    u3 = u.reshape(B, L, D)
    has_halo = L > tl
    step = tl // HALO
    in_specs = [pl.BlockSpec((bb, tl, D), lambda b, l: (b, l, 0))]
    args = [u3]
    if has_halo:
        in_specs.append(pl.BlockSpec((bb, HALO, D), lambda b, l: (b, jnp.maximum(l * step - 1, 0), 0)))
        args.append(u3)
    in_specs += [pl.BlockSpec((None, bb, HALO, D), lambda b, l: (mixer, b, 0, 0)),
                 pl.BlockSpec((None, G, gc, gc), lambda b, l: (mixer, 0, 0, 0)),
                 _row(mixer)(D)]
    args += [buf, pool_w, pool_scale]
    vm = 2 * (_nbytes((bb, tl, D), F32) + 3 * _nbytes((bb, HALO, D), F32) + _nbytes((G, gc, gc), BF16)
              + _nbytes((bb, tl, D), BF16)) + _nbytes((bb, tl + HALO, D), F32) + 4 * _nbytes((bb, tl, D), F32)
    mixed, newbuf = pl.pallas_call(
        functools.partial(_pool_kernel, pos0=pos0, has_halo=has_halo),
        grid=(B // bb, L // tl),
        in_specs=in_specs,
        out_specs=[pl.BlockSpec((bb, tl, D), lambda b, l: (b, l, 0)),
                   pl.BlockSpec((bb, HALO, D), lambda b, l: (b, 0, 0))],
        out_shape=[jax.ShapeDtypeStruct((B, L, D), out_dtype), jax.ShapeDtypeStruct((B, HALO, D), F32)],
        scratch_shapes=[pltpu.VMEM((bb, tl + HALO, D), F32)],
        compiler_params=_cparams(("parallel", "arbitrary"), vm),
        name="causal_pool",
    )(*args)
    return mixed.reshape(B * L, D), newbuf


def _attn_kernel(q_ref, k_ref, v_ref, o_ref, *, heads):
    bb, tq, D = q_ref.shape
    hd = D // heads
    scale = hd ** -0.5
    for i in range(bb):
        for h in range(heads):
            hs = slice(h * hd, (h + 1) * hd)
            s = _dot_nt(q_ref[i, :, hs].astype(BF16), k_ref[i, :, hs].astype(BF16)) * scale
            p = jnp.exp(s - jnp.max(s, axis=-1, keepdims=True))
            den = jnp.sum(p, axis=-1, keepdims=True)
            o_ref[i, :, hs] = (_dot(p.astype(BF16), v_ref[i, :, hs].astype(BF16)) / den).astype(o_ref.dtype)


def _attn(q, mk, mv, layer, B, L, heads, bb, tq, out_dtype):
    D = q.shape[-1]
    tq = min(tq, L)
    kv_shape = (bb,) + mk.shape[2:]
    kv_blk = (None,) + kv_shape
    vm = 2 * (_nbytes((bb, tq, D), q.dtype) + 2 * _nbytes(kv_shape, mk.dtype) + _nbytes((bb, tq, D), out_dtype))
    vm += 2 * _nbytes(kv_shape, BF16) + 8 * _nbytes((tq, D // heads), F32)
    o = pl.pallas_call(
        functools.partial(_attn_kernel, heads=heads),
        grid=(B // bb, L // tq),
        in_specs=[pl.BlockSpec((bb, tq, D), lambda b, l: (b, l, 0)),
                  pl.BlockSpec(kv_blk, lambda b, l: (layer, b, 0, 0)),
                  pl.BlockSpec(kv_blk, lambda b, l: (layer, b, 0, 0))],
        out_specs=pl.BlockSpec((bb, tq, D), lambda b, l: (b, l, 0)),
        out_shape=jax.ShapeDtypeStruct((B, L, D), out_dtype),
        compiler_params=_cparams(("parallel", "parallel"), vm),
        name="memory_cross_attention",
    )(q.reshape(B, L, D), mk, mv)
    return o.reshape(B * L, D)


def _attn_decode_pieces(q_ref, k_ref, v_ref, o_ref, *, heads):
    bb, tq, D = q_ref.shape
    n_mem, nu = k_ref.shape[1], k_ref.shape[2]
    hd = D // heads
    per = SUBL // heads
    W = n_mem * SUBL
    scale = hd ** -0.5
    res = lax.broadcasted_iota(jnp.int32, (tq, W), 1) % SUBL
    tile = lambda r: jnp.concatenate([r] * (W // LANES), axis=1)

    def feat(j):
        c, h = divmod(j, heads)
        return slice(h * hd + c * LANES, h * hd + (c + 1) * LANES)

    def same_row_reduce(x, op):
        r = x[:, :LANES]
        for j in range(1, W // LANES):
            r = op(r, x[:, j * LANES:(j + 1) * LANES])
        sh = SUBL
        while sh < LANES:
            r = op(r, pltpu.roll(r, sh, 1))
            sh *= 2
        return r

    for i in range(bb):
        w = None
        for u in range(nu):
            slab = k_ref[i, :, u].reshape(W, LANES).astype(BF16)
            qm = jnp.concatenate([q_ref[i, :, feat(u * SUBL + s)] for s in range(SUBL)], axis=0)
            p = _dot_nt(qm.astype(BF16), slab)
            z = p[:tq]
            for s in range(1, SUBL):
                z = jnp.where(res == s, p[s * tq:(s + 1) * tq], z)
            w = z if w is None else w + z
            yield
        sc = w
        for c in range(1, per):
            sc = sc + pltpu.roll(w, W - c * heads, 1)
        sc = sc * scale
        e = jnp.exp(sc - tile(same_row_reduce(sc, jnp.maximum)))
        pn = e / tile(same_row_reduce(e, jnp.add))
        blocks = []
        for s in range(SUBL):
            c_lo = s // heads
            src = pn if c_lo == 0 else pltpu.roll(pn, c_lo * heads, 1)
            blocks.append(jnp.where(res == s, src, 0.0))
        pm = jnp.concatenate(blocks, axis=0).astype(BF16)
        yield
        for u in range(nu):
            o = _dot(pm, v_ref[i, :, u].reshape(W, LANES).astype(BF16))
            for s in range(SUBL):
                o_ref[i, :, feat(u * SUBL + s)] = o[s * tq:(s + 1) * tq].astype(o_ref.dtype)
            yield


def _attn_decode_kernel(q_ref, k_ref, v_ref, o_ref, *, heads):
    for _ in _attn_decode_pieces(q_ref, k_ref, v_ref, o_ref, heads=heads):
        pass


def _attn_decode(q, ck, cv, layer, B, L, heads, bb, out_dtype):
    D = q.shape[-1]
    assert L == SUBL and SUBL % heads == 0 and D % (SUBL * LANES) == 0
    kv_shape = (bb,) + ck.shape[2:]
    vm = 2 * (2 * _nbytes((bb, L, D), F32) + 2 * _nbytes(kv_shape, ck.dtype))
    vm += 4 * _nbytes(kv_shape[1:], BF16) + 6 * _nbytes((SUBL * L, ck.shape[2] * SUBL), F32)
    o = pl.pallas_call(
        functools.partial(_attn_decode_kernel, heads=heads),
        grid=(B // bb,),
        in_specs=[pl.BlockSpec((bb, L, D), lambda b: (b, 0, 0)),
                  pl.BlockSpec((None,) + kv_shape, lambda b: (layer, b, 0, 0, 0, 0)),
                  pl.BlockSpec((None,) + kv_shape, lambda b: (layer, b, 0, 0, 0, 0))],
        out_specs=pl.BlockSpec((bb, L, D), lambda b: (b, 0, 0)),
        out_shape=jax.ShapeDtypeStruct((B, L, D), out_dtype),
        compiler_params=_cparams(("parallel",), vm),
        name="memory_cross_attention_decode",
    )(q.reshape(B, L, D), ck, cv)
    return o.reshape(B * L, D)


def _mem_proj_kernel(x_ref, nw_ref, w_ref, flat_ref, bf_ref, w_scr, *, heads):
    tm, D = x_ref.shape
    R = D // LANES
    nchunk = R // heads

    @pl.when(pl.program_id(1) == 0)
    def _():
        w_scr[...] = w_ref[...].astype(BF16)

    acc = _dot(_rms(x_ref[...], nw_ref[...]).astype(BF16), w_scr[...])
    bf_ref[...] = acc.astype(bf_ref.dtype)
    for h in range(heads):
        for c in range(nchunk):
            j = h * nchunk + c
            flat_ref[pl.ds(c * heads + h, tm, stride=R), :] = acc[:, j * LANES:(j + 1) * LANES]


def _mem_proj(mem, norm_w, w, half, heads, tm=256):
    M, D = mem.shape
    depth = w.shape[0]
    R = D // LANES
    tm = min(tm, M)
    nt = M // tm
    vm = 2 * (_nbytes((tm, D), F32) + _nbytes((D, D), F32) + _nbytes((tm, D), F32) + _nbytes((tm, D), BF16))
    vm += _nbytes((D, D), BF16) + 2 * _nbytes((tm, D), F32)
    return pl.pallas_call(
        functools.partial(_mem_proj_kernel, heads=heads),
        grid=(depth, nt),
        in_specs=[pl.BlockSpec((tm, D), lambda l, i: (i, 0)),
                  pl.BlockSpec((None, 1, D), lambda l, i: (l, 0, 0)),
                  pl.BlockSpec((None, D, D), lambda l, i: (l, 0, half))],
        out_specs=[pl.BlockSpec((tm * R, LANES), lambda l, i: (l * nt + i, 0)),
                   pl.BlockSpec((None, tm, D), lambda l, i: (l, i, 0))],
        out_shape=[jax.ShapeDtypeStruct((depth * M * R, LANES), F32),
                   jax.ShapeDtypeStruct((depth, M, D), BF16)],
        scratch_shapes=[pltpu.VMEM((D, D), BF16)],
        compiler_params=_cparams(("parallel", "arbitrary"), vm),
        name="memory_kv_projection",
    )(mem, norm_w, w)


class _Group:
    def __init__(self, x, B, L, pos0, S_in, buf_in, mk, mv, prompt, p):
        self.B, self.L, self.pos0, self.S_in, self.buf_in, self.mk, self.mv = B, L, pos0, S_in, buf_in, mk, mv
        self.prompt = prompt
        self.act = BF16 if prompt else F32
        self.x = x
        self.hn = None if prompt else _rmsnorm(x, p["norm_mix_pre"], 0)
        self.S_out, self.buf_out = [], []


class _NoRider:
    def job(self):
        return None

    def served(self, a):
        pass


def _mixer(g, l, p, rider=_NoRider()):
    j = l // 2
    if l % 2 == 0:
        w, wl = _weight(p, "w_in_a", j)
        h_in, prenorm = (g.x, (p["norm_mix_pre"], l)) if g.hn is None else (g.hn, None)
        q, k, lf, v, gate, cast, a = _proj_a(h_in, w, wl, p["hg_lb_logits"], l, g.act,
                                             rider.job() if g.prompt else None, prenorm)
        _hand_off(p, "w_in_a", j, cast)
        rider.served(a)
        if g.prompt:
            o, S, a = _gla_prompt(q, k, lf, v, gate, p["hg_norm"], j, g.B, g.L, rider.job())
            rider.served(a)
        else:
            o, S = _gla_step(q, k, lf, v, gate, g.S_in, p["hg_norm"], j, g.B, g.L)
        g.S_out.append(S)
        g.x, g.hn = _mm_res(o, p["w_out_a"], j, g.x, p["norm_mix_post"], p["norm_x_pre"], l)
    else:
        u = _project(g, p, "w_in_b", j, F32)
        bb, tl = (1, 512) if g.prompt else (16, g.L)
        mixed, nb = _pool(u, g.buf_in, g.pos0, p["pool_w"], p["pool_scale"], j, g.B, g.L, bb, tl, g.act)
        g.buf_out.append(nb[:, 1:, :])
        g.x, g.hn = _mm_res(mixed, p["w_out_b"], j, g.x, p["norm_mix_post"], p["norm_x_pre"], l)


def _weight(p, name, idx):
    cast = p["cast"].get((name, idx))
    return (cast, 0) if cast is not None else (p[name], idx)


def _hand_off(p, name, idx, cast):
    if cast is not None:
        p["cast"][(name, idx)] = cast


def _project(g, p, name, idx, dtype):
    w, wl = _weight(p, name, idx)
    outs = _matmul(g.hn, w, wl, [dtype])
    _hand_off(p, name, idx, outs[1] if len(outs) > 1 else None)
    return outs[0]


def _after_attention(g, l, a, p):
    g.x, g.hn = _mm_res(a, p["w_xo"], l, g.x, p["norm_x_post"], p["norm_mlp_pre"], l)
    (wu, wl), (wd, _) = _weight(p, "w_up", l), _weight(p, "w_down", l)
    g.x, g.hn, cast = _mlp(g.hn, wu, wd, wl, g.x, p["norm_mlp_post"], p["norm_mix_pre"], l)
    if cast is not None:
        _hand_off(p, "w_up", l, cast[0])
        _hand_off(p, "w_down", l, cast[1])


class _SampleWalk:
    def __init__(self, g, p):
        self.g, self.p = g, p
        self._walk = self._layers()
        self._pending = next(self._walk, None)

    def _layers(self):
        g, p = self.g, self.p
        for l in range(p["w_xq"].shape[0]):
            _mixer(g, l, p)
            q = _project(g, p, "w_xq", l, g.act)
            a = yield (q, g.mk, g.mv, l, g.B, g.L, p["mem_heads"], g.act)
            _after_attention(g, l, a, p)

    def job(self):
        return self._pending

    def served(self, a):
        if a is not None:
            try:
                self._pending = self._walk.send(a)
            except StopIteration:
                self._pending = None

    def serve_standalone(self):
        q, mk, mv, l, B, L, heads, dtype = self._pending
        self.served(_attn_decode(q, mk, mv, l, B, L, heads, 4, dtype))


def _trunks(gp, gs, p):
    heads = p["mem_heads"]
    rider = _SampleWalk(gs, p)
    for l in range(p["w_xq"].shape[0]):
        if l % 2 == 1 and rider.job() is not None:
            rider.serve_standalone()
        _mixer(gp, l, p, rider)
        qp = _project(gp, p, "w_xq", l, gp.act)
        _after_attention(gp, l, _attn(qp, gp.mk, gp.mv, l, gp.B, gp.L, heads, 1, 512, gp.act), p)
    while rider.job() is not None:
        rider.serve_standalone()


def kernel(x_prompt, x_sample, state_hgrn, state_pool, cache_mem_k, cache_mem_v, mem_prompt, w_in_a, hg_lb_logits, hg_norm, w_out_a, w_in_b, pool_w, pool_scale, w_out_b, norm_mem, w_xq, w_xkv, w_xo, norm_mix_pre, norm_mix_post, norm_x_pre, norm_x_post, norm_mlp_pre, norm_mlp_post, w_up, w_down):
    B, L, D = x_prompt.shape
    Bs, Ls, _ = x_sample.shape
    depth = w_xq.shape[0]
    n_mem, heads = cache_mem_k.shape[2], cache_mem_k.shape[3]
    assert L % GLA_BLOCK == 0 and Ls <= GLA_SUB and D % (HEAD_DIM * len(POOL_WINDOWS)) == 0
    assert state_pool.shape[2] == HALO - 1
    bf = lambda a: a.astype(BF16)
    rows = lambda a: a.reshape(a.shape[0], 1, a.shape[1])
    p = dict(w_in_a=w_in_a, hg_lb_logits=hg_lb_logits, hg_norm=rows(hg_norm), w_out_a=bf(w_out_a),
             w_in_b=w_in_b, pool_w=bf(pool_w), pool_scale=rows(pool_scale), w_out_b=bf(w_out_b),
             w_xq=w_xq, w_xo=bf(w_xo), norm_mix_pre=rows(norm_mix_pre), norm_mix_post=rows(norm_mix_post),
             norm_x_pre=rows(norm_x_pre), norm_x_post=rows(norm_x_post), norm_mlp_pre=rows(norm_mlp_pre),
             norm_mlp_post=rows(norm_mlp_post), w_up=w_up, w_down=w_down, mem_heads=heads, cast={})
    R = D // LANES
    nchunk = R // heads

    def cache_view(flat, nb):
        c = flat.reshape(depth, nb, n_mem, nchunk, heads, LANES)
        return jnp.swapaxes(c, 3, 4).reshape(depth, nb, n_mem, heads, D // heads)

    def flat_view(cache):
        nb = cache.shape[1]
        c = cache.reshape(depth, nb, n_mem, heads, nchunk, LANES)
        return jnp.swapaxes(c, 3, 4).reshape(depth, nb, n_mem, R // SUBL, SUBL, LANES)

    mem2d = mem_prompt.reshape(B * n_mem, D)
    k_flat, k_bf = _mem_proj(mem2d, rows(norm_mem), w_xkv, 0, heads)
    v_flat, v_bf = _mem_proj(mem2d, rows(norm_mem), w_xkv, 1, heads)
    cache_mem_k_prompt = cache_view(k_flat, B)
    cache_mem_v_prompt = cache_view(v_flat, B)
    mk_bf = k_bf.reshape(depth, B, n_mem, D)
    mv_bf = v_bf.reshape(depth, B, n_mem, D)

    n_b = state_pool.shape[0]
    buf0 = jnp.zeros((n_b, B, HALO, D), F32)
    buf_s = jnp.pad(state_pool, ((0, 0), (0, 0), (1, 0), (0, 0)))
    gp = _Group(x_prompt.reshape(B * L, D), B, L, 0, None, buf0, mk_bf, mv_bf, True, p)
    gs = _Group(x_sample.reshape(Bs * Ls, D), Bs, Ls, PAST_LEN, state_hgrn, buf_s,
                flat_view(cache_mem_k), flat_view(cache_mem_v), False, p)
    _trunks(gp, gs, p)
    return (gp.x.reshape(B, L, D), gs.x.reshape(Bs, Ls, D), jnp.stack(gp.S_out), jnp.stack(gp.buf_out),
            cache_mem_k_prompt, cache_mem_v_prompt, jnp.stack(gs.S_out), jnp.stack(gs.buf_out))
```

```python
import functools

import numpy as np

import jax
import jax.numpy as jnp
from jax import lax
from jax.experimental import pallas as pl
from jax.experimental.pallas import tpu as pltpu

F32 = jnp.float32
BF16 = jnp.bfloat16
EPS = 1e-6
PAST_LEN = 16384
POOL_WINDOWS = (2, 4, 8, 16)
LANES = 128
SUBL = 8
HEAD_DIM = 128
GLA_BLOCK = 128
GLA_SUB = 16
HALO = 16
MM_RES_SPLIT = 4
SIDE_EVERY = 4
PROJ_SIDE_PIECES = 2

V7X_VMEM_BYTES = 64 * 1024 * 1024
VMEM_CAP = V7X_VMEM_BYTES - 6 * 1024 * 1024


def _cparams(sem, vmem_bytes):
    return pltpu.CompilerParams(dimension_semantics=sem,
                                vmem_limit_bytes=int(min(max(vmem_bytes, 16 * 1024 * 1024), VMEM_CAP)))


def _nbytes(shape, dtype):
    n = 1
    for s in shape:
        n *= s
    return n * jnp.dtype(dtype).itemsize


def _rms(v, w):
    ms = jnp.mean(v * v, axis=-1, keepdims=True)
    return v * lax.rsqrt(ms + EPS) * w


def _sigmoid(x):
    return 1.0 / (1.0 + jnp.exp(-x))


def _dot(a, b):
    return jnp.dot(a, b, preferred_element_type=F32)


def _dot_nt(a, b):
    return lax.dot_general(a, b, (((1,), (1,)), ((), ())), preferred_element_type=F32)


def _dot_tn(a, b):
    return lax.dot_general(a, b, (((0,), (0,)), ((), ())), preferred_element_type=F32)


def _rmsnorm_kernel(x_ref, w_ref, o_ref):
    o_ref[...] = _rms(x_ref[...], w_ref[...]).astype(o_ref.dtype)


def _row(layer):
    return lambda width: pl.BlockSpec((None, 1, width), lambda *_: (layer, 0, 0))


def _rmsnorm(x, w, layer, tm=512):
    M, D = x.shape
    tm = min(tm, M)
    return pl.pallas_call(
        _rmsnorm_kernel,
        grid=(M // tm,),
        in_specs=[pl.BlockSpec((tm, D), lambda i: (i, 0)), _row(layer)(D)],
        out_specs=pl.BlockSpec((tm, D), lambda i: (i, 0)),
        out_shape=jax.ShapeDtypeStruct((M, D), BF16),
        compiler_params=_cparams(("parallel",), 4 * _nbytes((tm, D), F32)),
        name="rmsnorm",
    )(x, w)


def _mm_kernel(a_ref, w_ref, *o_refs, hand_off):
    w = w_ref[...].astype(BF16)
    if hand_off:
        o_refs[-1][...] = w
        o_refs = o_refs[:-1]
    acc = _dot(a_ref[...], w)
    for o_ref in o_refs:
        o_ref[...] = acc.astype(o_ref.dtype)


def _matmul(a, w, layer, out_dtypes, tm=1024, tn=1024):
    M, K = a.shape
    N = w.shape[2]
    tm, tn = min(tm, M), min(tn, N)
    hand_off = w.dtype != BF16
    assert not hand_off or M == tm
    vm = 2 * (_nbytes((tm, K), BF16) + _nbytes((K, tn), w.dtype)) + _nbytes((tm, tn), F32)
    vm += sum(2 * _nbytes((tm, tn), d) for d in out_dtypes)
    out_specs = [pl.BlockSpec((tm, tn), lambda i, j: (i, j)) for _ in out_dtypes]
    out_shape = [jax.ShapeDtypeStruct((M, N), d) for d in out_dtypes]
    if hand_off:
        out_specs.append(pl.BlockSpec((None, K, tn), lambda i, j: (0, 0, j)))
        out_shape.append(jax.ShapeDtypeStruct((1, K, N), BF16))
        vm += 3 * _nbytes((K, tn), BF16)
    outs = pl.pallas_call(
        functools.partial(_mm_kernel, hand_off=hand_off),
        grid=(M // tm, N // tn),
        in_specs=[pl.BlockSpec((tm, K), lambda i, j: (i, 0)),
                  pl.BlockSpec((None, K, tn), lambda i, j: (layer, 0, j))],
        out_specs=out_specs,
        out_shape=out_shape,
        compiler_params=_cparams(("parallel", "parallel"), vm),
        name="matmul",
    )(a, w)
    return outs


def _proj_a_kernel(*refs, layer, n_cast, decode_heads, prenorm):
    h_ref, wq_ref, wf_ref, wi_ref, wg_ref, lbl_ref = refs[:6]
    n_pre = 6 + (1 if prenorm else 0)
    n_in = n_pre + (0 if decode_heads is None else 3)
    q_ref, k_ref, lf_ref, v_ref, g_ref = refs[n_in:n_in + 5]
    w_out_refs = refs[n_in + 5:n_in + 5 + n_cast]
    side = iter(()) if decode_heads is None else _attn_decode_pieces(*refs[n_pre:n_in], refs[-1],
                                                                     heads=decode_heads)

    def side_steps():
        for _ in range(PROJ_SIDE_PIECES):
            next(side, None)

    h = _rms(h_ref[...], refs[6][...]).astype(BF16) if prenorm else h_ref[...]
    ws = [r[...].astype(BF16) for r in (wq_ref, wf_ref, wi_ref, wg_ref)]
    for r, w in zip(w_out_refs, ws):
        r[...] = w
    aq = _dot(h, ws[0])
    q_ref[...] = (aq * _sigmoid(aq)).astype(q_ref.dtype)
    side_steps()
    lg = lbl_ref[...]
    e = jnp.exp(lg - jnp.max(lg, axis=0, keepdims=True))
    lb = jnp.sum(e[:layer + 1], axis=0, keepdims=True) / jnp.sum(e, axis=0, keepdims=True)
    f = lb + (1.0 - lb) * _sigmoid(_dot(h, ws[1]))
    k_ref[...] = (1.0 - f).astype(k_ref.dtype)
    lf_ref[...] = jnp.log(f)
    side_steps()
    v_ref[...] = _dot(h, ws[2]).astype(v_ref.dtype)
    side_steps()
    ag = _dot(h, ws[3])
    g_ref[...] = (ag * _sigmoid(ag)).astype(g_ref.dtype)
    for _ in side:
        pass


def _proj_a(h, w, mixer, lb_logits, layer, act_dtype, decode=None, prenorm=None, tm=1024, tn=512):
    M, D = h.shape
    if decode is not None:
        tm = min(tm, 512)
    tm = min(tm, M)
    parts = isinstance(w, (tuple, list))
    hand_off = not parts and w.dtype != BF16
    if hand_off:
        tn = min(tn, 256)
        assert M == tm
    ni, nj = M // tm, D // tn
    nl = lb_logits.shape[0]
    if parts:
        wspec = lambda s: pl.BlockSpec((None, D, tn), lambda i, j: (0, 0, j))
        w_args, w_dtype = list(w), BF16
    else:
        wspec = lambda s: pl.BlockSpec((None, D, tn), lambda i, j: (mixer, 0, j + s * nj))
        w_args, w_dtype = [w] * 4, w.dtype
    ospec = pl.BlockSpec((tm, tn), lambda i, j: (i, j))
    vm = 2 * (_nbytes((tm, D), h.dtype) + 4 * _nbytes((D, tn), w_dtype) + 5 * _nbytes((tm, tn), F32))
    vm += 4 * _nbytes((tm, tn), F32)
    in_specs = [pl.BlockSpec((tm, D), lambda i, j: (i, 0)), wspec(0), wspec(1), wspec(2), wspec(3),
                pl.BlockSpec((nl, tn), lambda i, j: (0, j))]
    args = [h, *w_args, lb_logits]
    if prenorm is not None:
        in_specs.append(_row(prenorm[1])(D))
        args.append(prenorm[0])
        vm += 2 * _nbytes((tm, D), F32)
    out_specs = [ospec] * 5
    out_shape = [jax.ShapeDtypeStruct((M, D), act_dtype), jax.ShapeDtypeStruct((M, D), act_dtype),
                 jax.ShapeDtypeStruct((M, D), F32), jax.ShapeDtypeStruct((M, D), act_dtype),
                 jax.ShapeDtypeStruct((M, D), act_dtype)]
    if hand_off:
        out_specs += [pl.BlockSpec((None, D, tn), lambda i, j: (0, 0, j))] * 4
        out_shape += [jax.ShapeDtypeStruct((1, D, D), BF16)] * 4
        vm += 12 * _nbytes((D, tn), BF16)
    heads = None
    if decode is not None:
        dq, ck, cv, dlayer, Bd, Ld, heads, d_dtype = decode
        assert Ld == SUBL and SUBL % heads == 0 and Bd % (ni * nj) == 0
        bb = Bd // (ni * nj)
        kv_shape = (bb,) + ck.shape[2:]
        kv_spec = lambda: pl.BlockSpec((None,) + kv_shape, lambda i, j: (dlayer, i * nj + j, 0, 0, 0, 0))
        dblk = lambda: pl.BlockSpec((bb, Ld, D), lambda i, j: (i * nj + j, 0, 0))
        in_specs += [dblk(), kv_spec(), kv_spec()]
        args += [dq.reshape(Bd, Ld, D), ck, cv]
        out_specs.append(dblk())
        out_shape.append(jax.ShapeDtypeStruct((Bd, Ld, D), d_dtype))
        vm += 4 * _nbytes((bb, Ld, D), F32) + 4 * _nbytes(kv_shape, ck.dtype)
        vm += 4 * _nbytes(kv_shape[1:], BF16) + 6 * _nbytes((SUBL * Ld, ck.shape[2] * SUBL), F32)
    outs = pl.pallas_call(
        functools.partial(_proj_a_kernel, layer=layer, n_cast=4 if hand_off else 0, decode_heads=heads,
                          prenorm=prenorm is not None),
        grid=(ni, nj),
        in_specs=in_specs,
        out_specs=out_specs,
        out_shape=out_shape,
        compiler_params=_cparams(("parallel", "parallel"), vm),
        name="hgrn2_proj",
    )(*args)
    cast = tuple(outs[5:9]) if hand_off else None
    a_dec = outs[-1].reshape(Bd * Ld, D) if decode is not None else None
    return tuple(outs[:5]) + (cast, a_dec)


def _mm_res_kernel(a_ref, w_ref, x_ref, pw_ref, nw_ref, ox_ref, oh_ref):
    tm = a_ref.shape[0]
    nsplit = MM_RES_SPLIT if tm % (8 * MM_RES_SPLIT) == 0 and tm // MM_RES_SPLIT >= 128 else 1
    for p in range(nsplit):
        rows = slice(p * tm // nsplit, (p + 1) * tm // nsplit)
        m = _dot(a_ref[rows, :].astype(BF16), w_ref[...])
        y = x_ref[rows, :] + _rms(m, pw_ref[...])
        ox_ref[rows, :] = y
        oh_ref[rows, :] = _rms(y, nw_ref[...]).astype(oh_ref.dtype)


def _mm_res(a, w, wl, x, post_w, next_w, layer, tm=512):
    M, K = a.shape
    D = w.shape[2]
    tm = min(tm, M)
    vm = 2 * (_nbytes((tm, K), a.dtype) + _nbytes((K, D), BF16) + 2 * _nbytes((tm, D), F32)
              + _nbytes((tm, D), BF16)) + 2 * _nbytes((tm, D), F32)
    return pl.pallas_call(
        _mm_res_kernel,
        grid=(M // tm,),
        in_specs=[pl.BlockSpec((tm, K), lambda i: (i, 0)),
                  pl.BlockSpec((None, K, D), lambda i: (wl, 0, 0)),
                  pl.BlockSpec((tm, D), lambda i: (i, 0)), _row(layer)(D), _row(layer)(D)],
        out_specs=[pl.BlockSpec((tm, D), lambda i: (i, 0)), pl.BlockSpec((tm, D), lambda i: (i, 0))],
        out_shape=[jax.ShapeDtypeStruct((M, D), F32), jax.ShapeDtypeStruct((M, D), BF16)],
        compiler_params=_cparams(("parallel",), vm),
        name="matmul_norm_residual",
    )(a, w, x, post_w, next_w)


def _mlp_kernel(h_ref, wu_ref, wd_ref, x_ref, pw_ref, nw_ref, ox_ref, *rest, emit_next, hand_off):
    acc_ref = rest[-1]
    f_axis = 0 if hand_off else 1
    f = pl.program_id(f_axis)
    i = pl.program_id(1 - f_axis)
    ai = i if hand_off else 0

    @pl.when(f == 0)
    def _():
        acc_ref[ai] = jnp.zeros(acc_ref.shape[1:], F32)

    if hand_off:
        wu_ref_b, wd_ref_b = rest[-3], rest[-2]

        @pl.when(i == 0)
        def _():
            wu_ref_b[...] = wu_ref[...].astype(BF16)
            wd_ref_b[...] = wd_ref[...].astype(BF16)
    else:
        wu_ref_b, wd_ref_b = wu_ref, wd_ref
    tm = ox_ref.shape[0]
    h = h_ref[pl.ds(pl.multiple_of(i * tm, tm), tm), :] if hand_off else h_ref[...]
    u = jnp.maximum(_dot(h, wu_ref_b[...]), 0.0)
    acc_ref[ai] += _dot((u * u).astype(BF16), wd_ref_b[...])

    @pl.when(f == pl.num_programs(f_axis) - 1)
    def _():
        y = x_ref[...] + _rms(acc_ref[ai], pw_ref[...])
        ox_ref[...] = y
        if emit_next:
            rest[0][...] = _rms(y, nw_ref[...]).astype(BF16)


def _mlp(h, w_up, w_down, wl, x, post_w, next_w, layer, tm=512, tf=1024):
    M, D = h.shape
    FF = w_up.shape[2]
    tm = min(tm, M)
    hand_off = w_up.dtype != BF16
    if hand_off:
        tf = min(tf, 512)
    emit_next = layer + 1 < next_w.shape[0]
    nt, nf = M // tm, FF // tf
    if hand_off:
        ix = lambda fn: (lambda f, i: fn(i, f))
        last = lambda i, f: (jnp.where(f == nf - 1, i, 0), 0)
        grid, sem, n_acc = (nf, nt), ("arbitrary", "arbitrary"), nt
    else:
        ix = lambda fn: fn
        last = lambda i, f: (i, 0)
        grid, sem, n_acc = (nt, nf), ("parallel", "arbitrary"), 1
    tile = lambda: pl.BlockSpec((tm, D), ix(last))
    out_specs = [tile()] + ([tile()] if emit_next else [])
    out_shape = [jax.ShapeDtypeStruct((M, D), F32)] + ([jax.ShapeDtypeStruct((M, D), BF16)] if emit_next else [])
    h_rows = M if hand_off else tm
    vm = 2 * (_nbytes((h_rows, D), BF16) + _nbytes((tm, D), BF16) + 2 * _nbytes((D, tf), w_up.dtype)
              + 2 * _nbytes((tm, D), F32))
    vm += (1 + n_acc) * _nbytes((tm, D), F32) + 2 * _nbytes((tm, tf), F32)
    if hand_off:
        out_specs += [pl.BlockSpec((None, D, tf), ix(lambda i, f: (0, 0, f))),
                      pl.BlockSpec((None, tf, D), ix(lambda i, f: (0, f, 0)))]
        out_shape += [jax.ShapeDtypeStruct((1, D, FF), BF16), jax.ShapeDtypeStruct((1, FF, D), BF16)]
        vm += 4 * _nbytes((D, tf), BF16)
    outs = pl.pallas_call(
        functools.partial(_mlp_kernel, emit_next=emit_next, hand_off=hand_off),
        grid=grid,
        in_specs=[pl.BlockSpec((M, D), lambda f, i: (0, 0)) if hand_off
                  else pl.BlockSpec((tm, D), lambda i, f: (i, 0)),
                  pl.BlockSpec((None, D, tf), ix(lambda i, f: (wl, 0, f))),
                  pl.BlockSpec((None, tf, D), ix(lambda i, f: (wl, f, 0))),
                  tile(), _row(layer)(D), _row(layer + 1 if emit_next else layer)(D)],
        out_specs=out_specs,
        out_shape=out_shape,
        scratch_shapes=[pltpu.VMEM((n_acc, tm, D), F32)],
        compiler_params=_cparams(sem, vm),
        name="relu2_mlp",
    )(h, w_up, w_down, x, post_w, next_w)
    n = 2 if emit_next else 1
    return outs[0], (outs[1] if emit_next else None), (tuple(outs[n:]) if hand_off else None)


def _split3(x):
    hi = x.astype(BF16)
    r1 = x - hi.astype(F32)
    mid = r1.astype(BF16)
    lo = (r1 - mid.astype(F32)).astype(BF16)
    return hi, mid, lo


def _decay_rows(d, rows):
    hi, mid, lo = _split3(d)
    r = lax.broadcasted_iota(jnp.int32, (rows, d.shape[-1]), 0)
    return jnp.where(r == 0, hi.astype(F32), jnp.where(r == 1, mid.astype(F32),
                                                       jnp.where(r == 2, lo.astype(F32), 0.0)))


def _gla_prompt_kernel(*refs, decode_heads=None):
    if decode_heads is None:
        (q_ref, k_ref, lf_ref, v_ref, g_ref, nw_ref, sel_ref, lvl_ref, o_ref, sout_ref,
         s_scr, qs_scr, ks_scr, ke_scr, fac_scr, a_scr) = refs
    else:
        (q_ref, k_ref, lf_ref, v_ref, g_ref, nw_ref, sel_ref, lvl_ref, dq_ref, dk_ref, dv_ref,
         o_ref, sout_ref, do_ref, s_scr, qs_scr, ks_scr, ke_scr, fac_scr, a_scr) = refs
    T = q_ref.shape[1]
    D = q_ref.shape[2]
    H = D // HEAD_DIM
    SUB = GLA_SUB
    nb = T // SUB
    c_idx = pl.program_id(1)

    @pl.when(c_idx == 0)
    def _():
        s_scr[...] = jnp.zeros_like(s_scr)

    side = iter(()) if decode_heads is None else _attn_decode_pieces(dq_ref, dk_ref, dv_ref, do_ref,
                                                                     heads=decode_heads)
    stage = [0]

    def side_step():
        stage[0] += 1
        if stage[0] % SIDE_EVERY == 0:
            next(side, None)

    chunks = []
    c = 2 * SUB
    while c < T:
        chunks.append(c)
        c *= 2
    nf = 2 * len(chunks) + 3
    sums = _dot(sel_ref[...], jnp.concatenate(_split3(lf_ref[0]), axis=0))
    fac_scr[...] = jnp.exp(sums[T:])
    nq = len(chunks) + 1
    for i in range(nb):
        rows = slice(i * SUB, (i + 1) * SUB)
        r = sums[rows]
        q16 = q_ref[0, rows, :].astype(F32) * jnp.exp(r)
        kd = k_ref[0, rows, :].astype(F32) * jnp.exp(-r)
        f_row = lambda n: fac_scr[n * nb + i:n * nb + i + 1, :]
        qs_scr[0, rows, :] = (q16 * f_row(0)).astype(BF16)
        qs_scr[1, rows, :] = q16.astype(BF16)
        ks_scr[0, rows, :] = kd.astype(BF16)
        ks_scr[1, rows, :] = (kd * f_row(nq)).astype(BF16)
        for n in range(len(chunks)):
            qs_scr[n + 2, rows, :] = (q16 * f_row(n + 1)).astype(BF16)
            ks_scr[n + 2, rows, :] = (kd * f_row(nq + n + 1)).astype(BF16)
        ke_scr[rows, :] = (kd * f_row(nf - 1)).astype(BF16)
        side_step()
    ke_scr[T:, :] = _decay_rows(fac_scr[(nf - 1) * nb:(nf - 1) * nb + 1, :], SUB).astype(BF16)

    rr = lax.broadcasted_iota(jnp.int32, (SUB, 2 * HEAD_DIM), 0)
    cc = lax.broadcasted_iota(jnp.int32, (SUB, 2 * HEAD_DIM), 1)
    pick = jnp.where((rr < 3) & (cc >= HEAD_DIM), 1.0, 0.0).astype(BF16)
    zeros = jnp.zeros((T, HEAD_DIM), BF16)
    nw = nw_ref[...]

    lvl = lvl_ref[...]
    for h in range(H):
        hs = slice(h * HEAD_DIM, (h + 1) * HEAD_DIM)
        r = _dot_nt(qs_scr[1, :, hs], ks_scr[0:2, :, hs].reshape(2 * T, HEAD_DIM))
        a = jnp.where(lvl == 1, r[:, :T], jnp.where(lvl == 2, r[:, T:], 0.0))
        for n in range(len(chunks)):
            a = jnp.where(lvl == n + 3, _dot_nt(qs_scr[n + 2, :, hs], ks_scr[n + 2, :, hs]), a)
        a_scr[h] = a.astype(BF16)
        side_step()
    for h in range(H):
        hs = slice(h * HEAD_DIM, (h + 1) * HEAD_DIM)
        vh = v_ref[0, :, hs]
        s_old = s_scr[h]
        o = _dot(jnp.concatenate([a_scr[h], qs_scr[0, :, hs]], axis=1),
                 jnp.concatenate([vh, s_old.astype(BF16)], axis=0))
        rhs = jnp.concatenate([jnp.concatenate([vh, zeros], axis=1), pick], axis=0)
        upd = _dot_tn(ke_scr[:, hs], rhs)
        s_scr[h] = s_old * upd[:, HEAD_DIM:] + upd[:, :HEAD_DIM]
        o_ref[0, :, hs] = (_rms(o, nw) * g_ref[0, :, hs].astype(F32)).astype(o_ref.dtype)
        side_step()
    for _ in side:
        pass

    @pl.when(c_idx == pl.num_programs(1) - 1)
    def _():
        sout_ref[0] = s_scr[...]


def _gla_range_matrix(T, SUB):
    chunks = []
    c = 2 * SUB
    while c < T:
        chunks.append(c)
        c *= 2
    nb = T // SUB
    t = np.arange(T)
    rows = [((t[None, :] <= t[:, None]) & (t[None, :] // SUB == t[:, None] // SUB))]
    start = np.arange(nb)[:, None] * SUB
    for C in [T] + chunks:
        rows.append((t[None, :] >= (start // C) * C) & (t[None, :] < start))
    for C in [SUB] + chunks + [T]:
        rows.append((t[None, :] >= start) & (t[None, :] < (start // C + 1) * C))
    sel = np.concatenate(rows, axis=0).astype(np.float32)
    row, col = t[:, None], t[None, :]
    lvl = np.where((row // SUB == col // SUB) & (col <= row), 1, 0)
    for j, C in enumerate([SUB] + chunks):
        lvl = np.where(((row // C) % 2 == 1) & (col // C == row // C - 1), 2 + j, lvl)
    return sel, lvl.astype(np.int32), len(chunks)


def _gla_prompt(q, k, lf, v, g, norm_w, mixer, B, L, decode=None):
    D = q.shape[-1]
    H = D // HEAD_DIM
    T = GLA_BLOCK
    assert T == HEAD_DIM
    nc = L // T
    sel, lvl, nchunks = _gla_range_matrix(T, GLA_SUB)
    sel3 = jnp.asarray(np.concatenate([sel] * 3, axis=1), BF16)
    nfac = sel.shape[0] - T
    r3 = lambda a: a.reshape(B, L, D)
    blk = lambda: pl.BlockSpec((1, T, D), lambda b, c: (b, c, 0))
    const = lambda a: pl.BlockSpec(a.shape, lambda b, c: (0, 0))
    vm = 2 * (5 * _nbytes((T, D), BF16) + _nbytes((T, D), F32)) + 3 * _nbytes((H, HEAD_DIM, HEAD_DIM), F32)
    vm += (2 * nchunks + 6) * _nbytes((T, D), BF16) + 8 * _nbytes((T, D), F32)
    in_specs = [blk(), blk(), blk(), blk(), blk(), _row(mixer)(HEAD_DIM), const(sel3), const(lvl)]
    args = [r3(q), r3(k), r3(lf), r3(v), r3(g), norm_w, sel3, jnp.asarray(lvl)]
    out_specs = [blk(), pl.BlockSpec((1, H, HEAD_DIM, HEAD_DIM), lambda b, c: (b, 0, 0, 0))]
    out_shape = [jax.ShapeDtypeStruct((B, L, D), BF16), jax.ShapeDtypeStruct((B, H, HEAD_DIM, HEAD_DIM), F32)]
    heads = None
    if decode is not None:
        dq, ck, cv, layer, Bd, Ld, heads, d_dtype = decode
        assert Ld == SUBL and SUBL % heads == 0 and Bd % (B * nc) == 0
        bb = Bd // (B * nc)
        kv_shape = (bb,) + ck.shape[2:]
        kv_spec = lambda: pl.BlockSpec((None,) + kv_shape, lambda b, c: (layer, b * nc + c, 0, 0, 0, 0))
        dblk = lambda: pl.BlockSpec((bb, Ld, D), lambda b, c: (b * nc + c, 0, 0))
        in_specs += [dblk(), kv_spec(), kv_spec()]
        args += [dq.reshape(Bd, Ld, D), ck, cv]
        out_specs.append(dblk())
        out_shape.append(jax.ShapeDtypeStruct((Bd, Ld, D), d_dtype))
        vm += 4 * _nbytes((bb, Ld, D), F32) + 4 * _nbytes(kv_shape, ck.dtype)
        vm += 4 * _nbytes(kv_shape[1:], BF16) + 6 * _nbytes((SUBL * Ld, ck.shape[2] * SUBL), F32)
    outs = pl.pallas_call(
        functools.partial(_gla_prompt_kernel, decode_heads=heads),
        grid=(B, nc),
        in_specs=in_specs,
        out_specs=out_specs,
        out_shape=out_shape,
        scratch_shapes=[pltpu.VMEM((H, HEAD_DIM, HEAD_DIM), F32),
                        pltpu.VMEM((nchunks + 2, T, D), BF16),
                        pltpu.VMEM((nchunks + 2, T, D), BF16),
                        pltpu.VMEM((T + GLA_SUB, D), BF16),
                        pltpu.VMEM((nfac, D), F32),
                        pltpu.VMEM((H, T, T), BF16)],
        compiler_params=_cparams(("parallel", "arbitrary"), vm),
        name="hgrn2_recurrence_prompt",
    )(*args)
    a_dec = outs[2].reshape(Bd * Ld, D) if decode is not None else None
    return outs[0].reshape(B * L, D), outs[1], a_dec


def _gla_step_kernel(q_ref, k_ref, lf_ref, v_ref, g_ref, s0_ref, nw_ref, o_ref, sout_ref,
                     qs_scr, ks_scr, ke_scr, ds_scr):
    bb, T, D = q_ref.shape
    H = D // HEAD_DIM
    R = bb * T
    pos = lax.broadcasted_iota(jnp.int32, (bb, T, D), 1)
    b = lf_ref[...]
    sh = 1
    while sh < T:
        b = b + jnp.where(pos >= sh, pltpu.roll(b, sh, 1), 0.0)
        sh *= 2
    b_last = b[:, T - 1:T, :]
    k = k_ref[...]
    qs_scr[...] = q_ref[...] * jnp.exp(b)
    ks_scr[...] = k * jnp.exp(-b)
    ke_scr[...] = k * jnp.exp(b_last - b)
    d_hi, d_mid, d_lo = (t.astype(F32) for t in _split3(jnp.exp(b_last)))
    ds_scr[...] = jnp.where(pos == 0, d_hi, jnp.where(pos == 1, d_mid, jnp.where(pos == 2, d_lo, 0.0)))

    row = lax.broadcasted_iota(jnp.int32, (R, R), 0)
    col = lax.broadcasted_iota(jnp.int32, (R, R), 1)
    same_seq_causal = (row // T == col // T) & (col <= row)
    rr = lax.broadcasted_iota(jnp.int32, (T, 2 * HEAD_DIM), 0)
    cc = lax.broadcasted_iota(jnp.int32, (T, 2 * HEAD_DIM), 1)
    pick = jnp.where((rr < 3) & (cc >= HEAD_DIM), 1.0, 0.0)
    zeros = jnp.zeros((T, HEAD_DIM), F32)
    nw = nw_ref[...]

    for h in range(H):
        hs = slice(h * HEAD_DIM, (h + 1) * HEAD_DIM)
        qd = qs_scr[:, :, hs].reshape(R, HEAD_DIM)
        kd = ks_scr[:, :, hs].reshape(R, HEAD_DIM)
        vh = v_ref[:, :, hs].reshape(R, HEAD_DIM)
        a = jnp.where(same_seq_causal, _dot_nt(qd.astype(BF16), kd.astype(BF16)), 0.0)
        o_intra = _dot(a.astype(BF16), vh.astype(BF16))
        outs = []
        for i in range(bb):
            s_old = s0_ref[i, h]
            outs.append(o_intra[i * T:(i + 1) * T]
                        + _dot(qd[i * T:(i + 1) * T].astype(BF16), s_old.astype(BF16)))
            lhs = jnp.concatenate([ke_scr[i, :, hs], ds_scr[i, :, hs]], axis=0).astype(BF16)
            rhs = jnp.concatenate([jnp.concatenate([vh[i * T:(i + 1) * T], zeros], axis=1), pick],
                                  axis=0).astype(BF16)
            upd = _dot_tn(lhs, rhs)
            sout_ref[i, h] = s_old * upd[:, HEAD_DIM:] + upd[:, :HEAD_DIM]
        o = jnp.concatenate(outs, axis=0)
        on = _rms(o, nw) * g_ref[:, :, hs].reshape(R, HEAD_DIM)
        o_ref[:, :, hs] = on.reshape(bb, T, HEAD_DIM).astype(o_ref.dtype)


def _gla_step(q, k, lf, v, g, s0, norm_w, mixer, B, L, bb=8):
    D = q.shape[-1]
    H = D // HEAD_DIM
    assert L == 8 and B % bb == 0
    r3 = lambda a: a.reshape(B, L, D)
    blk = lambda: pl.BlockSpec((bb, L, D), lambda b: (b, 0, 0))
    sblk = lambda: pl.BlockSpec((bb, H, HEAD_DIM, HEAD_DIM), lambda b: (b, 0, 0, 0))
    vm = 2 * (6 * _nbytes((bb, L, D), F32) + 2 * _nbytes((bb, H, HEAD_DIM, HEAD_DIM), F32))
    vm += 12 * _nbytes((bb, L, D), F32)
    o, s = pl.pallas_call(
        _gla_step_kernel,
        grid=(B // bb,),
        in_specs=[blk(), blk(), blk(), blk(), blk(),
                  pl.BlockSpec((None, bb, H, HEAD_DIM, HEAD_DIM), lambda b: (mixer, b, 0, 0, 0)),
                  _row(mixer)(HEAD_DIM)],
        out_specs=[blk(), sblk()],
        out_shape=[jax.ShapeDtypeStruct((B, L, D), F32),
                   jax.ShapeDtypeStruct((B, H, HEAD_DIM, HEAD_DIM), F32)],
        scratch_shapes=[pltpu.VMEM((bb, L, D), F32)] * 4,
        compiler_params=_cparams(("parallel",), vm),
        name="hgrn2_recurrence_step",
    )(r3(q), r3(k), r3(lf), r3(v), r3(g), s0, norm_w)
    return o.reshape(B * L, D), s


def _pool_kernel(*refs, pos0, has_halo):
    if has_halo:
        u_ref, halo_ref, buf_ref, pw_ref, ps_ref, o_ref, nb_ref, ext = refs
    else:
        u_ref, buf_ref, pw_ref, ps_ref, o_ref, nb_ref, ext = refs
    bb, tl, D = u_ref.shape
    G = len(POOL_WINDOWS)
    gc = D // G
    l_idx = pl.program_id(1)
    ext[:, HALO:, :] = u_ref[...]
    if has_halo:
        @pl.when(l_idx == 0)
        def _():
            ext[:, :HALO, :] = buf_ref[...]

        @pl.when(l_idx > 0)
        def _():
            ext[:, :HALO, :] = halo_ref[...]
    else:
        ext[:, :HALO, :] = buf_ref[...]

    pos = pos0 + l_idx * tl + lax.broadcasted_iota(jnp.int32, (tl, gc), 0)
    for gi, w in enumerate(POOL_WINDOWS):
        ls = slice(gi * gc, (gi + 1) * gc)
        cnt = jnp.minimum(w, pos + 1).astype(F32)
        rows = []
        for i in range(bb):
            u = ext[i, HALO:HALO + tl, ls]
            s = u
            for d in range(1, w):
                s = s + ext[i, HALO - d:HALO - d + tl, ls]
            rows.append(s / cnt - u)
        pooled = rows[0] if bb == 1 else jnp.concatenate(rows, axis=0)
        mixed = (_dot(pooled.astype(BF16), pw_ref[gi]) * ps_ref[:, ls]).astype(o_ref.dtype)
        for i in range(bb):
            o_ref[i, :, ls] = mixed[i * tl:(i + 1) * tl]

    @pl.when(l_idx == pl.num_programs(1) - 1)
    def _():
        nb_ref[...] = ext[:, tl:tl + HALO, :]


def _pool(u, buf, pos0, pool_w, pool_scale, mixer, B, L, bb, tl, out_dtype):
    D = u.shape[-1]
    G = len(POOL_WINDOWS)
    gc = D // G
    tl = min(tl, L)
    u3 = u.reshape(B, L, D)
    has_halo = L > tl
    step = tl // HALO
    in_specs = [pl.BlockSpec((bb, tl, D), lambda b, l: (b, l, 0))]
    args = [u3]
    if has_halo:
        in_specs.append(pl.BlockSpec((bb, HALO, D), lambda b, l: (b, jnp.maximum(l * step - 1, 0), 0)))
        args.append(u3)
    in_specs += [pl.BlockSpec((None, bb, HALO, D), lambda b, l: (mixer, b, 0, 0)),
                 pl.BlockSpec((None, G, gc, gc), lambda b, l: (mixer, 0, 0, 0)),
                 _row(mixer)(D)]
    args += [buf, pool_w, pool_scale]
    vm = 2 * (_nbytes((bb, tl, D), F32) + 3 * _nbytes((bb, HALO, D), F32) + _nbytes((G, gc, gc), BF16)
              + _nbytes((bb, tl, D), BF16)) + _nbytes((bb, tl + HALO, D), F32) + 4 * _nbytes((bb, tl, D), F32)
    mixed, newbuf = pl.pallas_call(
        functools.partial(_pool_kernel, pos0=pos0, has_halo=has_halo),
        grid=(B // bb, L // tl),
        in_specs=in_specs,
        out_specs=[pl.BlockSpec((bb, tl, D), lambda b, l: (b, l, 0)),
                   pl.BlockSpec((bb, HALO, D), lambda b, l: (b, 0, 0))],
        out_shape=[jax.ShapeDtypeStruct((B, L, D), out_dtype), jax.ShapeDtypeStruct((B, HALO, D), F32)],
        scratch_shapes=[pltpu.VMEM((bb, tl + HALO, D), F32)],
        compiler_params=_cparams(("parallel", "arbitrary"), vm),
        name="causal_pool",
    )(*args)
    return mixed.reshape(B * L, D), newbuf


def _attn_kernel(q_ref, k_ref, v_ref, o_ref, *, heads):
    bb, tq, D = q_ref.shape
    hd = D // heads
    scale = hd ** -0.5
    for i in range(bb):
        for h in range(heads):
            hs = slice(h * hd, (h + 1) * hd)
            s = _dot_nt(q_ref[i, :, hs].astype(BF16), k_ref[i, :, hs].astype(BF16)) * scale
            p = jnp.exp(s - jnp.max(s, axis=-1, keepdims=True))
            den = jnp.sum(p, axis=-1, keepdims=True)
            o_ref[i, :, hs] = (_dot(p.astype(BF16), v_ref[i, :, hs].astype(BF16)) / den).astype(o_ref.dtype)


def _attn(q, mk, mv, layer, B, L, heads, bb, tq, out_dtype):
    D = q.shape[-1]
    tq = min(tq, L)
    kv_shape = (bb,) + mk.shape[2:]
    kv_blk = (None,) + kv_shape
    vm = 2 * (_nbytes((bb, tq, D), q.dtype) + 2 * _nbytes(kv_shape, mk.dtype) + _nbytes((bb, tq, D), out_dtype))
    vm += 2 * _nbytes(kv_shape, BF16) + 8 * _nbytes((tq, D // heads), F32)
    o = pl.pallas_call(
        functools.partial(_attn_kernel, heads=heads),
        grid=(B // bb, L // tq),
        in_specs=[pl.BlockSpec((bb, tq, D), lambda b, l: (b, l, 0)),
                  pl.BlockSpec(kv_blk, lambda b, l: (layer, b, 0, 0)),
                  pl.BlockSpec(kv_blk, lambda b, l: (layer, b, 0, 0))],
        out_specs=pl.BlockSpec((bb, tq, D), lambda b, l: (b, l, 0)),
        out_shape=jax.ShapeDtypeStruct((B, L, D), out_dtype),
        compiler_params=_cparams(("parallel", "parallel"), vm),
        name="memory_cross_attention",
    )(q.reshape(B, L, D), mk, mv)
    return o.reshape(B * L, D)


def _attn_decode_pieces(q_ref, k_ref, v_ref, o_ref, *, heads):
    bb, tq, D = q_ref.shape
    n_mem, nu = k_ref.shape[1], k_ref.shape[2]
    hd = D // heads
    per = SUBL // heads
    W = n_mem * SUBL
    scale = hd ** -0.5
    res = lax.broadcasted_iota(jnp.int32, (tq, W), 1) % SUBL
    tile = lambda r: jnp.concatenate([r] * (W // LANES), axis=1)

    def feat(j):
        c, h = divmod(j, heads)
        return slice(h * hd + c * LANES, h * hd + (c + 1) * LANES)

    def same_row_reduce(x, op):
        r = x[:, :LANES]
        for j in range(1, W // LANES):
            r = op(r, x[:, j * LANES:(j + 1) * LANES])
        sh = SUBL
        while sh < LANES:
            r = op(r, pltpu.roll(r, sh, 1))
            sh *= 2
        return r

    for i in range(bb):
        w = None
        for u in range(nu):
            slab = k_ref[i, :, u].reshape(W, LANES).astype(BF16)
            qm = jnp.concatenate([q_ref[i, :, feat(u * SUBL + s)] for s in range(SUBL)], axis=0)
            p = _dot_nt(qm.astype(BF16), slab)
            z = p[:tq]
            for s in range(1, SUBL):
                z = jnp.where(res == s, p[s * tq:(s + 1) * tq], z)
            w = z if w is None else w + z
            yield
        sc = w
        for c in range(1, per):
            sc = sc + pltpu.roll(w, W - c * heads, 1)
        sc = sc * scale
        e = jnp.exp(sc - tile(same_row_reduce(sc, jnp.maximum)))
        pn = e / tile(same_row_reduce(e, jnp.add))
        blocks = []
        for s in range(SUBL):
            c_lo = s // heads
            src = pn if c_lo == 0 else pltpu.roll(pn, c_lo * heads, 1)
            blocks.append(jnp.where(res == s, src, 0.0))
        pm = jnp.concatenate(blocks, axis=0).astype(BF16)
        yield
        for u in range(nu):
            o = _dot(pm, v_ref[i, :, u].reshape(W, LANES).astype(BF16))
            for s in range(SUBL):
                o_ref[i, :, feat(u * SUBL + s)] = o[s * tq:(s + 1) * tq].astype(o_ref.dtype)
            yield


def _attn_decode_kernel(q_ref, k_ref, v_ref, o_ref, *, heads):
    for _ in _attn_decode_pieces(q_ref, k_ref, v_ref, o_ref, heads=heads):
        pass


def _attn_decode(q, ck, cv, layer, B, L, heads, bb, out_dtype):
    D = q.shape[-1]
    assert L == SUBL and SUBL % heads == 0 and D % (SUBL * LANES) == 0
    kv_shape = (bb,) + ck.shape[2:]
    vm = 2 * (2 * _nbytes((bb, L, D), F32) + 2 * _nbytes(kv_shape, ck.dtype))
    vm += 4 * _nbytes(kv_shape[1:], BF16) + 6 * _nbytes((SUBL * L, ck.shape[2] * SUBL), F32)
    o = pl.pallas_call(
        functools.partial(_attn_decode_kernel, heads=heads),
        grid=(B // bb,),
        in_specs=[pl.BlockSpec((bb, L, D), lambda b: (b, 0, 0)),
                  pl.BlockSpec((None,) + kv_shape, lambda b: (layer, b, 0, 0, 0, 0)),
                  pl.BlockSpec((None,) + kv_shape, lambda b: (layer, b, 0, 0, 0, 0))],
        out_specs=pl.BlockSpec((bb, L, D), lambda b: (b, 0, 0)),
        out_shape=jax.ShapeDtypeStruct((B, L, D), out_dtype),
        compiler_params=_cparams(("parallel",), vm),
        name="memory_cross_attention_decode",
    )(q.reshape(B, L, D), ck, cv)
    return o.reshape(B * L, D)


def _mem_proj_kernel(x_ref, nw_ref, w_ref, flat_ref, bf_ref, w_scr, *, heads):
    tm, D = x_ref.shape
    R = D // LANES
    nchunk = R // heads

    @pl.when(pl.program_id(1) == 0)
    def _():
        w_scr[...] = w_ref[...].astype(BF16)

    acc = _dot(_rms(x_ref[...], nw_ref[...]).astype(BF16), w_scr[...])
    bf_ref[...] = acc.astype(bf_ref.dtype)
    for h in range(heads):
        for c in range(nchunk):
            j = h * nchunk + c
            flat_ref[pl.ds(c * heads + h, tm, stride=R), :] = acc[:, j * LANES:(j + 1) * LANES]


def _mem_proj(mem, norm_w, w, half, heads, tm=256):
    M, D = mem.shape
    depth = w.shape[0]
    R = D // LANES
    tm = min(tm, M)
    nt = M // tm
    vm = 2 * (_nbytes((tm, D), F32) + _nbytes((D, D), F32) + _nbytes((tm, D), F32) + _nbytes((tm, D), BF16))
    vm += _nbytes((D, D), BF16) + 2 * _nbytes((tm, D), F32)
    return pl.pallas_call(
        functools.partial(_mem_proj_kernel, heads=heads),
        grid=(depth, nt),
        in_specs=[pl.BlockSpec((tm, D), lambda l, i: (i, 0)),
                  pl.BlockSpec((None, 1, D), lambda l, i: (l, 0, 0)),
                  pl.BlockSpec((None, D, D), lambda l, i: (l, 0, half))],
        out_specs=[pl.BlockSpec((tm * R, LANES), lambda l, i: (l * nt + i, 0)),
                   pl.BlockSpec((None, tm, D), lambda l, i: (l, i, 0))],
        out_shape=[jax.ShapeDtypeStruct((depth * M * R, LANES), F32),
                   jax.ShapeDtypeStruct((depth, M, D), BF16)],
        scratch_shapes=[pltpu.VMEM((D, D), BF16)],
        compiler_params=_cparams(("parallel", "arbitrary"), vm),
        name="memory_kv_projection",
    )(mem, norm_w, w)


class _Group:
    def __init__(self, x, B, L, pos0, S_in, buf_in, mk, mv, prompt, p):
        self.B, self.L, self.pos0, self.S_in, self.buf_in, self.mk, self.mv = B, L, pos0, S_in, buf_in, mk, mv
        self.prompt = prompt
        self.act = BF16 if prompt else F32
        self.x = x
        self.hn = None if prompt else _rmsnorm(x, p["norm_mix_pre"], 0)
        self.S_out, self.buf_out = [], []


class _NoRider:
    def job(self):
        return None

    def served(self, a):
        pass


def _mixer(g, l, p, rider=_NoRider()):
    j = l // 2
    if l % 2 == 0:
        w, wl = _weight(p, "w_in_a", j)
        h_in, prenorm = (g.x, (p["norm_mix_pre"], l)) if g.hn is None else (g.hn, None)
        q, k, lf, v, gate, cast, a = _proj_a(h_in, w, wl, p["hg_lb_logits"], l, g.act,
                                             rider.job() if g.prompt else None, prenorm)
        _hand_off(p, "w_in_a", j, cast)
        rider.served(a)
        if g.prompt:
            o, S, a = _gla_prompt(q, k, lf, v, gate, p["hg_norm"], j, g.B, g.L, rider.job())
            rider.served(a)
        else:
            o, S = _gla_step(q, k, lf, v, gate, g.S_in, p["hg_norm"], j, g.B, g.L)
        g.S_out.append(S)
        g.x, g.hn = _mm_res(o, p["w_out_a"], j, g.x, p["norm_mix_post"], p["norm_x_pre"], l)
    else:
        u = _project(g, p, "w_in_b", j, F32)
        bb, tl = (1, 512) if g.prompt else (16, g.L)
        mixed, nb = _pool(u, g.buf_in, g.pos0, p["pool_w"], p["pool_scale"], j, g.B, g.L, bb, tl, g.act)
        g.buf_out.append(nb[:, 1:, :])
        g.x, g.hn = _mm_res(mixed, p["w_out_b"], j, g.x, p["norm_mix_post"], p["norm_x_pre"], l)


def _weight(p, name, idx):
    cast = p["cast"].get((name, idx))
    return (cast, 0) if cast is not None else (p[name], idx)


def _hand_off(p, name, idx, cast):
    if cast is not None:
        p["cast"][(name, idx)] = cast


def _project(g, p, name, idx, dtype):
    w, wl = _weight(p, name, idx)
    outs = _matmul(g.hn, w, wl, [dtype])
    _hand_off(p, name, idx, outs[1] if len(outs) > 1 else None)
    return outs[0]


def _after_attention(g, l, a, p):
    g.x, g.hn = _mm_res(a, p["w_xo"], l, g.x, p["norm_x_post"], p["norm_mlp_pre"], l)
    (wu, wl), (wd, _) = _weight(p, "w_up", l), _weight(p, "w_down", l)
    g.x, g.hn, cast = _mlp(g.hn, wu, wd, wl, g.x, p["norm_mlp_post"], p["norm_mix_pre"], l)
    if cast is not None:
        _hand_off(p, "w_up", l, cast[0])
        _hand_off(p, "w_down", l, cast[1])


class _SampleWalk:
    def __init__(self, g, p):
        self.g, self.p = g, p
        self._walk = self._layers()
        self._pending = next(self._walk, None)

    def _layers(self):
        g, p = self.g, self.p
        for l in range(p["w_xq"].shape[0]):
            _mixer(g, l, p)
            q = _project(g, p, "w_xq", l, g.act)
            a = yield (q, g.mk, g.mv, l, g.B, g.L, p["mem_heads"], g.act)
            _after_attention(g, l, a, p)

    def job(self):
        return self._pending

    def served(self, a):
        if a is not None:
            try:
                self._pending = self._walk.send(a)
            except StopIteration:
                self._pending = None

    def serve_standalone(self):
        q, mk, mv, l, B, L, heads, dtype = self._pending
        self.served(_attn_decode(q, mk, mv, l, B, L, heads, 4, dtype))


def _trunks(gp, gs, p):
    heads = p["mem_heads"]
    rider = _SampleWalk(gs, p)
    for l in range(p["w_xq"].shape[0]):
        if l % 2 == 1 and rider.job() is not None:
            rider.serve_standalone()
        _mixer(gp, l, p, rider)
        qp = _project(gp, p, "w_xq", l, gp.act)
        _after_attention(gp, l, _attn(qp, gp.mk, gp.mv, l, gp.B, gp.L, heads, 1, 512, gp.act), p)
    while rider.job() is not None:
        rider.serve_standalone()


def kernel(x_prompt, x_sample, state_hgrn, state_pool, cache_mem_k, cache_mem_v, mem_prompt, w_in_a, hg_lb_logits, hg_norm, w_out_a, w_in_b, pool_w, pool_scale, w_out_b, norm_mem, w_xq, w_xkv, w_xo, norm_mix_pre, norm_mix_post, norm_x_pre, norm_x_post, norm_mlp_pre, norm_mlp_post, w_up, w_down):
    B, L, D = x_prompt.shape
    Bs, Ls, _ = x_sample.shape
    depth = w_xq.shape[0]
    n_mem, heads = cache_mem_k.shape[2], cache_mem_k.shape[3]
    assert L % GLA_BLOCK == 0 and Ls <= GLA_SUB and D % (HEAD_DIM * len(POOL_WINDOWS)) == 0
    assert state_pool.shape[2] == HALO - 1
    bf = lambda a: a.astype(BF16)
    rows = lambda a: a.reshape(a.shape[0], 1, a.shape[1])
    p = dict(w_in_a=w_in_a, hg_lb_logits=hg_lb_logits, hg_norm=rows(hg_norm), w_out_a=bf(w_out_a),
             w_in_b=w_in_b, pool_w=bf(pool_w), pool_scale=rows(pool_scale), w_out_b=bf(w_out_b),
             w_xq=w_xq, w_xo=bf(w_xo), norm_mix_pre=rows(norm_mix_pre), norm_mix_post=rows(norm_mix_post),
             norm_x_pre=rows(norm_x_pre), norm_x_post=rows(norm_x_post), norm_mlp_pre=rows(norm_mlp_pre),
             norm_mlp_post=rows(norm_mlp_post), w_up=w_up, w_down=w_down, mem_heads=heads, cast={})
    R = D // LANES
    nchunk = R // heads

    def cache_view(flat, nb):
        c = flat.reshape(depth, nb, n_mem, nchunk, heads, LANES)
        return jnp.swapaxes(c, 3, 4).reshape(depth, nb, n_mem, heads, D // heads)

    def flat_view(cache):
        nb = cache.shape[1]
        c = cache.reshape(depth, nb, n_mem, heads, nchunk, LANES)
        return jnp.swapaxes(c, 3, 4).reshape(depth, nb, n_mem, R // SUBL, SUBL, LANES)

    mem2d = mem_prompt.reshape(B * n_mem, D)
    k_flat, k_bf = _mem_proj(mem2d, rows(norm_mem), w_xkv, 0, heads)
    v_flat, v_bf = _mem_proj(mem2d, rows(norm_mem), w_xkv, 1, heads)
    cache_mem_k_prompt = cache_view(k_flat, B)
    cache_mem_v_prompt = cache_view(v_flat, B)
    mk_bf = k_bf.reshape(depth, B, n_mem, D)
    mv_bf = v_bf.reshape(depth, B, n_mem, D)

    n_b = state_pool.shape[0]
    buf0 = jnp.zeros((n_b, B, HALO, D), F32)
    buf_s = jnp.pad(state_pool, ((0, 0), (0, 0), (1, 0), (0, 0)))
    gp = _Group(x_prompt.reshape(B * L, D), B, L, 0, None, buf0, mk_bf, mv_bf, True, p)
    gs = _Group(x_sample.reshape(Bs * Ls, D), Bs, Ls, PAST_LEN, state_hgrn, buf_s,
                flat_view(cache_mem_k), flat_view(cache_mem_v), False, p)
    _trunks(gp, gs, p)
    return (gp.x.reshape(B, L, D), gs.x.reshape(Bs, Ls, D), jnp.stack(gp.S_out), jnp.stack(gp.buf_out),
            cache_mem_k_prompt, cache_mem_v_prompt, jnp.stack(gs.S_out), jnp.stack(gs.buf_out))
```

```python
import functools

import numpy as np

import jax
import jax.numpy as jnp
from jax import lax
from jax.experimental import pallas as pl
from jax.experimental.pallas import tpu as pltpu

F32 = jnp.float32
BF16 = jnp.bfloat16
EPS = 1e-6
PAST_LEN = 16384
POOL_WINDOWS = (2, 4, 8, 16)
LANES = 128
SUBL = 8
HEAD_DIM = 128
GLA_BLOCK = 128
GLA_SUB = 16
HALO = 16
MM_RES_SPLIT = 4
SIDE_EVERY = 4
PROJ_SIDE_PIECES = 2

V7X_VMEM_BYTES = 64 * 1024 * 1024
VMEM_CAP = V7X_VMEM_BYTES - 6 * 1024 * 1024


def _cparams(sem, vmem_bytes):
    return pltpu.CompilerParams(dimension_semantics=sem,
                                vmem_limit_bytes=int(min(max(vmem_bytes, 16 * 1024 * 1024), VMEM_CAP)))


def _nbytes(shape, dtype):
    n = 1
    for s in shape:
        n *= s
    return n * jnp.dtype(dtype).itemsize


def _rms(v, w):
    ms = jnp.mean(v * v, axis=-1, keepdims=True)
    return v * lax.rsqrt(ms + EPS) * w


def _sigmoid(x):
    return 1.0 / (1.0 + jnp.exp(-x))


def _dot(a, b):
    return jnp.dot(a, b, preferred_element_type=F32)


def _dot_nt(a, b):
    return lax.dot_general(a, b, (((1,), (1,)), ((), ())), preferred_element_type=F32)


def _dot_tn(a, b):
    return lax.dot_general(a, b, (((0,), (0,)), ((), ())), preferred_element_type=F32)


def _rmsnorm_kernel(x_ref, w_ref, o_ref):
    o_ref[...] = _rms(x_ref[...], w_ref[...]).astype(o_ref.dtype)


def _row(layer):
    return lambda width: pl.BlockSpec((None, 1, width), lambda *_: (layer, 0, 0))


def _rmsnorm(x, w, layer, tm=512):
    M, D = x.shape
    tm = min(tm, M)
    return pl.pallas_call(
        _rmsnorm_kernel,
        grid=(M // tm,),
        in_specs=[pl.BlockSpec((tm, D), lambda i: (i, 0)), _row(layer)(D)],
        out_specs=pl.BlockSpec((tm, D), lambda i: (i, 0)),
        out_shape=jax.ShapeDtypeStruct((M, D), BF16),
        compiler_params=_cparams(("parallel",), 4 * _nbytes((tm, D), F32)),
        name="rmsnorm",
    )(x, w)


def _mm_kernel(a_ref, w_ref, *o_refs, hand_off):
    w = w_ref[...].astype(BF16)
    if hand_off:
        o_refs[-1][...] = w
        o_refs = o_refs[:-1]
    acc = _dot(a_ref[...], w)
    for o_ref in o_refs:
        o_ref[...] = acc.astype(o_ref.dtype)


def _matmul(a, w, layer, out_dtypes, tm=1024, tn=1024):
    M, K = a.shape
    N = w.shape[2]
    tm, tn = min(tm, M), min(tn, N)
    hand_off = w.dtype != BF16
    assert not hand_off or M == tm
    vm = 2 * (_nbytes((tm, K), BF16) + _nbytes((K, tn), w.dtype)) + _nbytes((tm, tn), F32)
    vm += sum(2 * _nbytes((tm, tn), d) for d in out_dtypes)
    out_specs = [pl.BlockSpec((tm, tn), lambda i, j: (i, j)) for _ in out_dtypes]
    out_shape = [jax.ShapeDtypeStruct((M, N), d) for d in out_dtypes]
    if hand_off:
        out_specs.append(pl.BlockSpec((None, K, tn), lambda i, j: (0, 0, j)))
        out_shape.append(jax.ShapeDtypeStruct((1, K, N), BF16))
        vm += 3 * _nbytes((K, tn), BF16)
    outs = pl.pallas_call(
        functools.partial(_mm_kernel, hand_off=hand_off),
        grid=(M // tm, N // tn),
        in_specs=[pl.BlockSpec((tm, K), lambda i, j: (i, 0)),
                  pl.BlockSpec((None, K, tn), lambda i, j: (layer, 0, j))],
        out_specs=out_specs,
        out_shape=out_shape,
        compiler_params=_cparams(("parallel", "parallel"), vm),
        name="matmul",
    )(a, w)
    return outs


def _proj_a_kernel(*refs, layer, n_cast, decode_heads, prenorm):
    h_ref, wq_ref, wf_ref, wi_ref, wg_ref, lbl_ref = refs[:6]
    n_pre = 6 + (1 if prenorm else 0)
    n_in = n_pre + (0 if decode_heads is None else 3)
    q_ref, k_ref, lf_ref, v_ref, g_ref = refs[n_in:n_in + 5]
    w_out_refs = refs[n_in + 5:n_in + 5 + n_cast]
    side = iter(()) if decode_heads is None else _attn_decode_pieces(*refs[n_pre:n_in], refs[-1],
                                                                     heads=decode_heads)

    def side_steps():
        for _ in range(PROJ_SIDE_PIECES):
            next(side, None)

    h = _rms(h_ref[...], refs[6][...]).astype(BF16) if prenorm else h_ref[...]
    ws = [r[...].astype(BF16) for r in (wq_ref, wf_ref, wi_ref, wg_ref)]
    for r, w in zip(w_out_refs, ws):
        r[...] = w
    aq = _dot(h, ws[0])
    q_ref[...] = (aq * _sigmoid(aq)).astype(q_ref.dtype)
    side_steps()
    lg = lbl_ref[...]
    e = jnp.exp(lg - jnp.max(lg, axis=0, keepdims=True))
    lb = jnp.sum(e[:layer + 1], axis=0, keepdims=True) / jnp.sum(e, axis=0, keepdims=True)
    f = lb + (1.0 - lb) * _sigmoid(_dot(h, ws[1]))
    k_ref[...] = (1.0 - f).astype(k_ref.dtype)
    lf_ref[...] = jnp.log(f)
    side_steps()
    v_ref[...] = _dot(h, ws[2]).astype(v_ref.dtype)
    side_steps()
    ag = _dot(h, ws[3])
    g_ref[...] = (ag * _sigmoid(ag)).astype(g_ref.dtype)
    for _ in side:
        pass


def _proj_a(h, w, mixer, lb_logits, layer, act_dtype, decode=None, prenorm=None, tm=1024, tn=512):
    M, D = h.shape
    if decode is not None:
        tm = min(tm, 512)
    tm = min(tm, M)
    parts = isinstance(w, (tuple, list))
    hand_off = not parts and w.dtype != BF16
    if hand_off:
        tn = min(tn, 256)
        assert M == tm
    ni, nj = M // tm, D // tn
    nl = lb_logits.shape[0]
    if parts:
        wspec = lambda s: pl.BlockSpec((None, D, tn), lambda i, j: (0, 0, j))
        w_args, w_dtype = list(w), BF16
    else:
        wspec = lambda s: pl.BlockSpec((None, D, tn), lambda i, j: (mixer, 0, j + s * nj))
        w_args, w_dtype = [w] * 4, w.dtype
    ospec = pl.BlockSpec((tm, tn), lambda i, j: (i, j))
    vm = 2 * (_nbytes((tm, D), h.dtype) + 4 * _nbytes((D, tn), w_dtype) + 5 * _nbytes((tm, tn), F32))
    vm += 4 * _nbytes((tm, tn), F32)
    in_specs = [pl.BlockSpec((tm, D), lambda i, j: (i, 0)), wspec(0), wspec(1), wspec(2), wspec(3),
                pl.BlockSpec((nl, tn), lambda i, j: (0, j))]
    args = [h, *w_args, lb_logits]
    if prenorm is not None:
        in_specs.append(_row(prenorm[1])(D))
        args.append(prenorm[0])
        vm += 2 * _nbytes((tm, D), F32)
    out_specs = [ospec] * 5
    out_shape = [jax.ShapeDtypeStruct((M, D), act_dtype), jax.ShapeDtypeStruct((M, D), act_dtype),
                 jax.ShapeDtypeStruct((M, D), F32), jax.ShapeDtypeStruct((M, D), act_dtype),
                 jax.ShapeDtypeStruct((M, D), act_dtype)]
    if hand_off:
        out_specs += [pl.BlockSpec((None, D, tn), lambda i, j: (0, 0, j))] * 4
        out_shape += [jax.ShapeDtypeStruct((1, D, D), BF16)] * 4
        vm += 12 * _nbytes((D, tn), BF16)
    heads = None
    if decode is not None:
        dq, ck, cv, dlayer, Bd, Ld, heads, d_dtype = decode
        assert Ld == SUBL and SUBL % heads == 0 and Bd % (ni * nj) == 0
        bb = Bd // (ni * nj)
        kv_shape = (bb,) + ck.shape[2:]
        kv_spec = lambda: pl.BlockSpec((None,) + kv_shape, lambda i, j: (dlayer, i * nj + j, 0, 0, 0, 0))
        dblk = lambda: pl.BlockSpec((bb, Ld, D), lambda i, j: (i * nj + j, 0, 0))
        in_specs += [dblk(), kv_spec(), kv_spec()]
        args += [dq.reshape(Bd, Ld, D), ck, cv]
        out_specs.append(dblk())
        out_shape.append(jax.ShapeDtypeStruct((Bd, Ld, D), d_dtype))
        vm += 4 * _nbytes((bb, Ld, D), F32) + 4 * _nbytes(kv_shape, ck.dtype)
        vm += 4 * _nbytes(kv_shape[1:], BF16) + 6 * _nbytes((SUBL * Ld, ck.shape[2] * SUBL), F32)
    outs = pl.pallas_call(
        functools.partial(_proj_a_kernel, layer=layer, n_cast=4 if hand_off else 0, decode_heads=heads,
                          prenorm=prenorm is not None),
        grid=(ni, nj),
        in_specs=in_specs,
        out_specs=out_specs,
        out_shape=out_shape,
        compiler_params=_cparams(("parallel", "parallel"), vm),
        name="hgrn2_proj",
    )(*args)
    cast = tuple(outs[5:9]) if hand_off else None
    a_dec = outs[-1].reshape(Bd * Ld, D) if decode is not None else None
    return tuple(outs[:5]) + (cast, a_dec)


def _mm_res_kernel(a_ref, w_ref, x_ref, pw_ref, nw_ref, ox_ref, oh_ref, *w_out):
    tm = a_ref.shape[0]
    nsplit = MM_RES_SPLIT if tm % (8 * MM_RES_SPLIT) == 0 and tm // MM_RES_SPLIT >= 128 else 1
    if w_out:
        wb_ref = w_out[0]

        @pl.when(pl.program_id(0) == 0)
        def _():
            wb_ref[...] = w_ref[...].astype(BF16)
    else:
        wb_ref = w_ref
    for p in range(nsplit):
        rows = slice(p * tm // nsplit, (p + 1) * tm // nsplit)
        m = _dot(a_ref[rows, :].astype(BF16), wb_ref[...])
        y = x_ref[rows, :] + _rms(m, pw_ref[...])
        ox_ref[rows, :] = y
        oh_ref[rows, :] = _rms(y, nw_ref[...]).astype(oh_ref.dtype)


def _mm_res(a, w, wl, x, post_w, next_w, layer, tm=512):
    M, K = a.shape
    D = w.shape[2]
    hand_off = w.dtype != BF16
    if hand_off:
        tm = min(tm, 256)
    tm = min(tm, M)
    vm = 2 * (_nbytes((tm, K), a.dtype) + 2 * _nbytes((tm, D), F32) + _nbytes((tm, D), BF16))
    vm += 2 * _nbytes((tm, D), F32)
    w_spec = pl.BlockSpec((None, K, D), lambda i: (wl, 0, 0))
    out_specs = [pl.BlockSpec((tm, D), lambda i: (i, 0)), pl.BlockSpec((tm, D), lambda i: (i, 0))]
    out_shape = [jax.ShapeDtypeStruct((M, D), F32), jax.ShapeDtypeStruct((M, D), BF16)]
    if hand_off:
        w_spec = pl.BlockSpec((None, K, D), lambda i: (wl, 0, 0), pipeline_mode=pl.Buffered(1))
        out_specs.append(pl.BlockSpec((None, K, D), lambda i: (0, 0, 0)))
        out_shape.append(jax.ShapeDtypeStruct((1, K, D), BF16))
        vm += _nbytes((K, D), F32) + 3 * _nbytes((K, D), BF16)
    else:
        vm += 2 * _nbytes((K, D), BF16)
    outs = pl.pallas_call(
        _mm_res_kernel,
        grid=(M // tm,),
        in_specs=[pl.BlockSpec((tm, K), lambda i: (i, 0)), w_spec,
                  pl.BlockSpec((tm, D), lambda i: (i, 0)), _row(layer)(D), _row(layer)(D)],
        out_specs=out_specs,
        out_shape=out_shape,
        compiler_params=_cparams(("arbitrary" if hand_off else "parallel",), vm),
        name="matmul_norm_residual",
    )(a, w, x, post_w, next_w)
    return outs[0], outs[1], (outs[2] if hand_off else None)


def _mlp_kernel(h_ref, wu_ref, wd_ref, x_ref, pw_ref, nw_ref, ox_ref, *rest, emit_next, hand_off):
    acc_ref = rest[-1]
    f_axis = 0 if hand_off else 1
    f = pl.program_id(f_axis)
    i = pl.program_id(1 - f_axis)
    ai = i if hand_off else 0

    @pl.when(f == 0)
    def _():
        acc_ref[ai] = jnp.zeros(acc_ref.shape[1:], F32)

    if hand_off:
        wu_ref_b, wd_ref_b = rest[-3], rest[-2]

        @pl.when(i == 0)
        def _():
            wu_ref_b[...] = wu_ref[...].astype(BF16)
            wd_ref_b[...] = wd_ref[...].astype(BF16)
    else:
        wu_ref_b, wd_ref_b = wu_ref, wd_ref
    tm = ox_ref.shape[0]
    h = h_ref[pl.ds(pl.multiple_of(i * tm, tm), tm), :] if hand_off else h_ref[...]
    u = jnp.maximum(_dot(h, wu_ref_b[...]), 0.0)
    acc_ref[ai] += _dot((u * u).astype(BF16), wd_ref_b[...])

    @pl.when(f == pl.num_programs(f_axis) - 1)
    def _():
        y = x_ref[...] + _rms(acc_ref[ai], pw_ref[...])
        ox_ref[...] = y
        if emit_next:
            rest[0][...] = _rms(y, nw_ref[...]).astype(BF16)


def _mlp(h, w_up, w_down, wl, x, post_w, next_w, layer, tm=512, tf=1024):
    M, D = h.shape
    FF = w_up.shape[2]
    tm = min(tm, M)
    hand_off = w_up.dtype != BF16
    if hand_off:
        tf = min(tf, 512)
    emit_next = layer + 1 < next_w.shape[0]
    nt, nf = M // tm, FF // tf
    if hand_off:
        ix = lambda fn: (lambda f, i: fn(i, f))
        last = lambda i, f: (jnp.where(f == nf - 1, i, 0), 0)
        grid, sem, n_acc = (nf, nt), ("arbitrary", "arbitrary"), nt
    else:
        ix = lambda fn: fn
        last = lambda i, f: (i, 0)
        grid, sem, n_acc = (nt, nf), ("parallel", "arbitrary"), 1
    tile = lambda: pl.BlockSpec((tm, D), ix(last))
    out_specs = [tile()] + ([tile()] if emit_next else [])
    out_shape = [jax.ShapeDtypeStruct((M, D), F32)] + ([jax.ShapeDtypeStruct((M, D), BF16)] if emit_next else [])
    h_rows = M if hand_off else tm
    vm = 2 * (_nbytes((h_rows, D), BF16) + _nbytes((tm, D), BF16) + 2 * _nbytes((D, tf), w_up.dtype)
              + 2 * _nbytes((tm, D), F32))
    vm += (1 + n_acc) * _nbytes((tm, D), F32) + 2 * _nbytes((tm, tf), F32)
    if hand_off:
        out_specs += [pl.BlockSpec((None, D, tf), ix(lambda i, f: (0, 0, f))),
                      pl.BlockSpec((None, tf, D), ix(lambda i, f: (0, f, 0)))]
        out_shape += [jax.ShapeDtypeStruct((1, D, FF), BF16), jax.ShapeDtypeStruct((1, FF, D), BF16)]
        vm += 4 * _nbytes((D, tf), BF16)
    outs = pl.pallas_call(
        functools.partial(_mlp_kernel, emit_next=emit_next, hand_off=hand_off),
        grid=grid,
        in_specs=[pl.BlockSpec((M, D), lambda f, i: (0, 0)) if hand_off
                  else pl.BlockSpec((tm, D), lambda i, f: (i, 0)),
                  pl.BlockSpec((None, D, tf), ix(lambda i, f: (wl, 0, f))),
                  pl.BlockSpec((None, tf, D), ix(lambda i, f: (wl, f, 0))),
                  tile(), _row(layer)(D), _row(layer + 1 if emit_next else layer)(D)],
        out_specs=out_specs,
        out_shape=out_shape,
        scratch_shapes=[pltpu.VMEM((n_acc, tm, D), F32)],
        compiler_params=_cparams(sem, vm),
        name="relu2_mlp",
    )(h, w_up, w_down, x, post_w, next_w)
    n = 2 if emit_next else 1
    return outs[0], (outs[1] if emit_next else None), (tuple(outs[n:]) if hand_off else None)


def _split3(x):
    hi = x.astype(BF16)
    r1 = x - hi.astype(F32)
    mid = r1.astype(BF16)
    lo = (r1 - mid.astype(F32)).astype(BF16)
    return hi, mid, lo


def _decay_rows(d, rows):
    hi, mid, lo = _split3(d)
    r = lax.broadcasted_iota(jnp.int32, (rows, d.shape[-1]), 0)
    return jnp.where(r == 0, hi.astype(F32), jnp.where(r == 1, mid.astype(F32),
                                                       jnp.where(r == 2, lo.astype(F32), 0.0)))


def _gla_prompt_kernel(*refs, decode_heads=None):
    if decode_heads is None:
        (q_ref, k_ref, lf_ref, v_ref, g_ref, nw_ref, sel_ref, lvl_ref, o_ref, sout_ref,
         s_scr, qs_scr, ks_scr, ke_scr, fac_scr, a_scr) = refs
    else:
        (q_ref, k_ref, lf_ref, v_ref, g_ref, nw_ref, sel_ref, lvl_ref, dq_ref, dk_ref, dv_ref,
         o_ref, sout_ref, do_ref, s_scr, qs_scr, ks_scr, ke_scr, fac_scr, a_scr) = refs
    T = q_ref.shape[1]
    D = q_ref.shape[2]
    H = D // HEAD_DIM
    SUB = GLA_SUB
    nb = T // SUB
    c_idx = pl.program_id(1)

    @pl.when(c_idx == 0)
    def _():
        s_scr[...] = jnp.zeros_like(s_scr)

    side = iter(()) if decode_heads is None else _attn_decode_pieces(dq_ref, dk_ref, dv_ref, do_ref,
                                                                     heads=decode_heads)
    stage = [0]

    def side_step():
        stage[0] += 1
        if stage[0] % SIDE_EVERY == 0:
            next(side, None)

    chunks = []
    c = 2 * SUB
    while c < T:
        chunks.append(c)
        c *= 2
    nf = 2 * len(chunks) + 3
    sums = _dot(sel_ref[...], jnp.concatenate(_split3(lf_ref[0]), axis=0))
    fac_scr[...] = jnp.exp(sums[T:])
    nq = len(chunks) + 1
    for i in range(nb):
        rows = slice(i * SUB, (i + 1) * SUB)
        r = sums[rows]
        q16 = q_ref[0, rows, :].astype(F32) * jnp.exp(r)
        kd = k_ref[0, rows, :].astype(F32) * jnp.exp(-r)
        f_row = lambda n: fac_scr[n * nb + i:n * nb + i + 1, :]
        qs_scr[0, rows, :] = (q16 * f_row(0)).astype(BF16)
        qs_scr[1, rows, :] = q16.astype(BF16)
        ks_scr[0, rows, :] = kd.astype(BF16)
        ks_scr[1, rows, :] = (kd * f_row(nq)).astype(BF16)
        for n in range(len(chunks)):
            qs_scr[n + 2, rows, :] = (q16 * f_row(n + 1)).astype(BF16)
            ks_scr[n + 2, rows, :] = (kd * f_row(nq + n + 1)).astype(BF16)
        ke_scr[rows, :] = (kd * f_row(nf - 1)).astype(BF16)
        side_step()
    ke_scr[T:, :] = _decay_rows(fac_scr[(nf - 1) * nb:(nf - 1) * nb + 1, :], SUB).astype(BF16)

    rr = lax.broadcasted_iota(jnp.int32, (SUB, 2 * HEAD_DIM), 0)
    cc = lax.broadcasted_iota(jnp.int32, (SUB, 2 * HEAD_DIM), 1)
    pick = jnp.where((rr < 3) & (cc >= HEAD_DIM), 1.0, 0.0).astype(BF16)
    zeros = jnp.zeros((T, HEAD_DIM), BF16)
    nw = nw_ref[...]

    lvl = lvl_ref[...]
    for h in range(H):
        hs = slice(h * HEAD_DIM, (h + 1) * HEAD_DIM)
        r = _dot_nt(qs_scr[1, :, hs], ks_scr[0:2, :, hs].reshape(2 * T, HEAD_DIM))
        a = jnp.where(lvl == 1, r[:, :T], jnp.where(lvl == 2, r[:, T:], 0.0))
        for n in range(len(chunks)):
            a = jnp.where(lvl == n + 3, _dot_nt(qs_scr[n + 2, :, hs], ks_scr[n + 2, :, hs]), a)
        a_scr[h] = a.astype(BF16)
        side_step()
    for h in range(H):
        hs = slice(h * HEAD_DIM, (h + 1) * HEAD_DIM)
        vh = v_ref[0, :, hs]
        s_old = s_scr[h]
        o = _dot(jnp.concatenate([a_scr[h], qs_scr[0, :, hs]], axis=1),
                 jnp.concatenate([vh, s_old.astype(BF16)], axis=0))
        rhs = jnp.concatenate([jnp.concatenate([vh, zeros], axis=1), pick], axis=0)
        upd = _dot_tn(ke_scr[:, hs], rhs)
        s_scr[h] = s_old * upd[:, HEAD_DIM:] + upd[:, :HEAD_DIM]
        o_ref[0, :, hs] = (_rms(o, nw) * g_ref[0, :, hs].astype(F32)).astype(o_ref.dtype)
        side_step()
    for _ in side:
        pass

    @pl.when(c_idx == pl.num_programs(1) - 1)
    def _():
        sout_ref[0] = s_scr[...]


def _gla_range_matrix(T, SUB):
    chunks = []
    c = 2 * SUB
    while c < T:
        chunks.append(c)
        c *= 2
    nb = T // SUB
    t = np.arange(T)
    rows = [((t[None, :] <= t[:, None]) & (t[None, :] // SUB == t[:, None] // SUB))]
    start = np.arange(nb)[:, None] * SUB
    for C in [T] + chunks:
        rows.append((t[None, :] >= (start // C) * C) & (t[None, :] < start))
    for C in [SUB] + chunks + [T]:
        rows.append((t[None, :] >= start) & (t[None, :] < (start // C + 1) * C))
    sel = np.concatenate(rows, axis=0).astype(np.float32)
    row, col = t[:, None], t[None, :]
    lvl = np.where((row // SUB == col // SUB) & (col <= row), 1, 0)
    for j, C in enumerate([SUB] + chunks):
        lvl = np.where(((row // C) % 2 == 1) & (col // C == row // C - 1), 2 + j, lvl)
    return sel, lvl.astype(np.int32), len(chunks)


def _gla_prompt(q, k, lf, v, g, norm_w, mixer, B, L, decode=None):
    D = q.shape[-1]
    H = D // HEAD_DIM
    T = GLA_BLOCK
    assert T == HEAD_DIM
    nc = L // T
    sel, lvl, nchunks = _gla_range_matrix(T, GLA_SUB)
    sel3 = jnp.asarray(np.concatenate([sel] * 3, axis=1), BF16)
    nfac = sel.shape[0] - T
    r3 = lambda a: a.reshape(B, L, D)
    blk = lambda: pl.BlockSpec((1, T, D), lambda b, c: (b, c, 0))
    const = lambda a: pl.BlockSpec(a.shape, lambda b, c: (0, 0))
    vm = 2 * (5 * _nbytes((T, D), BF16) + _nbytes((T, D), F32)) + 3 * _nbytes((H, HEAD_DIM, HEAD_DIM), F32)
    vm += (2 * nchunks + 6) * _nbytes((T, D), BF16) + 8 * _nbytes((T, D), F32)
    in_specs = [blk(), blk(), blk(), blk(), blk(), _row(mixer)(HEAD_DIM), const(sel3), const(lvl)]
    args = [r3(q), r3(k), r3(lf), r3(v), r3(g), norm_w, sel3, jnp.asarray(lvl)]
    out_specs = [blk(), pl.BlockSpec((1, H, HEAD_DIM, HEAD_DIM), lambda b, c: (b, 0, 0, 0))]
    out_shape = [jax.ShapeDtypeStruct((B, L, D), BF16), jax.ShapeDtypeStruct((B, H, HEAD_DIM, HEAD_DIM), F32)]
    heads = None
    if decode is not None:
        dq, ck, cv, layer, Bd, Ld, heads, d_dtype = decode
        assert Ld == SUBL and SUBL % heads == 0 and Bd % (B * nc) == 0
        bb = Bd // (B * nc)
        kv_shape = (bb,) + ck.shape[2:]
        kv_spec = lambda: pl.BlockSpec((None,) + kv_shape, lambda b, c: (layer, b * nc + c, 0, 0, 0, 0))
        dblk = lambda: pl.BlockSpec((bb, Ld, D), lambda b, c: (b * nc + c, 0, 0))
        in_specs += [dblk(), kv_spec(), kv_spec()]
        args += [dq.reshape(Bd, Ld, D), ck, cv]
        out_specs.append(dblk())
        out_shape.append(jax.ShapeDtypeStruct((Bd, Ld, D), d_dtype))
        vm += 4 * _nbytes((bb, Ld, D), F32) + 4 * _nbytes(kv_shape, ck.dtype)
        vm += 4 * _nbytes(kv_shape[1:], BF16) + 6 * _nbytes((SUBL * Ld, ck.shape[2] * SUBL), F32)
    outs = pl.pallas_call(
        functools.partial(_gla_prompt_kernel, decode_heads=heads),
        grid=(B, nc),
        in_specs=in_specs,
        out_specs=out_specs,
        out_shape=out_shape,
        scratch_shapes=[pltpu.VMEM((H, HEAD_DIM, HEAD_DIM), F32),
                        pltpu.VMEM((nchunks + 2, T, D), BF16),
                        pltpu.VMEM((nchunks + 2, T, D), BF16),
                        pltpu.VMEM((T + GLA_SUB, D), BF16),
                        pltpu.VMEM((nfac, D), F32),
                        pltpu.VMEM((H, T, T), BF16)],
        compiler_params=_cparams(("parallel", "arbitrary"), vm),
        name="hgrn2_recurrence_prompt",
    )(*args)
    a_dec = outs[2].reshape(Bd * Ld, D) if decode is not None else None
    return outs[0].reshape(B * L, D), outs[1], a_dec


def _gla_step_kernel(q_ref, k_ref, lf_ref, v_ref, g_ref, s0_ref, nw_ref, o_ref, sout_ref,
                     qs_scr, ks_scr, ke_scr, ds_scr):
    bb, T, D = q_ref.shape
    H = D // HEAD_DIM
    R = bb * T
    pos = lax.broadcasted_iota(jnp.int32, (bb, T, D), 1)
    b = lf_ref[...]
    sh = 1
    while sh < T:
        b = b + jnp.where(pos >= sh, pltpu.roll(b, sh, 1), 0.0)
        sh *= 2
    b_last = b[:, T - 1:T, :]
    k = k_ref[...]
    qs_scr[...] = q_ref[...] * jnp.exp(b)
    ks_scr[...] = k * jnp.exp(-b)
    ke_scr[...] = k * jnp.exp(b_last - b)
    d_hi, d_mid, d_lo = (t.astype(F32) for t in _split3(jnp.exp(b_last)))
    ds_scr[...] = jnp.where(pos == 0, d_hi, jnp.where(pos == 1, d_mid, jnp.where(pos == 2, d_lo, 0.0)))

    row = lax.broadcasted_iota(jnp.int32, (R, R), 0)
    col = lax.broadcasted_iota(jnp.int32, (R, R), 1)
    same_seq_causal = (row // T == col // T) & (col <= row)
    rr = lax.broadcasted_iota(jnp.int32, (T, 2 * HEAD_DIM), 0)
    cc = lax.broadcasted_iota(jnp.int32, (T, 2 * HEAD_DIM), 1)
    pick = jnp.where((rr < 3) & (cc >= HEAD_DIM), 1.0, 0.0)
    zeros = jnp.zeros((T, HEAD_DIM), F32)
    nw = nw_ref[...]

    for h in range(H):
        hs = slice(h * HEAD_DIM, (h + 1) * HEAD_DIM)
        qd = qs_scr[:, :, hs].reshape(R, HEAD_DIM)
        kd = ks_scr[:, :, hs].reshape(R, HEAD_DIM)
        vh = v_ref[:, :, hs].reshape(R, HEAD_DIM)
        a = jnp.where(same_seq_causal, _dot_nt(qd.astype(BF16), kd.astype(BF16)), 0.0)
        o_intra = _dot(a.astype(BF16), vh.astype(BF16))
        outs = []
        for i in range(bb):
            s_old = s0_ref[i, h]
            outs.append(o_intra[i * T:(i + 1) * T]
                        + _dot(qd[i * T:(i + 1) * T].astype(BF16), s_old.astype(BF16)))
            lhs = jnp.concatenate([ke_scr[i, :, hs], ds_scr[i, :, hs]], axis=0).astype(BF16)
            rhs = jnp.concatenate([jnp.concatenate([vh[i * T:(i + 1) * T], zeros], axis=1), pick],
                                  axis=0).astype(BF16)
            upd = _dot_tn(lhs, rhs)
            sout_ref[i, h] = s_old * upd[:, HEAD_DIM:] + upd[:, :HEAD_DIM]
        o = jnp.concatenate(outs, axis=0)
        on = _rms(o, nw) * g_ref[:, :, hs].reshape(R, HEAD_DIM)
        o_ref[:, :, hs] = on.reshape(bb, T, HEAD_DIM).astype(o_ref.dtype)


def _gla_step(q, k, lf, v, g, s0, norm_w, mixer, B, L, bb=8):
    D = q.shape[-1]
    H = D // HEAD_DIM
    assert L == 8 and B % bb == 0
    r3 = lambda a: a.reshape(B, L, D)
    blk = lambda: pl.BlockSpec((bb, L, D), lambda b: (b, 0, 0))
    sblk = lambda: pl.BlockSpec((bb, H, HEAD_DIM, HEAD_DIM), lambda b: (b, 0, 0, 0))
    vm = 2 * (6 * _nbytes((bb, L, D), F32) + 2 * _nbytes((bb, H, HEAD_DIM, HEAD_DIM), F32))
    vm += 12 * _nbytes((bb, L, D), F32)
    o, s = pl.pallas_call(
        _gla_step_kernel,
        grid=(B // bb,),
        in_specs=[blk(), blk(), blk(), blk(), blk(),
                  pl.BlockSpec((None, bb, H, HEAD_DIM, HEAD_DIM), lambda b: (mixer, b, 0, 0, 0)),
                  _row(mixer)(HEAD_DIM)],
        out_specs=[blk(), sblk()],
        out_shape=[jax.ShapeDtypeStruct((B, L, D), F32),
                   jax.ShapeDtypeStruct((B, H, HEAD_DIM, HEAD_DIM), F32)],
        scratch_shapes=[pltpu.VMEM((bb, L, D), F32)] * 4,
        compiler_params=_cparams(("parallel",), vm),
        name="hgrn2_recurrence_step",
    )(r3(q), r3(k), r3(lf), r3(v), r3(g), s0, norm_w)
    return o.reshape(B * L, D), s


def _pool_kernel(*refs, pos0, has_halo):
    if has_halo:
        u_ref, halo_ref, buf_ref, pw_ref, ps_ref, o_ref, nb_ref, ext = refs
    else:
        u_ref, buf_ref, pw_ref, ps_ref, o_ref, nb_ref, ext = refs
    bb, tl, D = u_ref.shape
    G = len(POOL_WINDOWS)
    gc = D // G
    l_idx = pl.program_id(1)
    ext[:, HALO:, :] = u_ref[...]
    if has_halo:
        @pl.when(l_idx == 0)
        def _():
            ext[:, :HALO, :] = buf_ref[...]

        @pl.when(l_idx > 0)
        def _():
            ext[:, :HALO, :] = halo_ref[...]
    else:
        ext[:, :HALO, :] = buf_ref[...]

    pos = pos0 + l_idx * tl + lax.broadcasted_iota(jnp.int32, (tl, gc), 0)
    for gi, w in enumerate(POOL_WINDOWS):
        ls = slice(gi * gc, (gi + 1) * gc)
        cnt = jnp.minimum(w, pos + 1).astype(F32)
        rows = []
        for i in range(bb):
            u = ext[i, HALO:HALO + tl, ls]
            s = u
            for d in range(1, w):
                s = s + ext[i, HALO - d:HALO - d + tl, ls]
            rows.append(s / cnt - u)
        pooled = rows[0] if bb == 1 else jnp.concatenate(rows, axis=0)
        mixed = (_dot(pooled.astype(BF16), pw_ref[gi]) * ps_ref[:, ls]).astype(o_ref.dtype)
        for i in range(bb):
            o_ref[i, :, ls] = mixed[i * tl:(i + 1) * tl]

    @pl.when(l_idx == pl.num_programs(1) - 1)
    def _():
        nb_ref[...] = ext[:, tl:tl + HALO, :]


def _pool(u, buf, pos0, pool_w, pool_scale, mixer, B, L, bb, tl, out_dtype):
    D = u.shape[-1]
    G = len(POOL_WINDOWS)
    gc = D // G
    tl = min(tl, L)
    u3 = u.reshape(B, L, D)
    has_halo = L > tl
    step = tl // HALO
    in_specs = [pl.BlockSpec((bb, tl, D), lambda b, l: (b, l, 0))]
    args = [u3]
    if has_halo:
        in_specs.append(pl.BlockSpec((bb, HALO, D), lambda b, l: (b, jnp.maximum(l * step - 1, 0), 0)))
        args.append(u3)
    in_specs += [pl.BlockSpec((None, bb, HALO, D), lambda b, l: (mixer, b, 0, 0)),
                 pl.BlockSpec((None, G, gc, gc), lambda b, l: (mixer, 0, 0, 0)),
                 _row(mixer)(D)]
    args += [buf, pool_w, pool_scale]
    vm = 2 * (_nbytes((bb, tl, D), F32) + 3 * _nbytes((bb, HALO, D), F32) + _nbytes((G, gc, gc), BF16)
              + _nbytes((bb, tl, D), BF16)) + _nbytes((bb, tl + HALO, D), F32) + 4 * _nbytes((bb, tl, D), F32)
    mixed, newbuf = pl.pallas_call(
        functools.partial(_pool_kernel, pos0=pos0, has_halo=has_halo),
        grid=(B // bb, L // tl),
        in_specs=in_specs,
        out_specs=[pl.BlockSpec((bb, tl, D), lambda b, l: (b, l, 0)),
                   pl.BlockSpec((bb, HALO, D), lambda b, l: (b, 0, 0))],
        out_shape=[jax.ShapeDtypeStruct((B, L, D), out_dtype), jax.ShapeDtypeStruct((B, HALO, D), F32)],
        scratch_shapes=[pltpu.VMEM((bb, tl + HALO, D), F32)],
        compiler_params=_cparams(("parallel", "arbitrary"), vm),
        name="causal_pool",
    )(*args)
    return mixed.reshape(B * L, D), newbuf


def _attn_kernel(q_ref, k_ref, v_ref, o_ref, *, heads):
    bb, tq, D = q_ref.shape
    hd = D // heads
    scale = hd ** -0.5
    for i in range(bb):
        for h in range(heads):
            hs = slice(h * hd, (h + 1) * hd)
            s = _dot_nt(q_ref[i, :, hs].astype(BF16), k_ref[i, :, hs].astype(BF16)) * scale
            p = jnp.exp(s - jnp.max(s, axis=-1, keepdims=True))
            den = jnp.sum(p, axis=-1, keepdims=True)
            o_ref[i, :, hs] = (_dot(p.astype(BF16), v_ref[i, :, hs].astype(BF16)) / den).astype(o_ref.dtype)


def _attn(q, mk, mv, layer, B, L, heads, bb, tq, out_dtype):
    D = q.shape[-1]
    tq = min(tq, L)
    kv_shape = (bb,) + mk.shape[2:]
    kv_blk = (None,) + kv_shape
    vm = 2 * (_nbytes((bb, tq, D), q.dtype) + 2 * _nbytes(kv_shape, mk.dtype) + _nbytes((bb, tq, D), out_dtype))
    vm += 2 * _nbytes(kv_shape, BF16) + 8 * _nbytes((tq, D // heads), F32)
    o = pl.pallas_call(
        functools.partial(_attn_kernel, heads=heads),
        grid=(B // bb, L // tq),
        in_specs=[pl.BlockSpec((bb, tq, D), lambda b, l: (b, l, 0)),
                  pl.BlockSpec(kv_blk, lambda b, l: (layer, b, 0, 0)),
                  pl.BlockSpec(kv_blk, lambda b, l: (layer, b, 0, 0))],
        out_specs=pl.BlockSpec((bb, tq, D), lambda b, l: (b, l, 0)),
        out_shape=jax.ShapeDtypeStruct((B, L, D), out_dtype),
        compiler_params=_cparams(("parallel", "parallel"), vm),
        name="memory_cross_attention",
    )(q.reshape(B, L, D), mk, mv)
    return o.reshape(B * L, D)


def _attn_decode_pieces(q_ref, k_ref, v_ref, o_ref, *, heads):
    bb, tq, D = q_ref.shape
    n_mem, nu = k_ref.shape[1], k_ref.shape[2]
    hd = D // heads
    per = SUBL // heads
    W = n_mem * SUBL
    scale = hd ** -0.5
    res = lax.broadcasted_iota(jnp.int32, (tq, W), 1) % SUBL
    tile = lambda r: jnp.concatenate([r] * (W // LANES), axis=1)

    def feat(j):
        c, h = divmod(j, heads)
        return slice(h * hd + c * LANES, h * hd + (c + 1) * LANES)

    def same_row_reduce(x, op):
        r = x[:, :LANES]
        for j in range(1, W // LANES):
            r = op(r, x[:, j * LANES:(j + 1) * LANES])
        sh = SUBL
        while sh < LANES:
            r = op(r, pltpu.roll(r, sh, 1))
            sh *= 2
        return r

    for i in range(bb):
        w = None
        for u in range(nu):
            slab = k_ref[i, :, u].reshape(W, LANES).astype(BF16)
            qm = jnp.concatenate([q_ref[i, :, feat(u * SUBL + s)] for s in range(SUBL)], axis=0)
            p = _dot_nt(qm.astype(BF16), slab)
            z = p[:tq]
            for s in range(1, SUBL):
                z = jnp.where(res == s, p[s * tq:(s + 1) * tq], z)
            w = z if w is None else w + z
            yield
        sc = w
        for c in range(1, per):
            sc = sc + pltpu.roll(w, W - c * heads, 1)
        sc = sc * scale
        e = jnp.exp(sc - tile(same_row_reduce(sc, jnp.maximum)))
        pn = e / tile(same_row_reduce(e, jnp.add))
        blocks = []
        for s in range(SUBL):
            c_lo = s // heads
            src = pn if c_lo == 0 else pltpu.roll(pn, c_lo * heads, 1)
            blocks.append(jnp.where(res == s, src, 0.0))
        pm = jnp.concatenate(blocks, axis=0).astype(BF16)
        yield
        for u in range(nu):
            o = _dot(pm, v_ref[i, :, u].reshape(W, LANES).astype(BF16))
            for s in range(SUBL):
                o_ref[i, :, feat(u * SUBL + s)] = o[s * tq:(s + 1) * tq].astype(o_ref.dtype)
            yield


def _attn_decode_kernel(q_ref, k_ref, v_ref, o_ref, *, heads):
    for _ in _attn_decode_pieces(q_ref, k_ref, v_ref, o_ref, heads=heads):
        pass


def _attn_decode(q, ck, cv, layer, B, L, heads, bb, out_dtype):
    D = q.shape[-1]
    assert L == SUBL and SUBL % heads == 0 and D % (SUBL * LANES) == 0
    kv_shape = (bb,) + ck.shape[2:]
    vm = 2 * (2 * _nbytes((bb, L, D), F32) + 2 * _nbytes(kv_shape, ck.dtype))
    vm += 4 * _nbytes(kv_shape[1:], BF16) + 6 * _nbytes((SUBL * L, ck.shape[2] * SUBL), F32)
    o = pl.pallas_call(
        functools.partial(_attn_decode_kernel, heads=heads),
        grid=(B // bb,),
        in_specs=[pl.BlockSpec((bb, L, D), lambda b: (b, 0, 0)),
                  pl.BlockSpec((None,) + kv_shape, lambda b: (layer, b, 0, 0, 0, 0)),
                  pl.BlockSpec((None,) + kv_shape, lambda b: (layer, b, 0, 0, 0, 0))],
        out_specs=pl.BlockSpec((bb, L, D), lambda b: (b, 0, 0)),
        out_shape=jax.ShapeDtypeStruct((B, L, D), out_dtype),
        compiler_params=_cparams(("parallel",), vm),
        name="memory_cross_attention_decode",
    )(q.reshape(B, L, D), ck, cv)
    return o.reshape(B * L, D)


def _mem_proj_kernel(x_ref, nw_ref, w_ref, flat_ref, bf_ref, w_scr, *, heads):
    tm, D = x_ref.shape
    R = D // LANES
    nchunk = R // heads

    @pl.when(pl.program_id(1) == 0)
    def _():
        w_scr[...] = w_ref[...].astype(BF16)

    acc = _dot(_rms(x_ref[...], nw_ref[...]).astype(BF16), w_scr[...])
    bf_ref[...] = acc.astype(bf_ref.dtype)
    for h in range(heads):
        for c in range(nchunk):
            j = h * nchunk + c
            flat_ref[pl.ds(c * heads + h, tm, stride=R), :] = acc[:, j * LANES:(j + 1) * LANES]


def _mem_proj(mem, norm_w, w, half, heads, tm=256):
    M, D = mem.shape
    depth = w.shape[0]
    R = D // LANES
    tm = min(tm, M)
    nt = M // tm
    vm = 2 * (_nbytes((tm, D), F32) + _nbytes((D, D), F32) + _nbytes((tm, D), F32) + _nbytes((tm, D), BF16))
    vm += _nbytes((D, D), BF16) + 2 * _nbytes((tm, D), F32)
    return pl.pallas_call(
        functools.partial(_mem_proj_kernel, heads=heads),
        grid=(depth, nt),
        in_specs=[pl.BlockSpec((tm, D), lambda l, i: (i, 0)),
                  pl.BlockSpec((None, 1, D), lambda l, i: (l, 0, 0)),
                  pl.BlockSpec((None, D, D), lambda l, i: (l, 0, half))],
        out_specs=[pl.BlockSpec((tm * R, LANES), lambda l, i: (l * nt + i, 0)),
                   pl.BlockSpec((None, tm, D), lambda l, i: (l, i, 0))],
        out_shape=[jax.ShapeDtypeStruct((depth * M * R, LANES), F32),
                   jax.ShapeDtypeStruct((depth, M, D), BF16)],
        scratch_shapes=[pltpu.VMEM((D, D), BF16)],
        compiler_params=_cparams(("parallel", "arbitrary"), vm),
        name="memory_kv_projection",
    )(mem, norm_w, w)


class _Group:
    def __init__(self, x, B, L, pos0, S_in, buf_in, mk, mv, prompt, p):
        self.B, self.L, self.pos0, self.S_in, self.buf_in, self.mk, self.mv = B, L, pos0, S_in, buf_in, mk, mv
        self.prompt = prompt
        self.act = BF16 if prompt else F32
        self.x = x
        self.hn = None if prompt else _rmsnorm(x, p["norm_mix_pre"], 0)
        self.S_out, self.buf_out = [], []


class _NoRider:
    def job(self):
        return None

    def served(self, a):
        pass


def _mixer(g, l, p, rider=_NoRider()):
    j = l // 2
    if l % 2 == 0:
        w, wl = _weight(p, "w_in_a", j)
        h_in, prenorm = (g.x, (p["norm_mix_pre"], l)) if g.hn is None else (g.hn, None)
        q, k, lf, v, gate, cast, a = _proj_a(h_in, w, wl, p["hg_lb_logits"], l, g.act,
                                             rider.job() if g.prompt else None, prenorm)
        _hand_off(p, "w_in_a", j, cast)
        rider.served(a)
        if g.prompt:
            o, S, a = _gla_prompt(q, k, lf, v, gate, p["hg_norm"], j, g.B, g.L, rider.job())
            rider.served(a)
        else:
            o, S = _gla_step(q, k, lf, v, gate, g.S_in, p["hg_norm"], j, g.B, g.L)
        g.S_out.append(S)
        _residual(g, p, o, "w_out_a", j, "norm_mix_post", "norm_x_pre", l)
    else:
        u = _project(g, p, "w_in_b", j, F32)
        bb, tl = (1, 512) if g.prompt else (16, g.L)
        mixed, nb = _pool(u, g.buf_in, g.pos0, p["pool_w"], p["pool_scale"], j, g.B, g.L, bb, tl, g.act)
        g.buf_out.append(nb[:, 1:, :])
        _residual(g, p, mixed, "w_out_b", j, "norm_mix_post", "norm_x_pre", l)


def _weight(p, name, idx):
    cast = p["cast"].get((name, idx))
    return (cast, 0) if cast is not None else (p[name], idx)


def _hand_off(p, name, idx, cast):
    if cast is not None:
        p["cast"][(name, idx)] = cast


def _project(g, p, name, idx, dtype):
    w, wl = _weight(p, name, idx)
    outs = _matmul(g.hn, w, wl, [dtype])
    _hand_off(p, name, idx, outs[1] if len(outs) > 1 else None)
    return outs[0]


def _residual(g, p, a, name, idx, post, nxt, l):
    w, wl = _weight(p, name, idx)
    g.x, g.hn, cast = _mm_res(a, w, wl, g.x, p[post], p[nxt], l)
    _hand_off(p, name, idx, cast)


def _after_attention(g, l, a, p):
    _residual(g, p, a, "w_xo", l, "norm_x_post", "norm_mlp_pre", l)
    (wu, wl), (wd, _) = _weight(p, "w_up", l), _weight(p, "w_down", l)
    g.x, g.hn, cast = _mlp(g.hn, wu, wd, wl, g.x, p["norm_mlp_post"], p["norm_mix_pre"], l)
    if cast is not None:
        _hand_off(p, "w_up", l, cast[0])
        _hand_off(p, "w_down", l, cast[1])


class _SampleWalk:
    def __init__(self, g, p):
        self.g, self.p = g, p
        self._walk = self._layers()
        self._pending = next(self._walk, None)

    def _layers(self):
        g, p = self.g, self.p
        for l in range(p["w_xq"].shape[0]):
            _mixer(g, l, p)
            q = _project(g, p, "w_xq", l, g.act)
            a = yield (q, g.mk, g.mv, l, g.B, g.L, p["mem_heads"], g.act)
            _after_attention(g, l, a, p)

    def job(self):
        return self._pending

    def served(self, a):
        if a is not None:
            try:
                self._pending = self._walk.send(a)
            except StopIteration:
                self._pending = None

    def serve_standalone(self):
        q, mk, mv, l, B, L, heads, dtype = self._pending
        self.served(_attn_decode(q, mk, mv, l, B, L, heads, 4, dtype))


def _trunks(gp, gs, p):
    heads = p["mem_heads"]
    rider = _SampleWalk(gs, p)
    for l in range(p["w_xq"].shape[0]):
        if l % 2 == 1 and rider.job() is not None:
            rider.serve_standalone()
        _mixer(gp, l, p, rider)
        qp = _project(gp, p, "w_xq", l, gp.act)
        _after_attention(gp, l, _attn(qp, gp.mk, gp.mv, l, gp.B, gp.L, heads, 1, 512, gp.act), p)
    while rider.job() is not None:
        rider.serve_standalone()


def kernel(x_prompt, x_sample, state_hgrn, state_pool, cache_mem_k, cache_mem_v, mem_prompt, w_in_a, hg_lb_logits, hg_norm, w_out_a, w_in_b, pool_w, pool_scale, w_out_b, norm_mem, w_xq, w_xkv, w_xo, norm_mix_pre, norm_mix_post, norm_x_pre, norm_x_post, norm_mlp_pre, norm_mlp_post, w_up, w_down):
    B, L, D = x_prompt.shape
    Bs, Ls, _ = x_sample.shape
    depth = w_xq.shape[0]
    n_mem, heads = cache_mem_k.shape[2], cache_mem_k.shape[3]
    assert L % GLA_BLOCK == 0 and Ls <= GLA_SUB and D % (HEAD_DIM * len(POOL_WINDOWS)) == 0
    assert state_pool.shape[2] == HALO - 1
    bf = lambda a: a.astype(BF16)
    rows = lambda a: a.reshape(a.shape[0], 1, a.shape[1])
    p = dict(w_in_a=w_in_a, hg_lb_logits=hg_lb_logits, hg_norm=rows(hg_norm), w_out_a=w_out_a,
             w_in_b=w_in_b, pool_w=bf(pool_w), pool_scale=rows(pool_scale), w_out_b=w_out_b,
             w_xq=w_xq, w_xo=w_xo, norm_mix_pre=rows(norm_mix_pre), norm_mix_post=rows(norm_mix_post),
             norm_x_pre=rows(norm_x_pre), norm_x_post=rows(norm_x_post), norm_mlp_pre=rows(norm_mlp_pre),
             norm_mlp_post=rows(norm_mlp_post), w_up=w_up, w_down=w_down, mem_heads=heads, cast={})
    R = D // LANES
    nchunk = R // heads

    def cache_view(flat, nb):
        c = flat.reshape(depth, nb, n_mem, nchunk, heads, LANES)
        return jnp.swapaxes(c, 3, 4).reshape(depth, nb, n_mem, heads, D // heads)

    def flat_view(cache):
        nb = cache.shape[1]
        c = cache.reshape(depth, nb, n_mem, heads, nchunk, LANES)
        return jnp.swapaxes(c, 3, 4).reshape(depth, nb, n_mem, R // SUBL, SUBL, LANES)

    mem2d = mem_prompt.reshape(B * n_mem, D)
    k_flat, k_bf = _mem_proj(mem2d, rows(norm_mem), w_xkv, 0, heads)
    v_flat, v_bf = _mem_proj(mem2d, rows(norm_mem), w_xkv, 1, heads)
    cache_mem_k_prompt = cache_view(k_flat, B)
    cache_mem_v_prompt = cache_view(v_flat, B)
    mk_bf = k_bf.reshape(depth, B, n_mem, D)
    mv_bf = v_bf.reshape(depth, B, n_mem, D)

    n_b = state_pool.shape[0]
    buf0 = jnp.zeros((n_b, B, HALO, D), F32)
    buf_s = jnp.pad(state_pool, ((0, 0), (0, 0), (1, 0), (0, 0)))
    gp = _Group(x_prompt.reshape(B * L, D), B, L, 0, None, buf0, mk_bf, mv_bf, True, p)
    gs = _Group(x_sample.reshape(Bs * Ls, D), Bs, Ls, PAST_LEN, state_hgrn, buf_s,
                flat_view(cache_mem_k), flat_view(cache_mem_v), False, p)
    _trunks(gp, gs, p)
    return (gp.x.reshape(B, L, D), gs.x.reshape(Bs, Ls, D), jnp.stack(gp.S_out), jnp.stack(gp.buf_out),
            cache_mem_k_prompt, cache_mem_v_prompt, jnp.stack(gs.S_out), jnp.stack(gs.buf_out))
```
